```python
import math
import jax, jax.numpy as jnp
from jax import lax
import numpy as np

D_MODEL = 1024
BATCH = 2
SEQ = 16384
DEPTH = 1
DEC_BATCH = 128
DEC_SEQ = 1
PAST_LEN = 8192
PAGE_SIZE = 128

N_HEADS = 8
HEAD_DIM = 64
ATTN_WIDTH = N_HEADS * HEAD_DIM
MOBA_BLOCK = 256
MOBA_TOPK = 3
N_SEL = MOBA_TOPK + 1
Q_BLOCK = 128
ATTN_SCALE = HEAD_DIM ** -0.5
REL_BUCKETS = 32
REL_MAX_DIST = 4096
SSM_GROUPS = 32
SSM_GROUP_CH = 16
SSM_WIDTH = SSM_GROUPS * SSM_GROUP_CH
SSM_STATE = 64
DT_MIN = 1e-3
DT_MAX = 1e-1
RMS_EPS = 1e-6
IN_WIDTHS = (ATTN_WIDTH, ATTN_WIDTH, ATTN_WIDTH, ATTN_WIDTH, SSM_WIDTH, SSM_WIDTH, D_MODEL, D_MODEL)
IN_WIDTH = 4 * ATTN_WIDTH + 2 * SSM_WIDTH + 2 * D_MODEL

kernel_name = "moba_s5_gated_hybrid_step"


def _rms_norm(x, g):
    x32 = x.astype(jnp.float32)
    y = x32 * lax.rsqrt(jnp.mean(x32 * x32, axis=-1, keepdims=True) + RMS_EPS)
    return (y * g.astype(jnp.float32)).astype(x.dtype)


def _rel_bucket(dist):
    n = jnp.maximum(dist, 0)
    max_exact = REL_BUCKETS // 2
    nf = jnp.maximum(n, 1).astype(jnp.float32)
    large = max_exact + (jnp.log(nf / max_exact)
                         * ((REL_BUCKETS - max_exact) / math.log(REL_MAX_DIST / max_exact))).astype(jnp.int32)
    return jnp.where(n < max_exact, n, jnp.minimum(large, REL_BUCKETS - 1))


def _select_blocks(q, block_means, q_pos):
    nb = block_means.shape[2]
    own = q_pos // MOBA_BLOCK
    s = jnp.einsum("bhqd,bhnd->bhqn", q.astype(jnp.float32), block_means)
    fully_past = jnp.arange(nb)[None, :] < own[:, None]
    s = jnp.where(fully_past, s, -jnp.inf)
    if nb < MOBA_TOPK:
        s = jnp.pad(s, ((0, 0), (0, 0), (0, 0), (0, MOBA_TOPK - nb)), constant_values=-jnp.inf)
    _, top = lax.top_k(s, MOBA_TOPK)
    valid = top < own[:, None]
    top = jnp.where(valid, top, own[:, None])
    own_b = jnp.broadcast_to(own[:, None], top.shape[:-1] + (1,))
    idx = jnp.concatenate([top, own_b], axis=-1)
    valid = jnp.concatenate([valid, jnp.ones_like(valid[..., :1])], axis=-1)
    return idx, valid


def _moba_attend(q, q_pos, idx, valid, k_sel, v_sel, rel_bias):
    k_pos = idx[..., None] * MOBA_BLOCK + jnp.arange(MOBA_BLOCK, dtype=jnp.int32)
    rel = q_pos[:, None, None] - k_pos
    mask = valid[..., None] & (rel >= 0)
    logits = jnp.einsum("bhqd,bhqnkd->bhqnk", q, k_sel,
                        preferred_element_type=jnp.float32) * ATTN_SCALE
    head = jnp.arange(N_HEADS)[None, :, None, None, None]
    bias = rel_bias.astype(jnp.float32).T[head, _rel_bucket(rel)]
    logits = jnp.where(mask, logits + bias, -jnp.inf)
    b, h, nq = logits.shape[:3]
    p = jax.nn.softmax(logits.reshape(b, h, nq, N_SEL * MOBA_BLOCK), axis=-1).reshape(logits.shape)
    return jnp.einsum("bhqnk,bhqnkd->bhqd", p.astype(v_sel.dtype), v_sel)


def _moba_prompt(q, k, v, rel_bias):
    bsz, seq = q.shape[:2]
    nb = -(-seq // MOBA_BLOCK)
    pad = ((0, 0), (0, nb * MOBA_BLOCK - seq), (0, 0), (0, 0))

    def to_blocks(t):
        return jnp.pad(t, pad).reshape(bsz, nb, MOBA_BLOCK, N_HEADS, HEAD_DIM).transpose(0, 3, 1, 2, 4)

    kb, vb = to_blocks(k), to_blocks(v)
    means = jnp.mean(kb.astype(jnp.float32), axis=3)
    n_qb = seq // Q_BLOCK
    qc = q.reshape(bsz, n_qb, Q_BLOCK, N_HEADS, HEAD_DIM).transpose(1, 0, 3, 2, 4)
    bi = jnp.arange(bsz)[:, None, None, None]
    hi = jnp.arange(N_HEADS)[None, :, None, None]

    def one(args):
        qblk, c = args
        q_pos = c * Q_BLOCK + jnp.arange(Q_BLOCK)
        idx, valid = _select_blocks(qblk, means, q_pos)
        return _moba_attend(qblk, q_pos, idx, valid, kb[bi, hi, idx], vb[bi, hi, idx], rel_bias)

    out = lax.map(one, (qc, jnp.arange(n_qb)))
    return out.transpose(1, 0, 3, 2, 4).reshape(bsz, seq, ATTN_WIDTH)


def _moba_sample(q, k_new, v_new, cache_k, cache_v, layer, page_sums, page_table, rel_bias):
    bsz, s_new = q.shape[:2]
    n_pages = page_table.shape[1]
    past_len = n_pages * PAGE_SIZE
    n_new = -(-s_new // PAGE_SIZE)
    ppb = MOBA_BLOCK // PAGE_SIZE
    pad = ((0, 0), (0, n_new * PAGE_SIZE - s_new), (0, 0), (0, 0))
    kp_new = jnp.pad(k_new, pad).reshape(bsz, n_new, PAGE_SIZE, N_HEADS, HEAD_DIM)
    vp_new = jnp.pad(v_new, pad).reshape(bsz, n_new, PAGE_SIZE, N_HEADS, HEAD_DIM)
    sums = jnp.concatenate([page_sums[page_table],
                            jnp.sum(kp_new, axis=2, dtype=jnp.float32)], axis=1)
    n_tot = n_pages + n_new
    nb = -(-n_tot // ppb)
    sums = jnp.pad(sums, ((0, 0), (0, nb * ppb - n_tot), (0, 0), (0, 0)))
    means = sums.reshape(bsz, nb, ppb, N_HEADS, HEAD_DIM).sum(2).transpose(0, 2, 1, 3) / MOBA_BLOCK
    q_pos = past_len + jnp.arange(s_new)
    qh = q.transpose(0, 2, 1, 3)
    idx, valid = _select_blocks(qh, means, q_pos)
    pages = idx[..., None] * ppb + jnp.arange(ppb)
    cached = pages < n_pages
    bi = jnp.arange(bsz)[:, None, None, None, None]
    hi = jnp.arange(N_HEADS)[None, :, None, None, None]
    phys = page_table[bi, jnp.minimum(pages, n_pages - 1)]
    new_i = jnp.clip(pages - n_pages, 0, n_new - 1)

    def gather(cache, new_pages):
        from_pool = cache[layer, phys, :, hi]
        from_new = new_pages[bi, new_i, :, hi]
        g = jnp.where(cached[..., None, None], from_pool, from_new)
        return g.reshape(bsz, N_HEADS, s_new, N_SEL, MOBA_BLOCK, HEAD_DIM)

    out = _moba_attend(qh, q_pos, idx, valid, gather(cache_k, kp_new), gather(cache_v, vp_new), rel_bias)
    return out.transpose(0, 2, 1, 3).reshape(bsz, s_new, ATTN_WIDTH)


def _s5(u, s0_re, s0_im, lam_re, lam_im, log_dt, b_re, b_im, c_re, c_im, d_skip):
    bsz, seq = u.shape[:2]
    u32 = u.astype(jnp.float32).reshape(bsz, seq, SSM_GROUPS, SSM_GROUP_CH)
    lam = lax.complex(lam_re.astype(jnp.float32), lam_im.astype(jnp.float32))
    dt = jnp.exp(log_dt.astype(jnp.float32))[:, None]
    lam_bar = jnp.exp(lam * dt)
    b_bar = ((lam_bar - 1.0) / lam)[..., None] * lax.complex(b_re.astype(jnp.float32),
                                                            b_im.astype(jnp.float32))
    bu = jnp.einsum("blgh,gph->blgp", u32.astype(jnp.complex64), b_bar)
    a = jnp.broadcast_to(lam_bar, bu.shape)

    def combine(e1, e2):
        a1, b1 = e1
        a2, b2 = e2
        return a1 * a2, a2 * b1 + b2

    a_cum, xs = lax.associative_scan(combine, (a, bu), axis=1)
    s0 = lax.complex(s0_re.astype(jnp.float32), s0_im.astype(jnp.float32))
    xs = xs + a_cum * s0[:, None]
    c = lax.complex(c_re.astype(jnp.float32), c_im.astype(jnp.float32))
    y = jnp.real(jnp.einsum("blgp,ghp->blgh", xs, c)) \
        + d_skip.astype(jnp.float32).reshape(SSM_GROUPS, SSM_GROUP_CH) * u32
    last = xs[:, -1]
    return y.reshape(bsz, seq, SSM_WIDTH), jnp.real(last), jnp.imag(last)


def _project_in(x, norm_g, w_in, q_norm_g, k_norm_g):
    bsz, seq = x.shape[:2]
    h = _rms_norm(x, norm_g)
    z = jnp.einsum("bld,de->ble", h, w_in)
    splits = [int(s) for s in np.cumsum(IN_WIDTHS)[:-1]]
    q, k, v, gate_a, u, gate_b, mg_a, mg_b = jnp.split(z, splits, axis=-1)
    q = _rms_norm(q.reshape(bsz, seq, N_HEADS, HEAD_DIM), q_norm_g)
    k = _rms_norm(k.reshape(bsz, seq, N_HEADS, HEAD_DIM), k_norm_g)
    v = v.reshape(bsz, seq, N_HEADS, HEAD_DIM)
    return q, k, v, gate_a, u, gate_b, mg_a, mg_b


def _project_out(x, attn, ssm_y, gate_a, gate_b, mg_a, mg_b, w_glu, b_glu, w_branch_a, w_branch_b, w_out):
    za = jax.nn.gelu(ssm_y).astype(x.dtype)
    glu = jnp.einsum("ble,ef->blf", za, w_glu) + b_glu
    glu_a, glu_b = jnp.split(glu, 2, axis=-1)
    ssm_out = glu_a * jax.nn.sigmoid(glu_b) * jax.nn.silu(gate_b)
    attn_out = attn * jax.nn.silu(gate_a)
    ya = jnp.einsum("ble,ed->bld", attn_out, w_branch_a)
    yb = jnp.einsum("ble,ed->bld", ssm_out, w_branch_b)
    merged = jax.nn.sigmoid(mg_a) * ya + jax.nn.sigmoid(mg_b) * yb
    return x + jnp.einsum("bld,de->ble", merged, w_out)


def setup_inputs(seed: int = 0) -> dict:
    key = jax.random.key(seed)
    ks = iter(jax.random.split(key, 32))
    f32 = jnp.float32

    def nrm(shape, scale):
        return scale * jax.random.normal(next(ks), shape, f32)

    n_pages = PAST_LEN // PAGE_SIZE
    n_pool = (DEC_BATCH * n_pages * 5) // 4
    pool_shape = (DEPTH, n_pool, PAGE_SIZE, N_HEADS, HEAD_DIM)
    st_shape = (DEPTH, DEC_BATCH, SSM_GROUPS, SSM_STATE)
    return {
        "x_prompt": nrm((BATCH, SEQ, D_MODEL), 1.0),
        "x_sample": nrm((DEC_BATCH, DEC_SEQ, D_MODEL), 1.0),
        "cache_k": nrm(pool_shape, 1.0),
        "cache_v": nrm(pool_shape, 1.0),
        "state_ssm_re": nrm(st_shape, 0.2),
        "state_ssm_im": nrm(st_shape, 0.2),
        "page_table": jax.random.permutation(next(ks), n_pool)[: DEC_BATCH * n_pages]
                      .reshape(DEC_BATCH, n_pages).astype(jnp.int32),
        "rel_bias": nrm((REL_BUCKETS, N_HEADS), 0.5),
        "norm_g": 1.0 + nrm((DEPTH, D_MODEL), 0.02),
        "w_in": nrm((DEPTH, D_MODEL, IN_WIDTH), D_MODEL ** -0.5),
        "q_norm_g": 1.0 + nrm((DEPTH, HEAD_DIM), 0.02),
        "k_norm_g": 1.0 + nrm((DEPTH, HEAD_DIM), 0.02),
        "lam_re": -0.5 + nrm((DEPTH, SSM_GROUPS, SSM_STATE), 0.01),
        "lam_im": jnp.pi * jnp.arange(SSM_STATE, dtype=f32) + nrm((DEPTH, SSM_GROUPS, SSM_STATE), 0.01),
        "log_dt": jax.random.uniform(next(ks), (DEPTH, SSM_GROUPS), f32,
                                     minval=math.log(DT_MIN), maxval=math.log(DT_MAX)),
        "b_re": nrm((DEPTH, SSM_GROUPS, SSM_STATE, SSM_GROUP_CH), (2 * SSM_GROUP_CH) ** -0.5),
        "b_im": nrm((DEPTH, SSM_GROUPS, SSM_STATE, SSM_GROUP_CH), (2 * SSM_GROUP_CH) ** -0.5),
        "c_re": nrm((DEPTH, SSM_GROUPS, SSM_GROUP_CH, SSM_STATE), SSM_STATE ** -0.5),
        "c_im": nrm((DEPTH, SSM_GROUPS, SSM_GROUP_CH, SSM_STATE), SSM_STATE ** -0.5),
        "d_skip": nrm((DEPTH, SSM_WIDTH), 1.0),
        "w_glu": nrm((DEPTH, SSM_WIDTH, 2 * SSM_WIDTH), SSM_WIDTH ** -0.5),
        "b_glu": nrm((DEPTH, 2 * SSM_WIDTH), 0.01),
        "w_branch_a": nrm((DEPTH, ATTN_WIDTH, D_MODEL), ATTN_WIDTH ** -0.5),
        "w_branch_b": nrm((DEPTH, SSM_WIDTH, D_MODEL), SSM_WIDTH ** -0.5),
        "w_out": nrm((DEPTH, D_MODEL, D_MODEL), D_MODEL ** -0.5),
    }


def reference(x_prompt, x_sample, cache_k, cache_v, state_ssm_re, state_ssm_im, page_table,
              rel_bias, norm_g, w_in, q_norm_g, k_norm_g, lam_re, lam_im, log_dt,
              b_re, b_im, c_re, c_im, d_skip, w_glu, b_glu, w_branch_a, w_branch_b, w_out):
    page_key_sums = jnp.sum(cache_k, axis=2, dtype=jnp.float32)
    hp, hs = x_prompt, x_sample
    nk_p, nv_p, nre_p, nim_p = [], [], [], []
    nk_s, nv_s, nre_s, nim_s = [], [], [], []
    for l in range(DEPTH):
        ssm_w = (lam_re[l], lam_im[l], log_dt[l], b_re[l], b_im[l], c_re[l], c_im[l], d_skip[l])
        out_w = (w_glu[l], b_glu[l], w_branch_a[l], w_branch_b[l], w_out[l])
        q, k, v, ga, u, gb, ma, mb = _project_in(hp, norm_g[l], w_in[l], q_norm_g[l], k_norm_g[l])
        attn = _moba_prompt(q, k, v, rel_bias)
        s0 = jnp.zeros((hp.shape[0], SSM_GROUPS, SSM_STATE), jnp.float32)
        ssm_y, s_re, s_im = _s5(u, s0, s0, *ssm_w)
        hp = _project_out(hp, attn, ssm_y, ga, gb, ma, mb, *out_w)
        nk_p.append(k); nv_p.append(v); nre_p.append(s_re); nim_p.append(s_im)
        q, k, v, ga, u, gb, ma, mb = _project_in(hs, norm_g[l], w_in[l], q_norm_g[l], k_norm_g[l])
        attn = _moba_sample(q, k, v, cache_k, cache_v, l, page_key_sums[l], page_table, rel_bias)
        ssm_y, s_re, s_im = _s5(u, state_ssm_re[l], state_ssm_im[l], *ssm_w)
        hs = _project_out(hs, attn, ssm_y, ga, gb, ma, mb, *out_w)
        nk_s.append(k); nv_s.append(v); nre_s.append(s_re); nim_s.append(s_im)
    return (hp, hs, jnp.stack(nk_p), jnp.stack(nv_p), jnp.stack(nre_p), jnp.stack(nim_p),
            jnp.stack(nk_s), jnp.stack(nv_s), jnp.stack(nre_s), jnp.stack(nim_s))
```

```python
import functools
import math

import jax
import jax.numpy as jnp
from jax import lax
from jax.experimental import pallas as pl
from jax.experimental.pallas import tpu as pltpu

F32 = jnp.float32
BF16 = jnp.bfloat16
HIGHEST = lax.Precision.HIGHEST

MOBA_BLOCK = 256
MOBA_TOPK = 3
REL_BUCKETS = 32
REL_MAX_DIST = 4096
RMS_EPS = 1e-6
LANES = 128
VMEM_LIMIT = 56 * 1024 * 1024

NEG_INF = float("-inf")


def _cparams(n_axes, vmem=VMEM_LIMIT):
    return pltpu.CompilerParams(dimension_semantics=("arbitrary",) * n_axes, vmem_limit_bytes=vmem)


def _const_spec(shape, n_grid):
    zeros = (0,) * len(shape)
    if n_grid == 1:
        imap = lambda i: zeros
    elif n_grid == 2:
        imap = lambda i, j: zeros
    else:
        imap = lambda i, j, k: zeros
    return pl.BlockSpec(shape, imap, pipeline_mode=pl.Buffered(1))


def _proj_in_kernel(x_ref, g_ref, w_ref, qg_ref, kg_ref, bd_ref, *out_refs, aw, sw, dm, head_dim, with_ksum):
    if with_ksum:
        (q_ref, k_ref, v_ref, kb_ref, vb_ref, ksum_ref, sga_ref, u_ref, sgb_ref, sma_ref, smb_ref) = out_refs
    else:
        (q_ref, k_ref, v_ref, kb_ref, vb_ref, sga_ref, u_ref, sgb_ref, sma_ref, smb_ref) = out_refs
        ksum_ref = None
    x = x_ref[...]
    ms = jnp.mean(x * x, axis=-1, keepdims=True)
    h = (x * lax.rsqrt(ms + RMS_EPS) * g_ref[...]).astype(BF16)

    def seg(a, b):
        return jnp.dot(h, w_ref[:, a:b], preferred_element_type=F32)

    def headnorm(z, gain):
        ss = jnp.dot((z * z).astype(BF16), bd_ref[...], preferred_element_type=F32)
        return z * lax.rsqrt(ss * (1.0 / head_dim) + RMS_EPS) * gain

    o = 0
    q = headnorm(seg(o, o + aw), qg_ref[...])
    q_ref[...] = q.astype(q_ref.dtype)
    o += aw
    k = headnorm(seg(o, o + aw), kg_ref[...])
    k_ref[...] = k
    kb_ref[...] = k.astype(BF16)
    if ksum_ref is not None:
        ksum_ref[0] = jnp.sum(k, axis=0, keepdims=True)
    o += aw
    v = seg(o, o + aw)
    v_ref[...] = v
    vb_ref[...] = v.astype(BF16)
    o += aw
    sga_ref[...] = jax.nn.silu(seg(o, o + aw)).astype(sga_ref.dtype)
    o += aw
    u_ref[...] = seg(o, o + sw).astype(u_ref.dtype)
    o += sw
    sgb_ref[...] = jax.nn.silu(seg(o, o + sw)).astype(sgb_ref.dtype)
    o += sw
    sma_ref[...] = jax.nn.sigmoid(seg(o, o + dm)).astype(sma_ref.dtype)
    o += dm
    smb_ref[...] = jax.nn.sigmoid(seg(o, o + dm)).astype(smb_ref.dtype)


def _proj_in(x2, g, w_bf, qg_t, kg_t, bd, *, tm, aw, sw, head_dim, with_ksum, q_dtype, act_dtype):
    t, dm = x2.shape
    n_tiles = t // tm
    row = lambda w: pl.BlockSpec((tm, w), lambda i: (i, 0))
    out_shape = [
        jax.ShapeDtypeStruct((t, aw), q_dtype),
        jax.ShapeDtypeStruct((t, aw), F32),
        jax.ShapeDtypeStruct((t, aw), F32),
        jax.ShapeDtypeStruct((t, aw), BF16),
        jax.ShapeDtypeStruct((t, aw), BF16),
    ]
    out_specs = [row(aw)] * 5
    if with_ksum:
        out_shape.append(jax.ShapeDtypeStruct((n_tiles, 1, aw), F32))
        out_specs.append(pl.BlockSpec((1, 1, aw), lambda i: (i, 0, 0)))
    out_shape += [
        jax.ShapeDtypeStruct((t, aw), act_dtype),
        jax.ShapeDtypeStruct((t, sw), act_dtype),
        jax.ShapeDtypeStruct((t, sw), act_dtype),
        jax.ShapeDtypeStruct((t, dm), act_dtype),
        jax.ShapeDtypeStruct((t, dm), act_dtype),
    ]
    out_specs += [row(aw), row(sw), row(sw), row(dm), row(dm)]
    kern = functools.partial(_proj_in_kernel, aw=aw, sw=sw, dm=dm, head_dim=head_dim, with_ksum=with_ksum)
    return pl.pallas_call(
        kern,
        grid=(n_tiles,),
        in_specs=[
            row(dm),
            _const_spec((1, dm), 1),
            _const_spec(w_bf.shape, 1),
            _const_spec((1, aw), 1),
            _const_spec((1, aw), 1),
            _const_spec((aw, aw), 1),
        ],
        out_specs=out_specs,
        out_shape=out_shape,
        compiler_params=_cparams(1),
        name="proj_in",
    )(x2, g, w_bf, qg_t, kg_t, bd)


def _rel_bucket(dist):
    n = jnp.maximum(dist, 0)
    max_exact = REL_BUCKETS // 2
    nf = jnp.maximum(n, 1).astype(F32)
    large = max_exact + (
        jnp.log(nf / max_exact) * ((REL_BUCKETS - max_exact) / math.log(REL_MAX_DIST / max_exact))
    ).astype(jnp.int32)
    return jnp.where(n < max_exact, n, jnp.minimum(large, REL_BUCKETS - 1))


def _bias_lookup(bucket, relb_ref, head):
    val = jnp.zeros(bucket.shape, F32)
    for b in range(REL_BUCKETS):
        val = jnp.where(bucket == b, relb_ref[b, head], val)
    return val


def _bias_tiles_kernel(relb_ref, o_ref):
    h = pl.program_id(0)
    d = pl.program_id(1)
    key = lax.broadcasted_iota(jnp.int32, (MOBA_BLOCK, MOBA_BLOCK), 0)
    qry = lax.broadcasted_iota(jnp.int32, (MOBA_BLOCK, MOBA_BLOCK), 1)
    dist = d * MOBA_BLOCK + qry - key
    o_ref[0, 0] = _bias_lookup(_rel_bucket(dist), relb_ref, h)


def _bias_tiles(rel_bias, n_heads, n_dist):
    return pl.pallas_call(
        _bias_tiles_kernel,
        grid=(n_heads, n_dist),
        in_specs=[pl.BlockSpec(memory_space=pltpu.SMEM)],
        out_specs=pl.BlockSpec((1, 1, MOBA_BLOCK, MOBA_BLOCK), lambda h, d: (h, d, 0, 0)),
        out_shape=jax.ShapeDtypeStruct((n_heads, n_dist, MOBA_BLOCK, MOBA_BLOCK), F32),
        compiler_params=_cparams(2),
        name="bias_tiles",
    )(rel_bias)


def _select_topk(s):
    rowi = lax.broadcasted_iota(jnp.int32, s.shape, 0)
    sel = jnp.zeros(s.shape, jnp.bool_)
    for _ in range(MOBA_TOPK):
        mx = jnp.max(s, axis=0, keepdims=True)
        cand = jnp.where(s == mx, rowi, s.shape[0])
        idx = jnp.min(cand, axis=0, keepdims=True)
        hit = (rowi == idx) & (mx > NEG_INF)
        sel = sel | hit
        s = jnp.where(hit, NEG_INF, s)
    return sel


def _attn_kernel(q_ref, kb_ref, vb_ref, ksum_ref, bias_ref, o_ref, vt_ref, qt_ref, sel_ref, m_ref, l_ref, acc_ref,
                 *, head_dim, n_dist, scale):
    t = pl.program_id(2)
    blk = MOBA_BLOCK
    n_blocks = vt_ref.shape[0]
    pair = 2 * head_dim

    @pl.when(t == 0)
    def _():
        def xpose(c, carry):
            v = vb_ref[0, pl.ds(pl.multiple_of(c * blk, blk), blk), :].astype(F32)
            vt_ref[c] = v.T.astype(BF16)
            return carry

        lax.fori_loop(0, n_blocks, xpose, 0)

    qt = q_ref[0].astype(F32).T
    feat = lax.broadcasted_iota(jnp.int32, (pair, blk), 0)
    means = ksum_ref[0] * (1.0 / blk)
    blk_i = lax.broadcasted_iota(jnp.int32, (n_blocks, blk), 0)
    key_i = lax.broadcasted_iota(jnp.int32, (blk, blk), 0)
    qry_i = lax.broadcasted_iota(jnp.int32, (blk, blk), 1)
    k_own = kb_ref[0, pl.ds(pl.multiple_of(t * blk, blk), blk), :]
    for h in range(2):
        in_head = (feat >= h * head_dim) & (feat < (h + 1) * head_dim)
        qt_h = jnp.where(in_head, qt, 0.0)
        qt_ref[h] = (qt_h * scale).astype(BF16)
        sc = jnp.dot(means, qt_h, preferred_element_type=F32, precision=HIGHEST)
        sc = jnp.where(blk_i < t, sc, NEG_INF)
        sel_ref[h] = _select_topk(sc).astype(F32)
        s = jnp.dot(k_own, qt_ref[h], preferred_element_type=F32) + bias_ref[h, 0]
        s = jnp.where(key_i <= qry_i, s, NEG_INF)
        m = jnp.max(s, axis=0, keepdims=True)
        p = jnp.exp(s - m)
        m_ref[h] = m
        l_ref[h] = jnp.sum(p, axis=0, keepdims=True)
        acc_ref[h] = jnp.dot(vt_ref[t, h * head_dim:(h + 1) * head_dim, :], p.astype(BF16),
                             preferred_element_type=F32)

    def past(j, carry):
        kj = kb_ref[0, pl.ds(pl.multiple_of(j * blk, blk), blk), :]
        d = jnp.minimum(t - j, n_dist - 1)
        for h in range(2):
            s = jnp.dot(kj, qt_ref[h], preferred_element_type=F32) + bias_ref[h, d]
            picked = sel_ref[h, pl.ds(j, 1), :] > 0.5
            s = jnp.where(picked, s, NEG_INF)
            m_old = m_ref[h]
            m_new = jnp.maximum(m_old, jnp.max(s, axis=0, keepdims=True))
            alpha = jnp.exp(m_old - m_new)
            p = jnp.exp(s - m_new)
            l_ref[h] = l_ref[h] * alpha + jnp.sum(p, axis=0, keepdims=True)
            acc_ref[h] = acc_ref[h] * alpha + jnp.dot(vt_ref[j, h * head_dim:(h + 1) * head_dim, :],
                                                      p.astype(BF16), preferred_element_type=F32)
            m_ref[h] = m_new
        return carry

    lax.fori_loop(0, t, past, 0)

    out_t = jnp.concatenate([acc_ref[0] / l_ref[0], acc_ref[1] / l_ref[1]], axis=0)
    o_ref[0] = out_t.T.astype(o_ref.dtype)


def _attention(q_bf, k_bf, v_bf, ksum, bias, *, head_dim):
    b, l, aw = q_bf.shape
    pair = 2 * head_dim
    assert pair == LANES and aw % pair == 0 and l % MOBA_BLOCK == 0
    n_pairs = aw // pair
    n_blocks = l // MOBA_BLOCK
    n_dist = bias.shape[1]
    kern = functools.partial(_attn_kernel, head_dim=head_dim, n_dist=n_dist, scale=head_dim ** -0.5)
    one = pl.Buffered(1)
    return pl.pallas_call(
        kern,
        grid=(b, n_pairs, n_blocks),
        in_specs=[
            pl.BlockSpec((1, MOBA_BLOCK, pair), lambda bi, hp, t: (bi, t, hp)),
            pl.BlockSpec((1, l, pair), lambda bi, hp, t: (bi, 0, hp), pipeline_mode=one),
            pl.BlockSpec((1, l, pair), lambda bi, hp, t: (bi, 0, hp), pipeline_mode=one),
            pl.BlockSpec((1, n_blocks, pair), lambda bi, hp, t: (bi, 0, hp), pipeline_mode=one),
            pl.BlockSpec((2, n_dist, MOBA_BLOCK, MOBA_BLOCK), lambda bi, hp, t: (hp, 0, 0, 0), pipeline_mode=one),
        ],
        out_specs=pl.BlockSpec((1, MOBA_BLOCK, pair), lambda bi, hp, t: (bi, t, hp)),
        out_shape=jax.ShapeDtypeStruct((b, l, aw), BF16),
        scratch_shapes=[
            pltpu.VMEM((n_blocks, pair, MOBA_BLOCK), BF16),
            pltpu.VMEM((2, pair, MOBA_BLOCK), BF16),
            pltpu.VMEM((2, n_blocks, MOBA_BLOCK), F32),
            pltpu.VMEM((2, 1, MOBA_BLOCK), F32),
            pltpu.VMEM((2, 1, MOBA_BLOCK), F32),
            pltpu.VMEM((2, head_dim, MOBA_BLOCK), F32),
        ],
        compiler_params=_cparams(3),
        name="moba_attn",
    )(q_bf, k_bf, v_bf, ksum, bias)


def _ssm_prep_kernel(lre_ref, lim_ref, ldt_ref, are_ref, aim_ref, cre_ref, cim_ref):
    lre = lre_ref[...]
    lim = lim_ref[...]
    dt = jnp.exp(ldt_ref[...])
    mag = jnp.exp(lre * dt)
    a_re = mag * jnp.cos(lim * dt)
    a_im = mag * jnp.sin(lim * dt)
    are_ref[...] = a_re
    aim_ref[...] = a_im
    nr = a_re - 1.0
    den = lre * lre + lim * lim
    cre_ref[...] = (nr * lre + a_im * lim) / den
    cim_ref[...] = (a_im * lre - nr * lim) / den


def _ssm_prep(lam_re, lam_im, log_dt):
    g, p = lam_re.shape
    shp = jax.ShapeDtypeStruct((g, p), F32)
    return pl.pallas_call(_ssm_prep_kernel, out_shape=[shp] * 4, name="ssm_prep")(
        lam_re, lam_im, log_dt.reshape(g, 1))


def _block_diag_mask(rows, cols, row_group, col_group):
    r = lax.broadcasted_iota(jnp.int32, (rows, cols), 0) // row_group
    c = lax.broadcasted_iota(jnp.int32, (rows, cols), 1) // col_group
    return r == c


def _build_ssm_weights(coef_re_ref, coef_im_ref, bt_re_ref, bt_im_ref, ct_re_ref, ct_im_ref,
                       wb_re_ref, wb_im_ref, wc_re_ref, wc_im_ref, *, gch, nstate):
    halves, kh, nh = wb_re_ref.shape
    mask_b = _block_diag_mask(kh, nh, gch, nstate)
    mask_c = _block_diag_mask(nh, kh, nstate, gch)
    for i in range(halves):
        cr = coef_re_ref[i]
        ci = coef_im_ref[i]
        br = bt_re_ref[i * kh:(i + 1) * kh, :]
        bi = bt_im_ref[i * kh:(i + 1) * kh, :]
        wb_re_ref[i] = jnp.where(mask_b, cr * br - ci * bi, 0.0).astype(wb_re_ref.dtype)
        wb_im_ref[i] = jnp.where(mask_b, cr * bi + ci * br, 0.0).astype(wb_im_ref.dtype)
        wc_re_ref[i] = jnp.where(mask_c, ct_re_ref[i * nh:(i + 1) * nh, :], 0.0).astype(wc_re_ref.dtype)
        wc_im_ref[i] = jnp.where(mask_c, -ct_im_ref[i * nh:(i + 1) * nh, :], 0.0).astype(wc_im_ref.dtype)


def _ssm_in(u, wb_re_ref, wb_im_ref, precision=None):
    halves, kh, _ = wb_re_ref.shape
    re, im = [], []
    for i in range(halves):
        ui = u[:, i * kh:(i + 1) * kh]
        re.append(jnp.dot(ui, wb_re_ref[i], preferred_element_type=F32, precision=precision))
        im.append(jnp.dot(ui, wb_im_ref[i], preferred_element_type=F32, precision=precision))
    return jnp.concatenate(re, axis=1), jnp.concatenate(im, axis=1)


def _ssm_out(x_re, x_im, wc_re_ref, wc_im_ref, precision=None):
    halves, nh, _ = wc_re_ref.shape
    ys = []
    for i in range(halves):
        ys.append(jnp.dot(x_re[:, i * nh:(i + 1) * nh], wc_re_ref[i], preferred_element_type=F32, precision=precision)
                  + jnp.dot(x_im[:, i * nh:(i + 1) * nh], wc_im_ref[i], preferred_element_type=F32,
                            precision=precision))
    return jnp.concatenate(ys, axis=1)


def _glu_tail(y, u32, dskip_ref, wglu_ref, bglu_ref, sgb, sw, precision=None):
    y = y + dskip_ref[...] * u32
    za = jax.nn.gelu(y).astype(wglu_ref.dtype)
    glu = jnp.dot(za, wglu_ref[...], preferred_element_type=F32, precision=precision) + bglu_ref[...]
    return glu[:, :sw] * jax.nn.sigmoid(glu[:, sw:]) * sgb


def _s5_prompt_kernel(u_ref, sgb_ref, are_ref, aim_ref, coef_re_ref, coef_im_ref, bt_re_ref, bt_im_ref,
                      ct_re_ref, ct_im_ref, dskip_ref, wglu_ref, bglu_ref,
                      out_ref, fre_ref, fim_ref,
                      wb_re_ref, wb_im_ref, wc_re_ref, wc_im_ref, bu_re_ref, bu_im_ref, xs_re_ref, xs_im_ref,
                      st_re_ref, st_im_ref, *, gch, nstate, sw):
    b = pl.program_id(0)
    t = pl.program_id(1)
    tl = u_ref.shape[1]

    @pl.when((b == 0) & (t == 0))
    def _():
        _build_ssm_weights(coef_re_ref, coef_im_ref, bt_re_ref, bt_im_ref, ct_re_ref, ct_im_ref,
                           wb_re_ref, wb_im_ref, wc_re_ref, wc_im_ref, gch=gch, nstate=nstate)

    @pl.when(t == 0)
    def _():
        st_re_ref[...] = jnp.zeros_like(st_re_ref)
        st_im_ref[...] = jnp.zeros_like(st_im_ref)

    u = u_ref[0]
    bu_re, bu_im = _ssm_in(u, wb_re_ref, wb_im_ref)
    bu_re_ref[...] = bu_re
    bu_im_ref[...] = bu_im
    a_re = are_ref[...]
    a_im = aim_ref[...]

    def step(i, carry):
        xr, xi = carry
        nr = a_re * xr - a_im * xi + bu_re_ref[pl.ds(i, 1), :]
        ni = a_re * xi + a_im * xr + bu_im_ref[pl.ds(i, 1), :]
        xs_re_ref[pl.ds(i, 1), :] = nr
        xs_im_ref[pl.ds(i, 1), :] = ni
        return nr, ni

    xr, xi = lax.fori_loop(0, tl, step, (st_re_ref[...], st_im_ref[...]), unroll=8)
    st_re_ref[...] = xr
    st_im_ref[...] = xi
    fre_ref[0] = xr
    fim_ref[0] = xi

    y = _ssm_out(xs_re_ref[...].astype(BF16), xs_im_ref[...].astype(BF16), wc_re_ref, wc_im_ref)
    out = _glu_tail(y, u.astype(F32), dskip_ref, wglu_ref, bglu_ref, sgb_ref[0].astype(F32), sw)
    out_ref[0] = out.astype(out_ref.dtype)


def _s5_prompt(u, sgb, a_re, a_im, coef_re, coef_im, bt_re, bt_im, ct_re, ct_im, d_skip, w_glu, b_glu, *, tl,
               gch, nstate):
    b, l, sw = u.shape
    ns = a_re.shape[1]
    halves = 2
    kh = sw // halves
    nh = ns // halves
    kern = functools.partial(_s5_prompt_kernel, gch=gch, nstate=nstate, sw=sw)
    tile = pl.BlockSpec((1, tl, sw), lambda bi, t: (bi, t, 0))
    c2 = lambda shape: _const_spec(shape, 2)
    return pl.pallas_call(
        kern,
        grid=(b, l // tl),
        in_specs=[tile, tile, c2((1, ns)), c2((1, ns)), c2((halves, 1, nh)), c2((halves, 1, nh)),
                  c2(bt_re.shape), c2(bt_im.shape), c2(ct_re.shape), c2(ct_im.shape),
                  c2((1, sw)), c2(w_glu.shape), c2((1, 2 * sw))],
        out_specs=[tile,
                   pl.BlockSpec((1, 1, ns), lambda bi, t: (bi, 0, 0)),
                   pl.BlockSpec((1, 1, ns), lambda bi, t: (bi, 0, 0))],
        out_shape=[jax.ShapeDtypeStruct((b, l, sw), BF16),
                   jax.ShapeDtypeStruct((b, 1, ns), F32),
                   jax.ShapeDtypeStruct((b, 1, ns), F32)],
        scratch_shapes=[
            pltpu.VMEM((halves, kh, nh), BF16), pltpu.VMEM((halves, kh, nh), BF16),
            pltpu.VMEM((halves, nh, kh), BF16), pltpu.VMEM((halves, nh, kh), BF16),
            pltpu.VMEM((tl, ns), F32), pltpu.VMEM((tl, ns), F32),
            pltpu.VMEM((tl, ns), F32), pltpu.VMEM((tl, ns), F32),
            pltpu.VMEM((1, ns), F32), pltpu.VMEM((1, ns), F32),
        ],
        compiler_params=_cparams(2),
        name="s5_prompt",
    )(u, sgb, a_re, a_im, coef_re, coef_im, bt_re, bt_im, ct_re, ct_im, d_skip, w_glu, b_glu)


def _s5_sample_kernel(u_ref, sgb_ref, sre_ref, sim_ref, are_ref, aim_ref, coef_re_ref, coef_im_ref,
                      bt_re_ref, bt_im_ref, ct_re_ref, ct_im_ref, dskip_ref, wglu_ref, bglu_ref,
                      out_ref, nre_ref, nim_ref, wb_re_ref, wb_im_ref, wc_re_ref, wc_im_ref, *, gch, nstate, sw):
    _build_ssm_weights(coef_re_ref, coef_im_ref, bt_re_ref, bt_im_ref, ct_re_ref, ct_im_ref,
                       wb_re_ref, wb_im_ref, wc_re_ref, wc_im_ref, gch=gch, nstate=nstate)
    u = u_ref[...]
    bu_re, bu_im = _ssm_in(u, wb_re_ref, wb_im_ref, precision=HIGHEST)
    a_re = are_ref[...]
    a_im = aim_ref[...]
    s_re = sre_ref[...]
    s_im = sim_ref[...]
    x_re = a_re * s_re - a_im * s_im + bu_re
    x_im = a_re * s_im + a_im * s_re + bu_im
    nre_ref[...] = x_re
    nim_ref[...] = x_im
    y = _ssm_out(x_re, x_im, wc_re_ref, wc_im_ref, precision=HIGHEST)
    out_ref[...] = _glu_tail(y, u, dskip_ref, wglu_ref, bglu_ref, sgb_ref[...], sw, precision=HIGHEST)


def _s5_sample(u, sgb, s_re, s_im, a_re, a_im, coef_re, coef_im, bt_re, bt_im, ct_re, ct_im, d_skip, w_glu,
               b_glu, *, gch, nstate):
    n, sw = u.shape
    ns = a_re.shape[1]
    halves = 2
    kh = sw // halves
    nh = ns // halves
    kern = functools.partial(_s5_sample_kernel, gch=gch, nstate=nstate, sw=sw)
    return pl.pallas_call(
        kern,
        out_shape=[jax.ShapeDtypeStruct((n, sw), F32),
                   jax.ShapeDtypeStruct((n, ns), F32),
                   jax.ShapeDtypeStruct((n, ns), F32)],
        scratch_shapes=[
            pltpu.VMEM((halves, kh, nh), F32), pltpu.VMEM((halves, kh, nh), F32),
            pltpu.VMEM((halves, nh, kh), F32), pltpu.VMEM((halves, nh, kh), F32),
        ],
        compiler_params=pltpu.CompilerParams(vmem_limit_bytes=VMEM_LIMIT),
        name="s5_sample",
    )(u, sgb, s_re, s_im, a_re, a_im, coef_re, coef_im, bt_re, bt_im, ct_re, ct_im, d_skip, w_glu, b_glu)


def _proj_out_kernel(x_ref, attn_ref, sga_ref, ssm_ref, sma_ref, smb_ref, wa_ref, wb_ref, wo_ref, o_ref):
    attn_out = (attn_ref[...].astype(F32) * sga_ref[...].astype(F32)).astype(BF16)
    ya = jnp.dot(attn_out, wa_ref[...], preferred_element_type=F32)
    yb = jnp.dot(ssm_ref[...].astype(BF16), wb_ref[...], preferred_element_type=F32)
    merged = sma_ref[...].astype(F32) * ya + smb_ref[...].astype(F32) * yb
    o_ref[...] = x_ref[...] + jnp.dot(merged.astype(BF16), wo_ref[...], preferred_element_type=F32)


def _proj_out(x2, attn, sga, ssm, sma, smb, wa, wb, wo, *, tm):
    t, dm = x2.shape
    aw = attn.shape[1]
    sw = ssm.shape[1]
    row = lambda w: pl.BlockSpec((tm, w), lambda i: (i, 0))
    return pl.pallas_call(
        _proj_out_kernel,
        grid=(t // tm,),
        in_specs=[row(dm), row(aw), row(aw), row(sw), row(dm), row(dm),
                  _const_spec(wa.shape, 1), _const_spec(wb.shape, 1), _const_spec(wo.shape, 1)],
        out_specs=row(dm),
        out_shape=jax.ShapeDtypeStruct((t, dm), F32),
        compiler_params=_cparams(1),
        name="proj_out",
    )(x2, attn, sga, ssm, sma, smb, wa, wb, wo)


def _page_sums_kernel(c_ref, o_ref):
    o_ref[...] = jnp.sum(c_ref[...], axis=1)


def _page_sums(cache2, pages_per_step):
    n_pool, page, w = cache2.shape
    return pl.pallas_call(
        _page_sums_kernel,
        grid=(n_pool // pages_per_step,),
        in_specs=[pl.BlockSpec((pages_per_step, page, w), lambda i: (i, 0, 0))],
        out_specs=pl.BlockSpec((pages_per_step, w), lambda i: (i, 0)),
        out_shape=jax.ShapeDtypeStruct((n_pool, w), F32),
        compiler_params=_cparams(1),
        name="page_sums",
    )(cache2)


def _sample_select_kernel(pt_ref, q_ref, sums_ref, seg_ref, o_ref, blk_ref, *, n_pages, ppb, n_heads):
    b = pl.program_id(0)
    n_blocks = n_pages // ppb
    for j in range(n_blocks):
        acc = sums_ref[pl.ds(pt_ref[b, j * ppb], 1), :]
        for i in range(1, ppb):
            acc = acc + sums_ref[pl.ds(pt_ref[b, j * ppb + i], 1), :]
        blk_ref[pl.ds(j, 1), :] = acc
    means = blk_ref[...] * (1.0 / MOBA_BLOCK)
    prod = means * q_ref[0]
    sc = jnp.dot(prod, seg_ref[...], preferred_element_type=F32, precision=HIGHEST)
    rowi = lax.broadcasted_iota(jnp.int32, sc.shape, 0)
    out = jnp.full((8, LANES), -1, jnp.int32)
    outr = lax.broadcasted_iota(jnp.int32, (8, LANES), 0)
    for r in range(MOBA_TOPK):
        mx = jnp.max(sc, axis=0, keepdims=True)
        cand = jnp.where(sc == mx, rowi, n_blocks)
        idx = jnp.min(cand, axis=0, keepdims=True)
        ok = mx > NEG_INF
        out = jnp.where(outr == r, jnp.where(ok, idx, -1), out)
        sc = jnp.where((rowi == idx) & ok, NEG_INF, sc)
    o_ref[0] = out


def _sample_select(page_table, q, sums, seg, *, ppb, n_heads):
    bd, n_pages = page_table.shape
    w = q.shape[-1]
    n_blocks = n_pages // ppb
    kern = functools.partial(_sample_select_kernel, n_pages=n_pages, ppb=ppb, n_heads=n_heads)
    grid_spec = pltpu.PrefetchScalarGridSpec(
        num_scalar_prefetch=1,
        grid=(bd,),
        in_specs=[pl.BlockSpec((1, 1, w), lambda b, pt: (b, 0, 0)),
                  pl.BlockSpec(sums.shape, lambda b, pt: (0, 0), pipeline_mode=pl.Buffered(1)),
                  pl.BlockSpec(seg.shape, lambda b, pt: (0, 0), pipeline_mode=pl.Buffered(1))],
        out_specs=pl.BlockSpec((1, 8, LANES), lambda b, pt: (b, 0, 0)),
        scratch_shapes=[pltpu.VMEM((n_blocks, w), F32)],
    )
    return pl.pallas_call(
        kern,
        grid_spec=grid_spec,
        out_shape=jax.ShapeDtypeStruct((bd, 8, LANES), jnp.int32),
        compiler_params=_cparams(1),
        name="sample_select",
    )(page_table, q.reshape(bd, 1, w), sums, seg)


def _sample_attn_kernel(pt_ref, sel_ref, relb_ref, q_ref, kn_ref, vn_ref, ck_ref, cv_ref, o_ref,
                        kbuf, vbuf, sem, *, n_heads, head_dim, ppb, page, past_len, scale):
    b = pl.program_id(0)
    pair = 2 * head_dim
    n_sel = MOBA_TOPK

    def copies(h, r, i):
        j = sel_ref[b, h * n_sel + r]
        phys = pt_ref[b, jnp.maximum(j, 0) * ppb + i]
        col = (h // 2) * pair
        slot = (h * n_sel + r) * ppb + i
        ck = pltpu.make_async_copy(ck_ref.at[phys, :, pl.ds(col, pair)], kbuf.at[slot], sem.at[0, slot])
        cv = pltpu.make_async_copy(cv_ref.at[phys, :, pl.ds(col, pair)], vbuf.at[slot], sem.at[1, slot])
        return j, ck, cv

    for h in range(n_heads):
        for r in range(n_sel):
            for i in range(ppb):
                j, ck, cv = copies(h, r, i)

                @pl.when(j >= 0)
                def _():
                    ck.start()
                    cv.start()

    q = q_ref[0]
    lane = lax.broadcasted_iota(jnp.int32, (1, pair), 1)
    key_off = lax.broadcasted_iota(jnp.int32, (1, page), 1)
    outs = []
    for h in range(n_heads):
        col = (h // 2) * pair
        in_head = (lane >= (h % 2) * head_dim) & (lane < (h % 2 + 1) * head_dim)
        qh = jnp.where(in_head, q[:, col:col + pair], 0.0)
        q8 = jnp.broadcast_to(qh, (8, pair))
        s_self = jnp.sum(qh * kn_ref[0][:, col:col + pair], axis=-1, keepdims=True) * scale + relb_ref[0, h]
        logits = []
        for r in range(n_sel):
            j = sel_ref[b, h * n_sel + r]
            for i in range(ppb):
                slot = (h * n_sel + r) * ppb + i
                _, ck, cv = copies(h, r, i)

                @pl.when(j >= 0)
                def _():
                    ck.wait()
                    cv.wait()

                s = lax.dot_general(q8, kbuf[slot], (((1,), (1,)), ((), ())), preferred_element_type=F32,
                                    precision=HIGHEST)[0:1, :] * scale
                dist = past_len - (j * ppb + i) * page - key_off
                s = s + _bias_lookup(_rel_bucket(dist), relb_ref, h)
                logits.append(jnp.where(j >= 0, s, NEG_INF))
        m = s_self
        for s in logits:
            m = jnp.maximum(m, jnp.max(s, axis=-1, keepdims=True))
        p_self = jnp.exp(s_self - m)
        den = p_self
        acc = p_self * vn_ref[0][:, col:col + pair]
        for n, s in enumerate(logits):
            p = jnp.exp(s - m)
            den = den + jnp.sum(p, axis=-1, keepdims=True)
            p8 = jnp.broadcast_to(p, (8, page))
            vblk = jnp.where(sel_ref[b, h * n_sel + n // ppb] >= 0, vbuf[h * n_sel * ppb + n], 0.0)
            acc = acc + jnp.dot(p8, vblk, preferred_element_type=F32, precision=HIGHEST)[0:1, :]
        outs.append(jnp.where(in_head, acc / den, 0.0))
    rows = []
    for hp in range(n_heads // 2):
        rows.append(outs[2 * hp] + outs[2 * hp + 1])
    o_ref[0] = jnp.concatenate(rows, axis=1)


def _sample_attention(page_table, sel, rel_bias, q, k_new, v_new, cache_k2, cache_v2, *, n_heads, head_dim, ppb,
                      past_len):
    bd, w = q.shape
    _, page, _ = cache_k2.shape
    pair = 2 * head_dim
    n_slots = n_heads * MOBA_TOPK * ppb
    kern = functools.partial(_sample_attn_kernel, n_heads=n_heads, head_dim=head_dim, ppb=ppb, page=page,
                             past_len=past_len, scale=head_dim ** -0.5)
    row = pl.BlockSpec((1, 1, w), lambda b, pt, sl: (b, 0, 0))
    grid_spec = pltpu.PrefetchScalarGridSpec(
        num_scalar_prefetch=2,
        grid=(bd,),
        in_specs=[pl.BlockSpec(memory_space=pltpu.SMEM), row, row, row,
                  pl.BlockSpec(memory_space=pl.ANY), pl.BlockSpec(memory_space=pl.ANY)],
        out_specs=row,
        scratch_shapes=[pltpu.VMEM((n_slots, page, pair), F32), pltpu.VMEM((n_slots, page, pair), F32),
                        pltpu.SemaphoreType.DMA((2, n_slots))],
    )
    out = pl.pallas_call(
        kern,
        grid_spec=grid_spec,
        out_shape=jax.ShapeDtypeStruct((bd, 1, w), F32),
        compiler_params=_cparams(1),
        name="sample_attn",
    )(page_table, sel, rel_bias, q.reshape(bd, 1, w), k_new.reshape(bd, 1, w), v_new.reshape(bd, 1, w),
      cache_k2, cache_v2)
    return out.reshape(bd, w)


def _layer(x_prompt, x_sample, cache_k, cache_v, state_re, state_im, page_table, rel_bias, norm_g, w_in,
           q_norm_g, k_norm_g, lam_re, lam_im, log_dt, b_re, b_im, c_re, c_im, d_skip, w_glu, b_glu,
           w_branch_a, w_branch_b, w_out):
    bsz, seq, dm = x_prompt.shape
    bd, s_new, _ = x_sample.shape
    n_pool, page, n_heads, head_dim = cache_k.shape
    n_groups, n_state, gch = b_re.shape
    aw = n_heads * head_dim
    sw = n_groups * gch
    ns = n_groups * n_state
    n_pages = page_table.shape[1]
    past_len = n_pages * page
    ppb = MOBA_BLOCK // page
    assert s_new == 1 and seq % MOBA_BLOCK == 0 and past_len % MOBA_BLOCK == 0
    assert w_in.shape == (dm, 4 * aw + 2 * sw + 2 * dm)

    w_in_bf = w_in.astype(BF16)
    g_row = norm_g.reshape(1, dm)
    qg_t = jnp.tile(q_norm_g.reshape(1, head_dim), (1, n_heads))
    kg_t = jnp.tile(k_norm_g.reshape(1, head_dim), (1, n_heads))
    head_of = jnp.arange(aw) // head_dim
    bd_ones = (head_of[:, None] == head_of[None, :]).astype(BF16)
    halves = 2
    bt_re = jnp.tile(b_re.transpose(0, 2, 1).reshape(sw, n_state), (1, n_groups // halves))
    bt_im = jnp.tile(b_im.transpose(0, 2, 1).reshape(sw, n_state), (1, n_groups // halves))
    ct_re = jnp.tile(c_re.transpose(0, 2, 1).reshape(ns, gch), (1, n_groups // halves))
    ct_im = jnp.tile(c_im.transpose(0, 2, 1).reshape(ns, gch), (1, n_groups // halves))
    dsk = d_skip.reshape(1, sw)
    bgl = b_glu.reshape(1, 2 * sw)

    a_re, a_im, coef_re, coef_im = _ssm_prep(lam_re, lam_im, log_dt)
    a_re = a_re.reshape(1, ns)
    a_im = a_im.reshape(1, ns)
    coef_re = coef_re.reshape(halves, 1, ns // halves)
    coef_im = coef_im.reshape(halves, 1, ns // halves)

    wa_bf = w_branch_a.astype(BF16)
    wb_bf = w_branch_b.astype(BF16)
    wo_bf = w_out.astype(BF16)
    wglu_bf = w_glu.astype(BF16)

    xp2 = x_prompt.reshape(bsz * seq, dm)
    (q_p, k_p, v_p, kb_p, vb_p, ksum_p, sga_p, u_p, sgb_p, sma_p, smb_p) = _proj_in(
        xp2, g_row, w_in_bf, qg_t, kg_t, bd_ones, tm=MOBA_BLOCK, aw=aw, sw=sw, head_dim=head_dim,
        with_ksum=True, q_dtype=BF16, act_dtype=BF16)
    n_blocks = seq // MOBA_BLOCK
    n_dist = min(n_blocks, -(-(REL_MAX_DIST + MOBA_BLOCK - 1) // MOBA_BLOCK) + 1)
    bias = _bias_tiles(rel_bias, n_heads, n_dist)
    attn_p = _attention(q_p.reshape(bsz, seq, aw), kb_p.reshape(bsz, seq, aw), vb_p.reshape(bsz, seq, aw),
                        ksum_p.reshape(bsz, n_blocks, aw), bias, head_dim=head_dim)
    ssm_p, fre_p, fim_p = _s5_prompt(u_p.reshape(bsz, seq, sw), sgb_p.reshape(bsz, seq, sw), a_re, a_im, coef_re,
                                     coef_im, bt_re, bt_im, ct_re, ct_im, dsk, wglu_bf, bgl, tl=MOBA_BLOCK,
                                     gch=gch, nstate=n_state)
    y_p = _proj_out(xp2, attn_p.reshape(bsz * seq, aw), sga_p, ssm_p.reshape(bsz * seq, sw), sma_p, smb_p,
                    wa_bf, wb_bf, wo_bf, tm=MOBA_BLOCK)

    xs2 = x_sample.reshape(bd, dm)
    (q_s, k_s, v_s, _, _, sga_s, u_s, sgb_s, sma_s, smb_s) = _proj_in(
        xs2, g_row, w_in_bf, qg_t, kg_t, bd_ones, tm=bd, aw=aw, sw=sw, head_dim=head_dim,
        with_ksum=False, q_dtype=F32, act_dtype=F32)
    cache_k2 = cache_k.reshape(n_pool, page, aw)
    cache_v2 = cache_v.reshape(n_pool, page, aw)
    pages_per_step = 16 if n_pool % 16 == 0 else 8
    sums = _page_sums(cache_k2, pages_per_step)
    seg = (head_of[:, None] == jnp.arange(LANES)[None, :]).astype(F32)
    sel8 = _sample_select(page_table, q_s, sums, seg, ppb=ppb, n_heads=n_heads)
    sel = sel8[:, :MOBA_TOPK, :n_heads].transpose(0, 2, 1).reshape(bd, n_heads * MOBA_TOPK)
    attn_s = _sample_attention(page_table, sel, rel_bias, q_s, k_s, v_s, cache_k2, cache_v2, n_heads=n_heads,
                               head_dim=head_dim, ppb=ppb, past_len=past_len)
    ssm_s, nre_s, nim_s = _s5_sample(u_s, sgb_s, state_re.reshape(bd, ns), state_im.reshape(bd, ns), a_re, a_im,
                                     coef_re, coef_im, bt_re, bt_im, ct_re, ct_im, dsk, w_glu, bgl, gch=gch,
                                     nstate=n_state)
    y_s = _proj_out(xs2, attn_s, sga_s, ssm_s, sma_s, smb_s, wa_bf, wb_bf, wo_bf, tm=bd)

    return (
        y_p.reshape(bsz, seq, dm), y_s.reshape(bd, 1, dm),
        k_p.reshape(bsz, seq, n_heads, head_dim), v_p.reshape(bsz, seq, n_heads, head_dim),
        fre_p.reshape(bsz, n_groups, n_state), fim_p.reshape(bsz, n_groups, n_state),
        k_s.reshape(bd, 1, n_heads, head_dim), v_s.reshape(bd, 1, n_heads, head_dim),
        nre_s.reshape(bd, n_groups, n_state), nim_s.reshape(bd, n_groups, n_state),
    )


def kernel(x_prompt, x_sample, cache_k, cache_v, state_ssm_re, state_ssm_im, page_table, rel_bias, norm_g, w_in,
           q_norm_g, k_norm_g, lam_re, lam_im, log_dt, b_re, b_im, c_re, c_im, d_skip, w_glu, b_glu, w_branch_a,
           w_branch_b, w_out):
    depth = norm_g.shape[0]
    assert depth == 1, "one layer per step"
    outs = _layer(x_prompt, x_sample, cache_k[0], cache_v[0], state_ssm_re[0], state_ssm_im[0], page_table,
                  rel_bias, norm_g[0], w_in[0], q_norm_g[0], k_norm_g[0], lam_re[0], lam_im[0], log_dt[0],
                  b_re[0], b_im[0], c_re[0], c_im[0], d_skip[0], w_glu[0], b_glu[0], w_branch_a[0],
                  w_branch_b[0], w_out[0])
    y_p, y_s, k_p, v_p, fre_p, fim_p, k_s, v_s, nre_s, nim_s = outs
    add = lambda a: a[None]
    return (y_p, y_s, add(k_p), add(v_p), add(fre_p), add(fim_p), add(k_s), add(v_s), add(nre_s), add(nim_s))
```

```python
import functools
import math

import jax
import jax.numpy as jnp
from jax import lax
from jax.experimental import pallas as pl
from jax.experimental.pallas import tpu as pltpu

F32 = jnp.float32
BF16 = jnp.bfloat16
HIGHEST = lax.Precision.HIGHEST

MOBA_BLOCK = 256
MOBA_TOPK = 3
REL_BUCKETS = 32
REL_MAX_DIST = 4096
RMS_EPS = 1e-6
LOG2E = math.log2(math.e)
LANES = 128
BF16_SUBLANES = 16
VMEM_LIMIT = 56 * 1024 * 1024

NEG_INF = float("-inf")


def _cparams(n_axes, vmem=VMEM_LIMIT):
    return pltpu.CompilerParams(dimension_semantics=("arbitrary",) * n_axes, vmem_limit_bytes=vmem)


def _const_spec(shape, n_grid):
    zeros = (0,) * len(shape)
    if n_grid == 1:
        imap = lambda i: zeros
    elif n_grid == 2:
        imap = lambda i, j: zeros
    else:
        imap = lambda i, j, k: zeros
    return pl.BlockSpec(shape, imap, pipeline_mode=pl.Buffered(1))


def _proj_in_kernel(x_ref, g_ref, w_ref, qg_ref, kg_ref, bd_ref, *out_refs, aw, sw, dm, head_dim, with_ksum):
    if with_ksum:
        (q_ref, k_ref, v_ref, kb_ref, vb_ref, ksum_ref, sga_ref, u_ref, sgb_ref, sma_ref, smb_ref) = out_refs
    else:
        (q_ref, k_ref, v_ref, kb_ref, vb_ref, sga_ref, u_ref, sgb_ref, sma_ref, smb_ref) = out_refs
        ksum_ref = None
    x = x_ref[...]
    ms = jnp.mean(x * x, axis=-1, keepdims=True)
    h = (x * lax.rsqrt(ms + RMS_EPS) * g_ref[...]).astype(BF16)

    def seg(a, b):
        return jnp.dot(h, w_ref[:, a:b], preferred_element_type=F32)

    def headnorm(z, gain):
        ss = jnp.dot((z * z).astype(BF16), bd_ref[...], preferred_element_type=F32)
        return z * lax.rsqrt(ss * (1.0 / head_dim) + RMS_EPS) * gain

    o = 0
    q = headnorm(seg(o, o + aw), qg_ref[...])
    q_ref[...] = q.astype(q_ref.dtype)
    o += aw
    k = headnorm(seg(o, o + aw), kg_ref[...])
    k_ref[...] = k
    kb_ref[...] = k.astype(BF16)
    if ksum_ref is not None:
        ksum_ref[0] = jnp.sum(k, axis=0, keepdims=True)
    o += aw
    v = seg(o, o + aw)
    v_ref[...] = v
    vb_ref[...] = v.astype(BF16)
    o += aw
    sga_ref[...] = jax.nn.silu(seg(o, o + aw)).astype(sga_ref.dtype)
    o += aw
    u_ref[...] = seg(o, o + sw).astype(u_ref.dtype)
    o += sw
    sgb_ref[...] = jax.nn.silu(seg(o, o + sw)).astype(sgb_ref.dtype)
    o += sw
    sma_ref[...] = jax.nn.sigmoid(seg(o, o + dm)).astype(sma_ref.dtype)
    o += dm
    smb_ref[...] = jax.nn.sigmoid(seg(o, o + dm)).astype(smb_ref.dtype)


def _proj_in(x2, g, w_bf, qg_t, kg_t, bd, *, tm, aw, sw, head_dim, with_ksum, q_dtype, act_dtype):
    t, dm = x2.shape
    n_tiles = t // tm
    row = lambda w: pl.BlockSpec((tm, w), lambda i: (i, 0))
    out_shape = [
        jax.ShapeDtypeStruct((t, aw), q_dtype),
        jax.ShapeDtypeStruct((t, aw), F32),
        jax.ShapeDtypeStruct((t, aw), F32),
        jax.ShapeDtypeStruct((t, aw), BF16),
        jax.ShapeDtypeStruct((t, aw), BF16),
    ]
    out_specs = [row(aw)] * 5
    if with_ksum:
        out_shape.append(jax.ShapeDtypeStruct((n_tiles, 1, aw), F32))
        out_specs.append(pl.BlockSpec((1, 1, aw), lambda i: (i, 0, 0)))
    out_shape += [
        jax.ShapeDtypeStruct((t, aw), act_dtype),
        jax.ShapeDtypeStruct((t, sw), act_dtype),
        jax.ShapeDtypeStruct((t, sw), act_dtype),
        jax.ShapeDtypeStruct((t, dm), act_dtype),
        jax.ShapeDtypeStruct((t, dm), act_dtype),
    ]
    out_specs += [row(aw), row(sw), row(sw), row(dm), row(dm)]
    kern = functools.partial(_proj_in_kernel, aw=aw, sw=sw, dm=dm, head_dim=head_dim, with_ksum=with_ksum)
    return pl.pallas_call(
        kern,
        grid=(n_tiles,),
        in_specs=[
            row(dm),
            _const_spec((1, dm), 1),
            _const_spec(w_bf.shape, 1),
            _const_spec((1, aw), 1),
            _const_spec((1, aw), 1),
            _const_spec((aw, aw), 1),
        ],
        out_specs=out_specs,
        out_shape=out_shape,
        compiler_params=_cparams(1),
        name="proj_in",
    )(x2, g, w_bf, qg_t, kg_t, bd)


def _rel_bucket(dist):
    n = jnp.maximum(dist, 0)
    max_exact = REL_BUCKETS // 2
    nf = jnp.maximum(n, 1).astype(F32)
    large = max_exact + (
        jnp.log(nf / max_exact) * ((REL_BUCKETS - max_exact) / math.log(REL_MAX_DIST / max_exact))
    ).astype(jnp.int32)
    return jnp.where(n < max_exact, n, jnp.minimum(large, REL_BUCKETS - 1))


def _bias_lookup(bucket, relb_ref, head):
    val = jnp.zeros(bucket.shape, F32)
    for b in range(REL_BUCKETS):
        val = jnp.where(bucket == b, relb_ref[b, head], val)
    return val


def _bias_tiles_kernel(relb_ref, o_ref):
    h = pl.program_id(0)
    d = pl.program_id(1)
    key = lax.broadcasted_iota(jnp.int32, (MOBA_BLOCK, MOBA_BLOCK), 0)
    qry = lax.broadcasted_iota(jnp.int32, (MOBA_BLOCK, MOBA_BLOCK), 1)
    dist = d * MOBA_BLOCK + qry - key
    o_ref[0, 0] = _bias_lookup(_rel_bucket(dist), relb_ref, h) * LOG2E


def _bias_tiles(rel_bias, n_heads, n_dist):
    return pl.pallas_call(
        _bias_tiles_kernel,
        grid=(n_heads, n_dist),
        in_specs=[pl.BlockSpec(memory_space=pltpu.SMEM)],
        out_specs=pl.BlockSpec((1, 1, MOBA_BLOCK, MOBA_BLOCK), lambda h, d: (h, d, 0, 0)),
        out_shape=jax.ShapeDtypeStruct((n_heads, n_dist, MOBA_BLOCK, MOBA_BLOCK), F32),
        compiler_params=_cparams(2),
        name="bias_tiles",
    )(rel_bias)


def _select_topk(s):
    rowi = lax.broadcasted_iota(jnp.int32, s.shape, 0)
    sel = jnp.zeros(s.shape, jnp.bool_)
    for _ in range(MOBA_TOPK):
        mx = jnp.max(s, axis=0, keepdims=True)
        cand = jnp.where(s == mx, rowi, s.shape[0])
        idx = jnp.min(cand, axis=0, keepdims=True)
        hit = (rowi == idx) & (mx > NEG_INF)
        sel = sel | hit
        s = jnp.where(hit, NEG_INF, s)
    return sel


def _attn_kernel(q_ref, kb_ref, vb_ref, ksum_ref, bias_ref, o_ref,
                 vt0, vt1, qt0, qt1, sel0, sel1, m0, m1, acc0, acc1, sa0, sa1, sb0, sb1, ca0, ca1, cb0, cb1,
                 *, head_dim, n_dist, scale):
    t = pl.program_id(2)
    blk = MOBA_BLOCK
    n_blocks = vt0.shape[0]
    pair = 2 * head_dim
    vts, qts, sels, ms, accs = (vt0, vt1), (qt0, qt1), (sel0, sel1), (m0, m1), (acc0, acc1)
    s_bufs, c_bufs = ((sa0, sa1), (sb0, sb1)), ((ca0, ca1), (cb0, cb1))

    @pl.when(t == 0)
    def _():
        ones = jnp.ones((vt0.shape[1] - head_dim, blk), BF16)

        def xpose(c, carry):
            vt = vb_ref[0, pl.ds(pl.multiple_of(c * blk, blk), blk), :].astype(F32).T.astype(BF16)
            for h in range(2):
                vts[h][c, 0:head_dim, :] = vt[h * head_dim:(h + 1) * head_dim, :]
                vts[h][c, head_dim:, :] = ones
            return carry

        lax.fori_loop(0, n_blocks, xpose, 0)

    qt = q_ref[0].astype(F32).T
    feat = lax.broadcasted_iota(jnp.int32, (pair, blk), 0)
    means = ksum_ref[0] * (1.0 / blk)
    blk_i = lax.broadcasted_iota(jnp.int32, (n_blocks, blk), 0)
    key_i = lax.broadcasted_iota(jnp.int32, (blk, blk), 0)
    qry_i = lax.broadcasted_iota(jnp.int32, (blk, blk), 1)
    k_own = kb_ref[0, pl.ds(pl.multiple_of(t * blk, blk), blk), :]
    for h in range(2):
        in_head = (feat >= h * head_dim) & (feat < (h + 1) * head_dim)
        qt_h = jnp.where(in_head, qt, 0.0)
        qts[h][...] = (qt_h * (scale * LOG2E)).astype(BF16)
        sc = jnp.dot(means, qt_h, preferred_element_type=F32, precision=HIGHEST)
        sc = jnp.where(blk_i < t, sc, NEG_INF)
        sels[h][...] = jnp.where(_select_topk(sc), 0.0, NEG_INF)
        s = jnp.dot(k_own, qts[h][...], preferred_element_type=F32) + bias_ref[h, 0]
        s = jnp.where(key_i <= qry_i, s, NEG_INF)
        m = jnp.max(s, axis=0, keepdims=True)
        ms[h][...] = m
        accs[h][...] = jnp.dot(vts[h][t], jnp.exp2(s - m).astype(BF16), preferred_element_type=F32)

    def stage_a(j, which):
        live = j < t
        jj = jnp.where(live, j, 0)
        kj = kb_ref[0, pl.ds(pl.multiple_of(jj * blk, blk), blk), :]
        d = jnp.minimum(t - jj, n_dist - 1)
        for h in range(2):
            row = jnp.where(live, sels[h][pl.ds(jj, 1), :], NEG_INF)
            s = jnp.dot(kj, qts[h][...], preferred_element_type=F32) + bias_ref[h, d] + row
            s_bufs[which][h][...] = s
            c_bufs[which][h][...] = jnp.max(s, axis=0, keepdims=True)

    def stage_b(j, which):
        jj = jnp.minimum(j, n_blocks - 1)
        for h in range(2):
            m_old = ms[h][...]
            m_new = jnp.maximum(m_old, c_bufs[which][h][...])
            p = jnp.exp2(s_bufs[which][h][...] - m_new).astype(BF16)
            accs[h][...] = accs[h][...] * jnp.exp2(m_old - m_new) + jnp.dot(vts[h][jj], p,
                                                                        preferred_element_type=F32)
            ms[h][...] = m_new

    stage_a(jnp.int32(0), 0)

    def two_blocks(i, carry):
        j = 2 * i
        stage_a(j + 1, 1)
        stage_b(j, 0)
        stage_a(j + 2, 0)
        stage_b(j + 1, 1)
        return carry

    lax.fori_loop(0, (t + 1) // 2, two_blocks, 0)

    outs = [accs[h][0:head_dim, :] / accs[h][head_dim:head_dim + 1, :] for h in range(2)]
    o_ref[0] = jnp.concatenate(outs, axis=0).T.astype(o_ref.dtype)


def _attention(q_bf, k_bf, v_bf, ksum, bias, *, head_dim):
    b, l, aw = q_bf.shape
    pair = 2 * head_dim
    assert pair == LANES and aw % pair == 0 and l % MOBA_BLOCK == 0
    n_pairs = aw // pair
    n_blocks = l // MOBA_BLOCK
    n_dist = bias.shape[1]
    v_rows = head_dim + BF16_SUBLANES
    kern = functools.partial(_attn_kernel, head_dim=head_dim, n_dist=n_dist, scale=head_dim ** -0.5)
    one = pl.Buffered(1)
    return pl.pallas_call(
        kern,
        grid=(b, n_pairs, n_blocks),
        in_specs=[
            pl.BlockSpec((1, MOBA_BLOCK, pair), lambda bi, hp, t: (bi, t, hp)),
            pl.BlockSpec((1, l, pair), lambda bi, hp, t: (bi, 0, hp), pipeline_mode=one),
            pl.BlockSpec((1, l, pair), lambda bi, hp, t: (bi, 0, hp), pipeline_mode=one),
            pl.BlockSpec((1, n_blocks, pair), lambda bi, hp, t: (bi, 0, hp), pipeline_mode=one),
            pl.BlockSpec((2, n_dist, MOBA_BLOCK, MOBA_BLOCK), lambda bi, hp, t: (hp, 0, 0, 0), pipeline_mode=one),
        ],
        out_specs=pl.BlockSpec((1, MOBA_BLOCK, pair), lambda bi, hp, t: (bi, t, hp)),
        out_shape=jax.ShapeDtypeStruct((b, l, aw), BF16),
        scratch_shapes=(
            [pltpu.VMEM((n_blocks, v_rows, MOBA_BLOCK), BF16)] * 2
            + [pltpu.VMEM((pair, MOBA_BLOCK), BF16)] * 2
            + [pltpu.VMEM((n_blocks, MOBA_BLOCK), F32)] * 2
            + [pltpu.VMEM((1, MOBA_BLOCK), F32)] * 2
            + [pltpu.VMEM((v_rows, MOBA_BLOCK), F32)] * 2
            + [pltpu.VMEM((MOBA_BLOCK, MOBA_BLOCK), F32)] * 4
            + [pltpu.VMEM((1, MOBA_BLOCK), F32)] * 4
        ),
        compiler_params=_cparams(3),
        name="moba_attn",
    )(q_bf, k_bf, v_bf, ksum, bias)


def _ssm_prep_kernel(lre_ref, lim_ref, ldt_ref, are_ref, aim_ref, cre_ref, cim_ref):
    lre = lre_ref[...]
    lim = lim_ref[...]
    dt = jnp.exp(ldt_ref[...])
    mag = jnp.exp(lre * dt)
    a_re = mag * jnp.cos(lim * dt)
    a_im = mag * jnp.sin(lim * dt)
    are_ref[...] = a_re
    aim_ref[...] = a_im
    nr = a_re - 1.0
    den = lre * lre + lim * lim
    cre_ref[...] = (nr * lre + a_im * lim) / den
    cim_ref[...] = (a_im * lre - nr * lim) / den


def _ssm_prep(lam_re, lam_im, log_dt):
    g, p = lam_re.shape
    shp = jax.ShapeDtypeStruct((g, p), F32)
    return pl.pallas_call(_ssm_prep_kernel, out_shape=[shp] * 4, name="ssm_prep")(
        lam_re, lam_im, log_dt.reshape(g, 1))


def _block_diag_mask(rows, cols, row_group, col_group):
    r = lax.broadcasted_iota(jnp.int32, (rows, cols), 0) // row_group
    c = lax.broadcasted_iota(jnp.int32, (rows, cols), 1) // col_group
    return r == c


def _build_ssm_weights(coef_re_ref, coef_im_ref, bt_re_ref, bt_im_ref, ct_re_ref, ct_im_ref,
                       wb_re_ref, wb_im_ref, wc_re_ref, wc_im_ref, *, gch, nstate):
    halves, kh, nh = wb_re_ref.shape
    mask_b = _block_diag_mask(kh, nh, gch, nstate)
    mask_c = _block_diag_mask(nh, kh, nstate, gch)
    for i in range(halves):
        cr = coef_re_ref[i]
        ci = coef_im_ref[i]
        br = bt_re_ref[i * kh:(i + 1) * kh, :]
        bi = bt_im_ref[i * kh:(i + 1) * kh, :]
        wb_re_ref[i] = jnp.where(mask_b, cr * br - ci * bi, 0.0).astype(wb_re_ref.dtype)
        wb_im_ref[i] = jnp.where(mask_b, cr * bi + ci * br, 0.0).astype(wb_im_ref.dtype)
        wc_re_ref[i] = jnp.where(mask_c, ct_re_ref[i * nh:(i + 1) * nh, :], 0.0).astype(wc_re_ref.dtype)
        wc_im_ref[i] = jnp.where(mask_c, -ct_im_ref[i * nh:(i + 1) * nh, :], 0.0).astype(wc_im_ref.dtype)


def _ssm_in(u, wb_re_ref, wb_im_ref, precision=None):
    halves, kh, _ = wb_re_ref.shape
    re, im = [], []
    for i in range(halves):
        ui = u[:, i * kh:(i + 1) * kh]
        re.append(jnp.dot(ui, wb_re_ref[i], preferred_element_type=F32, precision=precision))
        im.append(jnp.dot(ui, wb_im_ref[i], preferred_element_type=F32, precision=precision))
    return jnp.concatenate(re, axis=1), jnp.concatenate(im, axis=1)


def _ssm_out(x_re, x_im, wc_re_ref, wc_im_ref, precision=None):
    halves, nh, _ = wc_re_ref.shape
    ys = []
    for i in range(halves):
        ys.append(jnp.dot(x_re[:, i * nh:(i + 1) * nh], wc_re_ref[i], preferred_element_type=F32, precision=precision)
                  + jnp.dot(x_im[:, i * nh:(i + 1) * nh], wc_im_ref[i], preferred_element_type=F32,
                            precision=precision))
    return jnp.concatenate(ys, axis=1)


def _glu_tail(y, u32, dskip_ref, wglu_ref, bglu_ref, sgb, sw, precision=None):
    y = y + dskip_ref[...] * u32
    za = jax.nn.gelu(y).astype(wglu_ref.dtype)
    glu = jnp.dot(za, wglu_ref[...], preferred_element_type=F32, precision=precision) + bglu_ref[...]
    return glu[:, :sw] * jax.nn.sigmoid(glu[:, sw:]) * sgb


def _s5_prompt_kernel(u_ref, sgb_ref, are_ref, aim_ref, coef_re_ref, coef_im_ref, bt_re_ref, bt_im_ref,
                      ct_re_ref, ct_im_ref, dskip_ref, wglu_ref, bglu_ref,
                      out_ref, fre_ref, fim_ref,
                      wb_re_ref, wb_im_ref, wc_re_ref, wc_im_ref, bu_re_ref, bu_im_ref, xs_re_ref, xs_im_ref,
                      st_re_ref, st_im_ref, *, gch, nstate, sw):
    b = pl.program_id(0)
    t = pl.program_id(1)
    tl = u_ref.shape[1]

    @pl.when((b == 0) & (t == 0))
    def _():
        _build_ssm_weights(coef_re_ref, coef_im_ref, bt_re_ref, bt_im_ref, ct_re_ref, ct_im_ref,
                           wb_re_ref, wb_im_ref, wc_re_ref, wc_im_ref, gch=gch, nstate=nstate)

    @pl.when(t == 0)
    def _():
        st_re_ref[...] = jnp.zeros_like(st_re_ref)
        st_im_ref[...] = jnp.zeros_like(st_im_ref)

    u = u_ref[0]
    bu_re, bu_im = _ssm_in(u, wb_re_ref, wb_im_ref)
    bu_re_ref[...] = bu_re
    bu_im_ref[...] = bu_im
    a_re = are_ref[...]
    a_im = aim_ref[...]

    def step(i, carry):
        xr, xi = carry
        nr = a_re * xr - a_im * xi + bu_re_ref[pl.ds(i, 1), :]
        ni = a_re * xi + a_im * xr + bu_im_ref[pl.ds(i, 1), :]
        xs_re_ref[pl.ds(i, 1), :] = nr
        xs_im_ref[pl.ds(i, 1), :] = ni
        return nr, ni

    xr, xi = lax.fori_loop(0, tl, step, (st_re_ref[...], st_im_ref[...]), unroll=8)
    st_re_ref[...] = xr
    st_im_ref[...] = xi
    fre_ref[0] = xr
    fim_ref[0] = xi

    y = _ssm_out(xs_re_ref[...].astype(BF16), xs_im_ref[...].astype(BF16), wc_re_ref, wc_im_ref)
    out = _glu_tail(y, u.astype(F32), dskip_ref, wglu_ref, bglu_ref, sgb_ref[0].astype(F32), sw)
    out_ref[0] = out.astype(out_ref.dtype)


def _s5_prompt(u, sgb, a_re, a_im, coef_re, coef_im, bt_re, bt_im, ct_re, ct_im, d_skip, w_glu, b_glu, *, tl,
               gch, nstate):
    b, l, sw = u.shape
    ns = a_re.shape[1]
    halves = 2
    kh = sw // halves
    nh = ns // halves
    kern = functools.partial(_s5_prompt_kernel, gch=gch, nstate=nstate, sw=sw)
    tile = pl.BlockSpec((1, tl, sw), lambda bi, t: (bi, t, 0))
    c2 = lambda shape: _const_spec(shape, 2)
    return pl.pallas_call(
        kern,
        grid=(b, l // tl),
        in_specs=[tile, tile, c2((1, ns)), c2((1, ns)), c2((halves, 1, nh)), c2((halves, 1, nh)),
                  c2(bt_re.shape), c2(bt_im.shape), c2(ct_re.shape), c2(ct_im.shape),
                  c2((1, sw)), c2(w_glu.shape), c2((1, 2 * sw))],
        out_specs=[tile,
                   pl.BlockSpec((1, 1, ns), lambda bi, t: (bi, 0, 0)),
                   pl.BlockSpec((1, 1, ns), lambda bi, t: (bi, 0, 0))],
        out_shape=[jax.ShapeDtypeStruct((b, l, sw), BF16),
                   jax.ShapeDtypeStruct((b, 1, ns), F32),
                   jax.ShapeDtypeStruct((b, 1, ns), F32)],
        scratch_shapes=[
            pltpu.VMEM((halves, kh, nh), BF16), pltpu.VMEM((halves, kh, nh), BF16),
            pltpu.VMEM((halves, nh, kh), BF16), pltpu.VMEM((halves, nh, kh), BF16),
            pltpu.VMEM((tl, ns), F32), pltpu.VMEM((tl, ns), F32),
            pltpu.VMEM((tl, ns), F32), pltpu.VMEM((tl, ns), F32),
            pltpu.VMEM((1, ns), F32), pltpu.VMEM((1, ns), F32),
        ],
        compiler_params=_cparams(2),
        name="s5_prompt",
    )(u, sgb, a_re, a_im, coef_re, coef_im, bt_re, bt_im, ct_re, ct_im, d_skip, w_glu, b_glu)


def _s5_sample_kernel(u_ref, sgb_ref, sre_ref, sim_ref, are_ref, aim_ref, coef_re_ref, coef_im_ref,
                      bt_re_ref, bt_im_ref, ct_re_ref, ct_im_ref, dskip_ref, wglu_ref, bglu_ref,
                      out_ref, nre_ref, nim_ref, wb_re_ref, wb_im_ref, wc_re_ref, wc_im_ref, *, gch, nstate, sw):
    _build_ssm_weights(coef_re_ref, coef_im_ref, bt_re_ref, bt_im_ref, ct_re_ref, ct_im_ref,
                       wb_re_ref, wb_im_ref, wc_re_ref, wc_im_ref, gch=gch, nstate=nstate)
    u = u_ref[...]
    bu_re, bu_im = _ssm_in(u, wb_re_ref, wb_im_ref, precision=HIGHEST)
    a_re = are_ref[...]
    a_im = aim_ref[...]
    s_re = sre_ref[...]
    s_im = sim_ref[...]
    x_re = a_re * s_re - a_im * s_im + bu_re
    x_im = a_re * s_im + a_im * s_re + bu_im
    nre_ref[...] = x_re
    nim_ref[...] = x_im
    y = _ssm_out(x_re, x_im, wc_re_ref, wc_im_ref, precision=HIGHEST)
    out_ref[...] = _glu_tail(y, u, dskip_ref, wglu_ref, bglu_ref, sgb_ref[...], sw, precision=HIGHEST)


def _s5_sample(u, sgb, s_re, s_im, a_re, a_im, coef_re, coef_im, bt_re, bt_im, ct_re, ct_im, d_skip, w_glu,
               b_glu, *, gch, nstate):
    n, sw = u.shape
    ns = a_re.shape[1]
    halves = 2
    kh = sw // halves
    nh = ns // halves
    kern = functools.partial(_s5_sample_kernel, gch=gch, nstate=nstate, sw=sw)
    return pl.pallas_call(
        kern,
        out_shape=[jax.ShapeDtypeStruct((n, sw), F32),
                   jax.ShapeDtypeStruct((n, ns), F32),
                   jax.ShapeDtypeStruct((n, ns), F32)],
        scratch_shapes=[
            pltpu.VMEM((halves, kh, nh), F32), pltpu.VMEM((halves, kh, nh), F32),
            pltpu.VMEM((halves, nh, kh), F32), pltpu.VMEM((halves, nh, kh), F32),
        ],
        compiler_params=pltpu.CompilerParams(vmem_limit_bytes=VMEM_LIMIT),
        name="s5_sample",
    )(u, sgb, s_re, s_im, a_re, a_im, coef_re, coef_im, bt_re, bt_im, ct_re, ct_im, d_skip, w_glu, b_glu)


def _proj_out_kernel(x_ref, attn_ref, sga_ref, ssm_ref, sma_ref, smb_ref, wa_ref, wb_ref, wo_ref, o_ref):
    attn_out = (attn_ref[...].astype(F32) * sga_ref[...].astype(F32)).astype(BF16)
    ya = jnp.dot(attn_out, wa_ref[...], preferred_element_type=F32)
    yb = jnp.dot(ssm_ref[...].astype(BF16), wb_ref[...], preferred_element_type=F32)
    merged = sma_ref[...].astype(F32) * ya + smb_ref[...].astype(F32) * yb
    o_ref[...] = x_ref[...] + jnp.dot(merged.astype(BF16), wo_ref[...], preferred_element_type=F32)


def _proj_out(x2, attn, sga, ssm, sma, smb, wa, wb, wo, *, tm):
    t, dm = x2.shape
    aw = attn.shape[1]
    sw = ssm.shape[1]
    row = lambda w: pl.BlockSpec((tm, w), lambda i: (i, 0))
    return pl.pallas_call(
        _proj_out_kernel,
        grid=(t // tm,),
        in_specs=[row(dm), row(aw), row(aw), row(sw), row(dm), row(dm),
                  _const_spec(wa.shape, 1), _const_spec(wb.shape, 1), _const_spec(wo.shape, 1)],
        out_specs=row(dm),
        out_shape=jax.ShapeDtypeStruct((t, dm), F32),
        compiler_params=_cparams(1),
        name="proj_out",
    )(x2, attn, sga, ssm, sma, smb, wa, wb, wo)


def _col_from_row(row, n):
    eye = lax.broadcasted_iota(jnp.int32, (n, n), 0) == lax.broadcasted_iota(jnp.int32, (n, n), 1)
    return jnp.sum(jnp.where(eye, jnp.broadcast_to(row, (n, n)), 0.0), axis=1, keepdims=True)


def _row_from_col(col, n):
    eye = lax.broadcasted_iota(jnp.int32, (n, n), 0) == lax.broadcasted_iota(jnp.int32, (n, n), 1)
    return jnp.sum(jnp.where(eye, jnp.broadcast_to(col, (n, n)), 0.0), axis=0, keepdims=True)


def _sample_attn_kernel(pt_ref, relb_ref, q_ref, kn_ref, vn_ref, ck_ref, cv_ref, o_ref,
                        kbuf, ksem, qb_ref, lg_ref, selv_ref, sels_ref, selsem, vbuf, vsem,
                        *, n_heads, head_dim, ppb, page, n_pages, n_ring, scale):
    b = pl.program_id(0)
    total = pl.num_programs(0) * n_pages
    n_blocks = n_pages // ppb
    past_len = n_pages * page
    n_sel = MOBA_TOPK

    def k_copy(g):
        phys = pt_ref[lax.div(g, n_pages), lax.rem(g, n_pages)]
        slot = lax.rem(g, n_ring)
        return pltpu.make_async_copy(ck_ref.at[0, phys], kbuf.at[slot], ksem.at[slot])

    @pl.when(b == 0)
    def _():
        for g in range(n_ring - 1):
            k_copy(jnp.int32(g)).start()

    q = q_ref[0]
    for h in range(n_heads):
        col = _col_from_row(q[:, h * head_dim:(h + 1) * head_dim], head_dim)
        qb_ref[h] = jnp.broadcast_to(col, (head_dim, page))

    def stream(pg, carry):
        g = b * n_pages + pg
        nxt = g + (n_ring - 1)

        @pl.when(nxt < total)
        def _():
            k_copy(nxt).start()

        k_copy(g).wait()
        kt = kbuf[lax.rem(g, n_ring)]
        lg_ref[pg] = jnp.sum(kt * qb_ref[...], axis=1)
        return carry

    lax.fori_loop(0, n_pages, stream, 0)

    lane = lax.broadcasted_iota(jnp.int32, (n_heads, LANES), 1)
    sc = jnp.full((n_heads, LANES), NEG_INF, F32)
    for jb in range(n_blocks):
        tot = lg_ref[jb * ppb]
        for i in range(1, ppb):
            tot = tot + lg_ref[jb * ppb + i]
        sc = jnp.where(lane == jb, jnp.sum(tot, axis=1, keepdims=True) * (1.0 / MOBA_BLOCK), sc)
    sel = jnp.full((n_heads, LANES), -1, jnp.int32)
    for r in range(n_sel):
        mx = jnp.max(sc, axis=1, keepdims=True)
        idx = jnp.min(jnp.where(sc == mx, lane, LANES), axis=1, keepdims=True)
        ok = mx > NEG_INF
        sel = jnp.where(lane == r, jnp.where(ok, idx, -1), sel)
        sc = jnp.where((lane == idx) & ok, NEG_INF, sc)
    selv_ref[...] = sel
    to_smem = pltpu.make_async_copy(selv_ref, sels_ref, selsem)
    to_smem.start()
    to_smem.wait()

    def chosen(h, r, i):
        j = sels_ref[h, r]
        pgl = jnp.maximum(j, 0) * ppb + i
        return j, pgl, (h * n_sel + r) * ppb + i

    def v_copy(h, r, i):
        _, pgl, slot = chosen(h, r, i)
        return pltpu.make_async_copy(cv_ref.at[0, pt_ref[b, pgl], h], vbuf.at[slot], vsem.at[slot])

    for h in range(n_heads):
        for r in range(n_sel):
            for i in range(ppb):
                @pl.when(chosen(h, r, i)[0] >= 0)
                def _():
                    v_copy(h, r, i).start()

    pos = lax.broadcasted_iota(jnp.int32, (1, page), 1)
    rows = []
    for h in range(n_heads):
        hs = slice(h * head_dim, (h + 1) * head_dim)
        qh = q[:, hs]
        s_self = jnp.sum(qh * kn_ref[0][:, hs], axis=1, keepdims=True) * scale + relb_ref[0, h]
        logits = []
        for r in range(n_sel):
            for i in range(ppb):
                j, pgl, _ = chosen(h, r, i)
                s = lg_ref[pgl][h:h + 1, :] * scale
                s = s + _bias_lookup(_rel_bucket(past_len - pgl * page - pos), relb_ref, h)
                logits.append(jnp.where(j >= 0, s, NEG_INF))
        m = s_self
        for s in logits:
            m = jnp.maximum(m, jnp.max(s, axis=1, keepdims=True))
        p_self = jnp.exp(s_self - m)
        den = p_self
        acc = jnp.zeros((head_dim, page), F32)
        for n, s in enumerate(logits):
            j, _, slot = chosen(h, n // ppb, n % ppb)

            @pl.when(j >= 0)
            def _():
                v_copy(h, n // ppb, n % ppb).wait()

            p = jnp.exp(s - m)
            den = den + jnp.sum(p, axis=1, keepdims=True)
            acc = acc + jnp.where(j >= 0, vbuf[slot], 0.0) * p
        o_row = _row_from_col(jnp.sum(acc, axis=1, keepdims=True), head_dim)
        rows.append((o_row + p_self * vn_ref[0][:, hs]) / den)
    o_ref[0] = jnp.concatenate(rows, axis=1)


def _sample_attention(page_table, rel_bias, q, k_new, v_new, cache_kt, cache_vt, *, ppb, n_ring=4):
    bd, w = q.shape
    _, n_pool, n_heads, head_dim, page = cache_kt.shape
    n_pages = page_table.shape[1]
    assert n_pages % ppb == 0 and n_pages // ppb <= LANES and n_heads * head_dim == w
    n_slots = n_heads * MOBA_TOPK * ppb
    kern = functools.partial(_sample_attn_kernel, n_heads=n_heads, head_dim=head_dim, ppb=ppb, page=page,
                             n_pages=n_pages, n_ring=n_ring, scale=head_dim ** -0.5)
    row = pl.BlockSpec((1, 1, w), lambda b, pt: (b, 0, 0))
    grid_spec = pltpu.PrefetchScalarGridSpec(
        num_scalar_prefetch=1,
        grid=(bd,),
        in_specs=[pl.BlockSpec(memory_space=pltpu.SMEM), row, row, row,
                  pl.BlockSpec(memory_space=pl.ANY), pl.BlockSpec(memory_space=pl.ANY)],
        out_specs=row,
        scratch_shapes=[
            pltpu.VMEM((n_ring, n_heads, head_dim, page), F32),
            pltpu.SemaphoreType.DMA((n_ring,)),
            pltpu.VMEM((n_heads, head_dim, page), F32),
            pltpu.VMEM((n_pages, n_heads, page), F32),
            pltpu.VMEM((n_heads, LANES), jnp.int32),
            pltpu.SMEM((n_heads, LANES), jnp.int32),
            pltpu.SemaphoreType.DMA(()),
            pltpu.VMEM((n_slots, head_dim, page), F32),
            pltpu.SemaphoreType.DMA((n_slots,)),
        ],
    )
    out = pl.pallas_call(
        kern,
        grid_spec=grid_spec,
        out_shape=jax.ShapeDtypeStruct((bd, 1, w), F32),
        compiler_params=_cparams(1),
        name="sample_attn",
    )(page_table, rel_bias, q.reshape(bd, 1, w), k_new.reshape(bd, 1, w), v_new.reshape(bd, 1, w),
      cache_kt, cache_vt)
    return out.reshape(bd, w)


def _layer(x_prompt, x_sample, cache_k, cache_v, state_re, state_im, page_table, rel_bias, norm_g, w_in,
           q_norm_g, k_norm_g, lam_re, lam_im, log_dt, b_re, b_im, c_re, c_im, d_skip, w_glu, b_glu,
           w_branch_a, w_branch_b, w_out):
    bsz, seq, dm = x_prompt.shape
    bd, s_new, _ = x_sample.shape
    _, n_pool, page, n_heads, head_dim = cache_k.shape
    n_groups, n_state, gch = b_re.shape
    aw = n_heads * head_dim
    sw = n_groups * gch
    ns = n_groups * n_state
    n_pages = page_table.shape[1]
    past_len = n_pages * page
    ppb = MOBA_BLOCK // page
    assert s_new == 1 and seq % MOBA_BLOCK == 0 and past_len % MOBA_BLOCK == 0
    assert w_in.shape == (dm, 4 * aw + 2 * sw + 2 * dm)

    w_in_bf = w_in.astype(BF16)
    g_row = norm_g.reshape(1, dm)
    qg_t = jnp.tile(q_norm_g.reshape(1, head_dim), (1, n_heads))
    kg_t = jnp.tile(k_norm_g.reshape(1, head_dim), (1, n_heads))
    head_of = jnp.arange(aw) // head_dim
    bd_ones = (head_of[:, None] == head_of[None, :]).astype(BF16)
    halves = 2
    bt_re = jnp.tile(b_re.transpose(0, 2, 1).reshape(sw, n_state), (1, n_groups // halves))
    bt_im = jnp.tile(b_im.transpose(0, 2, 1).reshape(sw, n_state), (1, n_groups // halves))
    ct_re = jnp.tile(c_re.transpose(0, 2, 1).reshape(ns, gch), (1, n_groups // halves))
    ct_im = jnp.tile(c_im.transpose(0, 2, 1).reshape(ns, gch), (1, n_groups // halves))
    dsk = d_skip.reshape(1, sw)
    bgl = b_glu.reshape(1, 2 * sw)

    a_re, a_im, coef_re, coef_im = _ssm_prep(lam_re, lam_im, log_dt)
    a_re = a_re.reshape(1, ns)
    a_im = a_im.reshape(1, ns)
    coef_re = coef_re.reshape(halves, 1, ns // halves)
    coef_im = coef_im.reshape(halves, 1, ns // halves)

    wa_bf = w_branch_a.astype(BF16)
    wb_bf = w_branch_b.astype(BF16)
    wo_bf = w_out.astype(BF16)
    wglu_bf = w_glu.astype(BF16)

    xp2 = x_prompt.reshape(bsz * seq, dm)
    (q_p, k_p, v_p, kb_p, vb_p, ksum_p, sga_p, u_p, sgb_p, sma_p, smb_p) = _proj_in(
        xp2, g_row, w_in_bf, qg_t, kg_t, bd_ones, tm=MOBA_BLOCK, aw=aw, sw=sw, head_dim=head_dim,
        with_ksum=True, q_dtype=BF16, act_dtype=BF16)
    n_blocks = seq // MOBA_BLOCK
    n_dist = min(n_blocks, -(-(REL_MAX_DIST + MOBA_BLOCK - 1) // MOBA_BLOCK) + 1)
    bias = _bias_tiles(rel_bias, n_heads, n_dist)
    attn_p = _attention(q_p.reshape(bsz, seq, aw), kb_p.reshape(bsz, seq, aw), vb_p.reshape(bsz, seq, aw),
                        ksum_p.reshape(bsz, n_blocks, aw), bias, head_dim=head_dim)
    ssm_p, fre_p, fim_p = _s5_prompt(u_p.reshape(bsz, seq, sw), sgb_p.reshape(bsz, seq, sw), a_re, a_im, coef_re,
                                     coef_im, bt_re, bt_im, ct_re, ct_im, dsk, wglu_bf, bgl, tl=MOBA_BLOCK,
                                     gch=gch, nstate=n_state)
    y_p = _proj_out(xp2, attn_p.reshape(bsz * seq, aw), sga_p, ssm_p.reshape(bsz * seq, sw), sma_p, smb_p,
                    wa_bf, wb_bf, wo_bf, tm=MOBA_BLOCK)

    xs2 = x_sample.reshape(bd, dm)
    (q_s, k_s, v_s, _, _, sga_s, u_s, sgb_s, sma_s, smb_s) = _proj_in(
        xs2, g_row, w_in_bf, qg_t, kg_t, bd_ones, tm=bd, aw=aw, sw=sw, head_dim=head_dim,
        with_ksum=False, q_dtype=F32, act_dtype=F32)
    cache_kt = cache_k.transpose(0, 1, 3, 4, 2)
    cache_vt = cache_v.transpose(0, 1, 3, 4, 2)
    attn_s = _sample_attention(page_table, rel_bias, q_s, k_s, v_s, cache_kt, cache_vt, ppb=ppb)
    ssm_s, nre_s, nim_s = _s5_sample(u_s, sgb_s, state_re.reshape(bd, ns), state_im.reshape(bd, ns), a_re, a_im,
                                     coef_re, coef_im, bt_re, bt_im, ct_re, ct_im, dsk, w_glu, bgl, gch=gch,
                                     nstate=n_state)
    y_s = _proj_out(xs2, attn_s, sga_s, ssm_s, sma_s, smb_s, wa_bf, wb_bf, wo_bf, tm=bd)

    return (
        y_p.reshape(bsz, seq, dm), y_s.reshape(bd, 1, dm),
        k_p.reshape(bsz, seq, n_heads, head_dim), v_p.reshape(bsz, seq, n_heads, head_dim),
        fre_p.reshape(bsz, n_groups, n_state), fim_p.reshape(bsz, n_groups, n_state),
        k_s.reshape(bd, 1, n_heads, head_dim), v_s.reshape(bd, 1, n_heads, head_dim),
        nre_s.reshape(bd, n_groups, n_state), nim_s.reshape(bd, n_groups, n_state),
    )


def kernel(x_prompt, x_sample, cache_k, cache_v, state_ssm_re, state_ssm_im, page_table, rel_bias, norm_g, w_in,
           q_norm_g, k_norm_g, lam_re, lam_im, log_dt, b_re, b_im, c_re, c_im, d_skip, w_glu, b_glu, w_branch_a,
           w_branch_b, w_out):
    depth = norm_g.shape[0]
    assert depth == 1, "one layer per step"
    outs = _layer(x_prompt, x_sample, cache_k, cache_v, state_ssm_re[0], state_ssm_im[0], page_table,
                  rel_bias, norm_g[0], w_in[0], q_norm_g[0], k_norm_g[0], lam_re[0], lam_im[0], log_dt[0],
                  b_re[0], b_im[0], c_re[0], c_im[0], d_skip[0], w_glu[0], b_glu[0], w_branch_a[0],
                  w_branch_b[0], w_out[0])
    y_p, y_s, k_p, v_p, fre_p, fim_p, k_s, v_s, nre_s, nim_s = outs
    add = lambda a: a[None]
    return (y_p, y_s, add(k_p), add(v_p), add(fre_p), add(fim_p), add(k_s), add(v_s), add(nre_s), add(nim_s))
```

```python
import functools
import math

import jax
import jax.numpy as jnp
from jax import lax
from jax.experimental import pallas as pl
from jax.experimental.pallas import tpu as pltpu

F32 = jnp.float32
BF16 = jnp.bfloat16
HIGHEST = lax.Precision.HIGHEST

MOBA_BLOCK = 256
MOBA_TOPK = 3
REL_BUCKETS = 32
REL_MAX_DIST = 4096
RMS_EPS = 1e-6
LOG2E = math.log2(math.e)
LANES = 128
BF16_SUBLANES = 16
VMEM_LIMIT = 56 * 1024 * 1024

NEG_INF = float("-inf")


def _cparams(n_axes, vmem=VMEM_LIMIT):
    return pltpu.CompilerParams(dimension_semantics=("arbitrary",) * n_axes, vmem_limit_bytes=vmem)


def _const_spec(shape, n_grid):
    zeros = (0,) * len(shape)
    if n_grid == 1:
        imap = lambda i: zeros
    elif n_grid == 2:
        imap = lambda i, j: zeros
    else:
        imap = lambda i, j, k: zeros
    return pl.BlockSpec(shape, imap, pipeline_mode=pl.Buffered(1))


def _proj_in_kernel(x_ref, g_ref, w_ref, qg_ref, kg_ref, bd_ref, *out_refs, aw, sw, dm, head_dim, with_ksum):
    if with_ksum:
        (q_ref, k_ref, v_ref, kb_ref, vb_ref, ksum_ref, sga_ref, u_ref, sgb_ref, sma_ref, smb_ref) = out_refs
    else:
        (q_ref, k_ref, v_ref, kb_ref, vb_ref, sga_ref, u_ref, sgb_ref, sma_ref, smb_ref) = out_refs
        ksum_ref = None
    x = x_ref[...]
    ms = jnp.mean(x * x, axis=-1, keepdims=True)
    h = (x * lax.rsqrt(ms + RMS_EPS) * g_ref[...]).astype(BF16)

    def seg(a, b):
        return jnp.dot(h, w_ref[:, a:b], preferred_element_type=F32)

    def headnorm(z, gain):
        ss = jnp.dot((z * z).astype(BF16), bd_ref[...], preferred_element_type=F32)
        return z * lax.rsqrt(ss * (1.0 / head_dim) + RMS_EPS) * gain

    o = 0
    q = headnorm(seg(o, o + aw), qg_ref[...])
    q_ref[...] = q.astype(q_ref.dtype)
    o += aw
    k = headnorm(seg(o, o + aw), kg_ref[...])
    k_ref[...] = k
    kb_ref[...] = k.astype(BF16)
    if ksum_ref is not None:
        ksum_ref[0] = jnp.sum(k, axis=0, keepdims=True)
    o += aw
    v = seg(o, o + aw)
    v_ref[...] = v
    vb_ref[...] = v.astype(BF16)
    o += aw
    sga_ref[...] = jax.nn.silu(seg(o, o + aw)).astype(sga_ref.dtype)
    o += aw
    u_ref[...] = seg(o, o + sw).astype(u_ref.dtype)
    o += sw
    sgb_ref[...] = jax.nn.silu(seg(o, o + sw)).astype(sgb_ref.dtype)
    o += sw
    sma_ref[...] = jax.nn.sigmoid(seg(o, o + dm)).astype(sma_ref.dtype)
    o += dm
    smb_ref[...] = jax.nn.sigmoid(seg(o, o + dm)).astype(smb_ref.dtype)


def _proj_in(x2, g, w_bf, qg_t, kg_t, bd, *, tm, aw, sw, head_dim, with_ksum, q_dtype, act_dtype):
    t, dm = x2.shape
    n_tiles = t // tm
    row = lambda w: pl.BlockSpec((tm, w), lambda i: (i, 0))
    out_shape = [
        jax.ShapeDtypeStruct((t, aw), q_dtype),
        jax.ShapeDtypeStruct((t, aw), F32),
        jax.ShapeDtypeStruct((t, aw), F32),
        jax.ShapeDtypeStruct((t, aw), BF16),
        jax.ShapeDtypeStruct((t, aw), BF16),
    ]
    out_specs = [row(aw)] * 5
    if with_ksum:
        out_shape.append(jax.ShapeDtypeStruct((n_tiles, 1, aw), F32))
        out_specs.append(pl.BlockSpec((1, 1, aw), lambda i: (i, 0, 0)))
    out_shape += [
        jax.ShapeDtypeStruct((t, aw), act_dtype),
        jax.ShapeDtypeStruct((t, sw), act_dtype),
        jax.ShapeDtypeStruct((t, sw), act_dtype),
        jax.ShapeDtypeStruct((t, dm), act_dtype),
        jax.ShapeDtypeStruct((t, dm), act_dtype),
    ]
    out_specs += [row(aw), row(sw), row(sw), row(dm), row(dm)]
    kern = functools.partial(_proj_in_kernel, aw=aw, sw=sw, dm=dm, head_dim=head_dim, with_ksum=with_ksum)
    return pl.pallas_call(
        kern,
        grid=(n_tiles,),
        in_specs=[
            row(dm),
            _const_spec((1, dm), 1),
            _const_spec(w_bf.shape, 1),
            _const_spec((1, aw), 1),
            _const_spec((1, aw), 1),
            _const_spec((aw, aw), 1),
        ],
        out_specs=out_specs,
        out_shape=out_shape,
        compiler_params=_cparams(1),
        name="proj_in",
    )(x2, g, w_bf, qg_t, kg_t, bd)


def _rel_bucket(dist):
    n = jnp.maximum(dist, 0)
    max_exact = REL_BUCKETS // 2
    nf = jnp.maximum(n, 1).astype(F32)
    large = max_exact + (
        jnp.log(nf / max_exact) * ((REL_BUCKETS - max_exact) / math.log(REL_MAX_DIST / max_exact))
    ).astype(jnp.int32)
    return jnp.where(n < max_exact, n, jnp.minimum(large, REL_BUCKETS - 1))


def _bias_lookup(bucket, relb_ref, head):
    val = jnp.zeros(bucket.shape, F32)
    for b in range(REL_BUCKETS):
        val = jnp.where(bucket == b, relb_ref[b, head], val)
    return val


def _bias_tiles_kernel(relb_ref, o_ref):
    h = pl.program_id(0)
    d = pl.program_id(1)
    key = lax.broadcasted_iota(jnp.int32, (MOBA_BLOCK, MOBA_BLOCK), 0)
    qry = lax.broadcasted_iota(jnp.int32, (MOBA_BLOCK, MOBA_BLOCK), 1)
    dist = d * MOBA_BLOCK + qry - key
    o_ref[0, 0] = _bias_lookup(_rel_bucket(dist), relb_ref, h) * LOG2E


def _n_bias_tiles(n_blocks):
    saturated = -(-(REL_MAX_DIST + MOBA_BLOCK - 1) // MOBA_BLOCK) + 1
    return min(n_blocks, saturated)


def _bias_tiles(rel_bias, n_heads, n_dist):
    return pl.pallas_call(
        _bias_tiles_kernel,
        grid=(n_heads, n_dist),
        in_specs=[pl.BlockSpec(memory_space=pltpu.SMEM)],
        out_specs=pl.BlockSpec((1, 1, MOBA_BLOCK, MOBA_BLOCK), lambda h, d: (h, d, 0, 0)),
        out_shape=jax.ShapeDtypeStruct((n_heads, n_dist, MOBA_BLOCK, MOBA_BLOCK), F32),
        compiler_params=_cparams(2),
        name="bias_tiles",
    )(rel_bias)


def _select_topk(s):
    rowi = lax.broadcasted_iota(jnp.int32, s.shape, 0)
    sel = jnp.zeros(s.shape, jnp.bool_)
    for _ in range(MOBA_TOPK):
        mx = jnp.max(s, axis=0, keepdims=True)
        cand = jnp.where(s == mx, rowi, s.shape[0])
        idx = jnp.min(cand, axis=0, keepdims=True)
        hit = (rowi == idx) & (mx > NEG_INF)
        sel = sel | hit
        s = jnp.where(hit, NEG_INF, s)
    return sel


def _attn_kernel(q_ref, kb_ref, vb_ref, ksum_ref, bias_ref, o_ref,
                 vt0, vt1, qt0, qt1, sel0, sel1, m0, m1, acc0, acc1, sbuf0, sbuf1, cbuf0, cbuf1,
                 *, head_dim, n_dist, scale):
    t = pl.program_id(2)
    blk = MOBA_BLOCK
    n_blocks = vt0.shape[0]
    pair = 2 * head_dim
    vts, qts, sels, ms, accs = (vt0, vt1), (qt0, qt1), (sel0, sel1), (m0, m1), (acc0, acc1)
    s_bufs, c_bufs = (sbuf0, sbuf1), (cbuf0, cbuf1)

    @pl.when(t == 0)
    def _():
        ones = jnp.ones((vt0.shape[1] - head_dim, blk), BF16)

        def xpose(c, carry):
            vt = vb_ref[0, pl.ds(pl.multiple_of(c * blk, blk), blk), :].astype(F32).T.astype(BF16)
            for h in range(2):
                vts[h][c, 0:head_dim, :] = vt[h * head_dim:(h + 1) * head_dim, :]
                vts[h][c, head_dim:, :] = ones
            return carry

        lax.fori_loop(0, n_blocks, xpose, 0)

    qt = q_ref[0].astype(F32).T
    feat = lax.broadcasted_iota(jnp.int32, (pair, blk), 0)
    means = ksum_ref[0] * (1.0 / blk)
    blk_i = lax.broadcasted_iota(jnp.int32, (n_blocks, blk), 0)
    key_i = lax.broadcasted_iota(jnp.int32, (blk, blk), 0)
    qry_i = lax.broadcasted_iota(jnp.int32, (blk, blk), 1)
    k_own = kb_ref[0, pl.ds(pl.multiple_of(t * blk, blk), blk), :]
    for h in range(2):
        in_head = (feat >= h * head_dim) & (feat < (h + 1) * head_dim)
        qt_h = jnp.where(in_head, qt, 0.0)
        qts[h][...] = (qt_h * (scale * LOG2E)).astype(BF16)
        sc = jnp.dot(means, qt_h, preferred_element_type=F32, precision=HIGHEST)
        sc = jnp.where(blk_i < t, sc, NEG_INF)
        sels[h][...] = _select_topk(sc).astype(F32)
        s = jnp.dot(k_own, qts[h][...], preferred_element_type=F32) + bias_ref[h, 0]
        s = jnp.where(key_i <= qry_i, s, NEG_INF)
        m = jnp.max(s, axis=0, keepdims=True)
        ms[h][...] = m
        accs[h][...] = jnp.dot(vts[h][t], jnp.exp2(s - m).astype(BF16), preferred_element_type=F32)

    def stage_a(j, slot):
        jj = jnp.where(j < t, j, 0)
        kj = kb_ref[0, pl.ds(pl.multiple_of(jj * blk, blk), blk), :]
        tile = jnp.minimum(t - jj, n_dist - 1)
        for h in range(2):
            s = jnp.dot(kj, qts[h][...], preferred_element_type=F32) + bias_ref[h, tile]
            s_bufs[h][slot] = s
            c_bufs[h][slot] = jnp.max(s, axis=0, keepdims=True)

    def stage_b(j, slot):
        live = j < t
        jj = jnp.where(live, j, 0)
        for h in range(2):
            picked = jnp.where(live, sels[h][pl.ds(jj, 1), :], 0.0)
            m_old = ms[h][...]
            m_blk = jnp.maximum(m_old, c_bufs[h][slot])
            m_new = jnp.where(picked > 0.0, m_blk, m_old)
            p = jnp.exp2(s_bufs[h][slot] - m_blk).astype(BF16)
            upd = jnp.dot(vts[h][jj], p, preferred_element_type=F32)
            accs[h][...] = accs[h][...] * jnp.exp2(m_old - m_new) + upd * picked
            ms[h][...] = m_new

    @pl.when(t > 0)
    def _():
        stage_a(jnp.int32(0), 0)
        stage_a(jnp.int32(1), 1)

    n_iter = (t + 1) // 2

    def two_blocks(i, carry):
        j = 2 * i
        for parity in range(2):
            cur, nxt = 2 * parity, 2 - 2 * parity

            @pl.when((i % 2 == parity) & (i < n_iter - 1))
            def _():
                stage_a(j + 2, nxt)
                stage_a(j + 3, nxt + 1)
                stage_b(j, cur)
                stage_b(j + 1, cur + 1)

            @pl.when((i % 2 == parity) & (i == n_iter - 1))
            def _():
                stage_b(j, cur)
                stage_b(j + 1, cur + 1)

        return carry

    lax.fori_loop(0, n_iter, two_blocks, 0)

    outs = [accs[h][0:head_dim, :] / accs[h][head_dim:head_dim + 1, :] for h in range(2)]
    o_ref[0] = jnp.concatenate(outs, axis=0).T.astype(o_ref.dtype)


def _attention(q_bf, k_bf, v_bf, ksum, bias, *, head_dim):
    b, l, aw = q_bf.shape
    pair = 2 * head_dim
    assert pair == LANES and aw % pair == 0 and l % MOBA_BLOCK == 0
    n_pairs = aw // pair
    n_blocks = l // MOBA_BLOCK
    n_dist = _n_bias_tiles(n_blocks)
    assert bias.shape[1] == n_dist
    v_rows = head_dim + BF16_SUBLANES
    kern = functools.partial(_attn_kernel, head_dim=head_dim, n_dist=n_dist, scale=head_dim ** -0.5)
    one = pl.Buffered(1)
    return pl.pallas_call(
        kern,
        grid=(b, n_pairs, n_blocks),
        in_specs=[
            pl.BlockSpec((1, MOBA_BLOCK, pair), lambda bi, hp, t: (bi, t, hp)),
            pl.BlockSpec((1, l, pair), lambda bi, hp, t: (bi, 0, hp), pipeline_mode=one),
            pl.BlockSpec((1, l, pair), lambda bi, hp, t: (bi, 0, hp), pipeline_mode=one),
            pl.BlockSpec((1, n_blocks, pair), lambda bi, hp, t: (bi, 0, hp), pipeline_mode=one),
            pl.BlockSpec((2, n_dist, MOBA_BLOCK, MOBA_BLOCK), lambda bi, hp, t: (hp, 0, 0, 0), pipeline_mode=one),
        ],
        out_specs=pl.BlockSpec((1, MOBA_BLOCK, pair), lambda bi, hp, t: (bi, t, hp)),
        out_shape=jax.ShapeDtypeStruct((b, l, aw), BF16),
        scratch_shapes=(
            [pltpu.VMEM((n_blocks, v_rows, MOBA_BLOCK), BF16)] * 2
            + [pltpu.VMEM((pair, MOBA_BLOCK), BF16)] * 2
            + [pltpu.VMEM((n_blocks, MOBA_BLOCK), F32)] * 2
            + [pltpu.VMEM((1, MOBA_BLOCK), F32)] * 2
            + [pltpu.VMEM((v_rows, MOBA_BLOCK), F32)] * 2
            + [pltpu.VMEM((4, MOBA_BLOCK, MOBA_BLOCK), F32)] * 2
            + [pltpu.VMEM((4, 1, MOBA_BLOCK), F32)] * 2
        ),
        compiler_params=_cparams(3),
        name="moba_attn",
    )(q_bf, k_bf, v_bf, ksum, bias)


def _ssm_prep_kernel(lre_ref, lim_ref, ldt_ref, are_ref, aim_ref, cre_ref, cim_ref):
    lre = lre_ref[...]
    lim = lim_ref[...]
    dt = jnp.exp(ldt_ref[...])
    mag = jnp.exp(lre * dt)
    a_re = mag * jnp.cos(lim * dt)
    a_im = mag * jnp.sin(lim * dt)
    are_ref[...] = a_re
    aim_ref[...] = a_im
    nr = a_re - 1.0
    den = lre * lre + lim * lim
    cre_ref[...] = (nr * lre + a_im * lim) / den
    cim_ref[...] = (a_im * lre - nr * lim) / den


def _ssm_prep(lam_re, lam_im, log_dt):
    g, p = lam_re.shape
    shp = jax.ShapeDtypeStruct((g, p), F32)
    return pl.pallas_call(_ssm_prep_kernel, out_shape=[shp] * 4, name="ssm_prep")(
        lam_re, lam_im, log_dt.reshape(g, 1))


def _block_diag_mask(rows, cols, row_group, col_group):
    r = lax.broadcasted_iota(jnp.int32, (rows, cols), 0) // row_group
    c = lax.broadcasted_iota(jnp.int32, (rows, cols), 1) // col_group
    return r == c


def _build_ssm_weights(coef_re_ref, coef_im_ref, bt_re_ref, bt_im_ref, ct_re_ref, ct_im_ref,
                       wb_re_ref, wb_im_ref, wc_re_ref, wc_im_ref, *, gch, nstate):
    halves, kh, nh = wb_re_ref.shape
    mask_b = _block_diag_mask(kh, nh, gch, nstate)
    mask_c = _block_diag_mask(nh, kh, nstate, gch)
    for i in range(halves):
        cr = coef_re_ref[i]
        ci = coef_im_ref[i]
        br = bt_re_ref[i * kh:(i + 1) * kh, :]
        bi = bt_im_ref[i * kh:(i + 1) * kh, :]
        wb_re_ref[i] = jnp.where(mask_b, cr * br - ci * bi, 0.0).astype(wb_re_ref.dtype)
        wb_im_ref[i] = jnp.where(mask_b, cr * bi + ci * br, 0.0).astype(wb_im_ref.dtype)
        wc_re_ref[i] = jnp.where(mask_c, ct_re_ref[i * nh:(i + 1) * nh, :], 0.0).astype(wc_re_ref.dtype)
        wc_im_ref[i] = jnp.where(mask_c, -ct_im_ref[i * nh:(i + 1) * nh, :], 0.0).astype(wc_im_ref.dtype)


def _ssm_in(u, wb_re_ref, wb_im_ref, precision=None):
    halves, kh, _ = wb_re_ref.shape
    re, im = [], []
    for i in range(halves):
        ui = u[:, i * kh:(i + 1) * kh]
        re.append(jnp.dot(ui, wb_re_ref[i], preferred_element_type=F32, precision=precision))
        im.append(jnp.dot(ui, wb_im_ref[i], preferred_element_type=F32, precision=precision))
    return jnp.concatenate(re, axis=1), jnp.concatenate(im, axis=1)


def _ssm_out(x_re, x_im, wc_re_ref, wc_im_ref, precision=None):
    halves, nh, _ = wc_re_ref.shape
    ys = []
    for i in range(halves):
        ys.append(jnp.dot(x_re[:, i * nh:(i + 1) * nh], wc_re_ref[i], preferred_element_type=F32, precision=precision)
                  + jnp.dot(x_im[:, i * nh:(i + 1) * nh], wc_im_ref[i], preferred_element_type=F32,
                            precision=precision))
    return jnp.concatenate(ys, axis=1)


def _glu_tail(y, u32, dskip_ref, wglu_ref, bglu_ref, sgb, sw, precision=None):
    y = y + dskip_ref[...] * u32
    za = jax.nn.gelu(y).astype(wglu_ref.dtype)
    glu = jnp.dot(za, wglu_ref[...], preferred_element_type=F32, precision=precision) + bglu_ref[...]
    return glu[:, :sw] * jax.nn.sigmoid(glu[:, sw:]) * sgb


def _s5_prompt_kernel(u_ref, sgb_ref, are_ref, aim_ref, coef_re_ref, coef_im_ref, bt_re_ref, bt_im_ref,
                      ct_re_ref, ct_im_ref, dskip_ref, wglu_ref, bglu_ref,
                      out_ref, fre_ref, fim_ref,
                      wb_re_ref, wb_im_ref, wc_re_ref, wc_im_ref, bu_re_ref, bu_im_ref, xs_re_ref, xs_im_ref,
                      st_re_ref, st_im_ref, *, gch, nstate, sw):
    b = pl.program_id(0)
    t = pl.program_id(1)
    tl = u_ref.shape[1]

    @pl.when((b == 0) & (t == 0))
    def _():
        _build_ssm_weights(coef_re_ref, coef_im_ref, bt_re_ref, bt_im_ref, ct_re_ref, ct_im_ref,
                           wb_re_ref, wb_im_ref, wc_re_ref, wc_im_ref, gch=gch, nstate=nstate)

    @pl.when(t == 0)
    def _():
        st_re_ref[...] = jnp.zeros_like(st_re_ref)
        st_im_ref[...] = jnp.zeros_like(st_im_ref)

    u = u_ref[0]
    bu_re, bu_im = _ssm_in(u, wb_re_ref, wb_im_ref)
    bu_re_ref[...] = bu_re
    bu_im_ref[...] = bu_im
    a_re = are_ref[...]
    a_im = aim_ref[...]

    def step(i, carry):
        xr, xi = carry
        nr = a_re * xr - a_im * xi + bu_re_ref[pl.ds(i, 1), :]
        ni = a_re * xi + a_im * xr + bu_im_ref[pl.ds(i, 1), :]
        xs_re_ref[pl.ds(i, 1), :] = nr
        xs_im_ref[pl.ds(i, 1), :] = ni
        return nr, ni

    xr, xi = lax.fori_loop(0, tl, step, (st_re_ref[...], st_im_ref[...]), unroll=8)
    st_re_ref[...] = xr
    st_im_ref[...] = xi
    fre_ref[0] = xr
    fim_ref[0] = xi

    y = _ssm_out(xs_re_ref[...].astype(BF16), xs_im_ref[...].astype(BF16), wc_re_ref, wc_im_ref)
    out = _glu_tail(y, u.astype(F32), dskip_ref, wglu_ref, bglu_ref, sgb_ref[0].astype(F32), sw)
    out_ref[0] = out.astype(out_ref.dtype)


def _s5_prompt(u, sgb, a_re, a_im, coef_re, coef_im, bt_re, bt_im, ct_re, ct_im, d_skip, w_glu, b_glu, *, tl,
               gch, nstate):
    b, l, sw = u.shape
    ns = a_re.shape[1]
    halves = 2
    kh = sw // halves
    nh = ns // halves
    kern = functools.partial(_s5_prompt_kernel, gch=gch, nstate=nstate, sw=sw)
    tile = pl.BlockSpec((1, tl, sw), lambda bi, t: (bi, t, 0))
    c2 = lambda shape: _const_spec(shape, 2)
    return pl.pallas_call(
        kern,
        grid=(b, l // tl),
        in_specs=[tile, tile, c2((1, ns)), c2((1, ns)), c2((halves, 1, nh)), c2((halves, 1, nh)),
                  c2(bt_re.shape), c2(bt_im.shape), c2(ct_re.shape), c2(ct_im.shape),
                  c2((1, sw)), c2(w_glu.shape), c2((1, 2 * sw))],
        out_specs=[tile,
                   pl.BlockSpec((1, 1, ns), lambda bi, t: (bi, 0, 0)),
                   pl.BlockSpec((1, 1, ns), lambda bi, t: (bi, 0, 0))],
        out_shape=[jax.ShapeDtypeStruct((b, l, sw), BF16),
                   jax.ShapeDtypeStruct((b, 1, ns), F32),
                   jax.ShapeDtypeStruct((b, 1, ns), F32)],
        scratch_shapes=[
            pltpu.VMEM((halves, kh, nh), BF16), pltpu.VMEM((halves, kh, nh), BF16),
            pltpu.VMEM((halves, nh, kh), BF16), pltpu.VMEM((halves, nh, kh), BF16),
            pltpu.VMEM((tl, ns), F32), pltpu.VMEM((tl, ns), F32),
            pltpu.VMEM((tl, ns), F32), pltpu.VMEM((tl, ns), F32),
            pltpu.VMEM((1, ns), F32), pltpu.VMEM((1, ns), F32),
        ],
        compiler_params=_cparams(2),
        name="s5_prompt",
    )(u, sgb, a_re, a_im, coef_re, coef_im, bt_re, bt_im, ct_re, ct_im, d_skip, w_glu, b_glu)


def _s5_sample_kernel(u_ref, sgb_ref, sre_ref, sim_ref, are_ref, aim_ref, coef_re_ref, coef_im_ref,
                      bt_re_ref, bt_im_ref, ct_re_ref, ct_im_ref, dskip_ref, wglu_ref, bglu_ref,
                      out_ref, nre_ref, nim_ref, wb_re_ref, wb_im_ref, wc_re_ref, wc_im_ref, *, gch, nstate, sw):
    _build_ssm_weights(coef_re_ref, coef_im_ref, bt_re_ref, bt_im_ref, ct_re_ref, ct_im_ref,
                       wb_re_ref, wb_im_ref, wc_re_ref, wc_im_ref, gch=gch, nstate=nstate)
    u = u_ref[...]
    bu_re, bu_im = _ssm_in(u, wb_re_ref, wb_im_ref, precision=HIGHEST)
    a_re = are_ref[...]
    a_im = aim_ref[...]
    s_re = sre_ref[...]
    s_im = sim_ref[...]
    x_re = a_re * s_re - a_im * s_im + bu_re
    x_im = a_re * s_im + a_im * s_re + bu_im
    nre_ref[...] = x_re
    nim_ref[...] = x_im
    y = _ssm_out(x_re, x_im, wc_re_ref, wc_im_ref, precision=HIGHEST)
    out_ref[...] = _glu_tail(y, u, dskip_ref, wglu_ref, bglu_ref, sgb_ref[...], sw, precision=HIGHEST)


def _s5_sample(u, sgb, s_re, s_im, a_re, a_im, coef_re, coef_im, bt_re, bt_im, ct_re, ct_im, d_skip, w_glu,
               b_glu, *, gch, nstate):
    n, sw = u.shape
    ns = a_re.shape[1]
    halves = 2
    kh = sw // halves
    nh = ns // halves
    kern = functools.partial(_s5_sample_kernel, gch=gch, nstate=nstate, sw=sw)
    return pl.pallas_call(
        kern,
        out_shape=[jax.ShapeDtypeStruct((n, sw), F32),
                   jax.ShapeDtypeStruct((n, ns), F32),
                   jax.ShapeDtypeStruct((n, ns), F32)],
        scratch_shapes=[
            pltpu.VMEM((halves, kh, nh), F32), pltpu.VMEM((halves, kh, nh), F32),
            pltpu.VMEM((halves, nh, kh), F32), pltpu.VMEM((halves, nh, kh), F32),
        ],
        compiler_params=pltpu.CompilerParams(vmem_limit_bytes=VMEM_LIMIT),
        name="s5_sample",
    )(u, sgb, s_re, s_im, a_re, a_im, coef_re, coef_im, bt_re, bt_im, ct_re, ct_im, d_skip, w_glu, b_glu)


def _proj_out_kernel(x_ref, attn_ref, sga_ref, ssm_ref, sma_ref, smb_ref, wa_ref, wb_ref, wo_ref, o_ref):
    attn_out = (attn_ref[...].astype(F32) * sga_ref[...].astype(F32)).astype(BF16)
    ya = jnp.dot(attn_out, wa_ref[...], preferred_element_type=F32)
    yb = jnp.dot(ssm_ref[...].astype(BF16), wb_ref[...], preferred_element_type=F32)
    merged = sma_ref[...].astype(F32) * ya + smb_ref[...].astype(F32) * yb
    o_ref[...] = x_ref[...] + jnp.dot(merged.astype(BF16), wo_ref[...], preferred_element_type=F32)


def _proj_out(x2, attn, sga, ssm, sma, smb, wa, wb, wo, *, tm):
    t, dm = x2.shape
    aw = attn.shape[1]
    sw = ssm.shape[1]
    row = lambda w: pl.BlockSpec((tm, w), lambda i: (i, 0))
    return pl.pallas_call(
        _proj_out_kernel,
        grid=(t // tm,),
        in_specs=[row(dm), row(aw), row(aw), row(sw), row(dm), row(dm),
                  _const_spec(wa.shape, 1), _const_spec(wb.shape, 1), _const_spec(wo.shape, 1)],
        out_specs=row(dm),
        out_shape=jax.ShapeDtypeStruct((t, dm), F32),
        compiler_params=_cparams(1),
        name="proj_out",
    )(x2, attn, sga, ssm, sma, smb, wa, wb, wo)


def _col_from_row(row, n):
    eye = lax.broadcasted_iota(jnp.int32, (n, n), 0) == lax.broadcasted_iota(jnp.int32, (n, n), 1)
    return jnp.sum(jnp.where(eye, jnp.broadcast_to(row, (n, n)), 0.0), axis=1, keepdims=True)


def _row_from_col(col, n):
    eye = lax.broadcasted_iota(jnp.int32, (n, n), 0) == lax.broadcasted_iota(jnp.int32, (n, n), 1)
    return jnp.sum(jnp.where(eye, jnp.broadcast_to(col, (n, n)), 0.0), axis=0, keepdims=True)


def _sample_attn_kernel(pt_ref, relb_ref, q_ref, kn_ref, vn_ref, ck_ref, cv_ref, o_ref,
                        kbuf, ksem, qb_ref, lg_ref, tb_ref, selv_ref, sels_ref, selsem, vbuf, vsem,
                        *, n_heads, head_dim, ppb, page, n_pages, n_ring, scale):
    b = pl.program_id(0)
    total = pl.num_programs(0) * n_pages
    n_blocks = n_pages // ppb
    past_len = n_pages * page
    n_sel = MOBA_TOPK

    def k_copy(g):
        phys = pt_ref[lax.div(g, n_pages), lax.rem(g, n_pages)]
        slot = lax.rem(g, n_ring)
        return pltpu.make_async_copy(ck_ref.at[0, phys], kbuf.at[slot], ksem.at[slot])

    head_i = lax.broadcasted_iota(jnp.int32, (n_heads, page), 0)

    @pl.when(b == 0)
    def _():
        for g in range(n_ring - 1):
            k_copy(jnp.int32(g)).start()
        for bk in range(REL_BUCKETS):
            tab = jnp.zeros((n_heads, page), F32)
            for h in range(n_heads):
                tab = jnp.where(head_i == h, relb_ref[bk, h], tab)
            tb_ref[bk] = tab

    q = q_ref[0]
    for h in range(n_heads):
        col = _col_from_row(q[:, h * head_dim:(h + 1) * head_dim], head_dim)
        qb_ref[h] = jnp.broadcast_to(col, (head_dim, page))

    def stream(pg, carry):
        g = b * n_pages + pg
        nxt = g + (n_ring - 1)

        @pl.when(nxt < total)
        def _():
            k_copy(nxt).start()

        k_copy(g).wait()
        kt = kbuf[lax.rem(g, n_ring)]
        lg_ref[pg] = jnp.sum(kt * qb_ref[...], axis=1)
        return carry

    lax.fori_loop(0, n_pages, stream, 0)

    lane = lax.broadcasted_iota(jnp.int32, (n_heads, LANES), 1)
    sc = jnp.full((n_heads, LANES), NEG_INF, F32)
    for jb in range(n_blocks):
        tot = lg_ref[jb * ppb]
        for i in range(1, ppb):
            tot = tot + lg_ref[jb * ppb + i]
        sc = jnp.where(lane == jb, jnp.sum(tot, axis=1, keepdims=True) * (1.0 / MOBA_BLOCK), sc)
    sel = jnp.full((n_heads, LANES), -1, jnp.int32)
    for r in range(n_sel):
        mx = jnp.max(sc, axis=1, keepdims=True)
        idx = jnp.min(jnp.where(sc == mx, lane, LANES), axis=1, keepdims=True)
        ok = mx > NEG_INF
        sel = jnp.where(lane == r, jnp.where(ok, idx, -1), sel)
        sc = jnp.where((lane == idx) & ok, NEG_INF, sc)
    selv_ref[...] = sel
    to_smem = pltpu.make_async_copy(selv_ref, sels_ref, selsem)
    to_smem.start()
    to_smem.wait()

    def chosen(h, r, i):
        j = sels_ref[h, r]
        pgl = jnp.maximum(j, 0) * ppb + i
        return j, pgl, (h * n_sel + r) * ppb + i

    def v_copy(h, r, i):
        _, pgl, slot = chosen(h, r, i)
        return pltpu.make_async_copy(cv_ref.at[0, pt_ref[b, pgl], h], vbuf.at[slot], vsem.at[slot])

    for h in range(n_heads):
        for r in range(n_sel):
            for i in range(ppb):
                @pl.when(chosen(h, r, i)[0] >= 0)
                def _():
                    v_copy(h, r, i).start()

    pos = lax.broadcasted_iota(jnp.int32, (n_heads, page), 1)
    qk_new = q * kn_ref[0]
    s_self = jnp.zeros((n_heads, 1), F32)
    for h in range(n_heads):
        dot_h = jnp.sum(qk_new[:, h * head_dim:(h + 1) * head_dim], axis=1, keepdims=True)
        s_self = jnp.where(head_i[:, 0:1] == h, dot_h, s_self)
    s_self = s_self * scale + tb_ref[0][:, 0:1]
    tiles = []
    for r in range(n_sel):
        blk_r = jnp.sum(jnp.where(lane == r, sel, 0), axis=1, keepdims=True)
        for i in range(ppb):
            raw = jnp.zeros((n_heads, page), F32)
            for h in range(n_heads):
                raw = jnp.where(head_i == h, lg_ref[chosen(h, r, i)[1]][h:h + 1, :], raw)
            bucket = _rel_bucket(past_len - (jnp.maximum(blk_r, 0) * ppb + i) * page - pos)
            bias = jnp.zeros((n_heads, page), F32)
            for bk in range(REL_BUCKETS):
                bias = jnp.where(bucket == bk, tb_ref[bk], bias)
            tiles.append(jnp.where(blk_r >= 0, raw * scale + bias, NEG_INF))
    m = s_self
    for s in tiles:
        m = jnp.maximum(m, jnp.max(s, axis=1, keepdims=True))
    p_self = jnp.exp(s_self - m)
    den = p_self
    probs = []
    for s in tiles:
        p = jnp.exp(s - m)
        den = den + jnp.sum(p, axis=1, keepdims=True)
        probs.append(p)
    rows = []
    for h in range(n_heads):
        acc = jnp.zeros((head_dim, page), F32)
        for n, p in enumerate(probs):
            j, _, slot = chosen(h, n // ppb, n % ppb)

            @pl.when(j >= 0)
            def _():
                v_copy(h, n // ppb, n % ppb).wait()

            acc = acc + jnp.where(j >= 0, vbuf[slot], 0.0) * p[h:h + 1, :]
        o_row = _row_from_col(jnp.sum(acc, axis=1, keepdims=True), head_dim)
        v_self = vn_ref[0][:, h * head_dim:(h + 1) * head_dim]
        rows.append((o_row + p_self[h:h + 1, :] * v_self) / den[h:h + 1, :])
    o_ref[0] = jnp.concatenate(rows, axis=1)


def _sample_attention(page_table, rel_bias, q, k_new, v_new, cache_kt, cache_vt, *, ppb, n_ring=32):
    bd, w = q.shape
    _, n_pool, n_heads, head_dim, page = cache_kt.shape
    n_pages = page_table.shape[1]
    assert n_pages % ppb == 0 and n_pages // ppb <= LANES and n_heads * head_dim == w
    n_slots = n_heads * MOBA_TOPK * ppb
    n_ring = min(n_ring, bd * n_pages)
    kern = functools.partial(_sample_attn_kernel, n_heads=n_heads, head_dim=head_dim, ppb=ppb, page=page,
                             n_pages=n_pages, n_ring=n_ring, scale=head_dim ** -0.5)
    row = pl.BlockSpec((1, 1, w), lambda b, pt: (b, 0, 0))
    grid_spec = pltpu.PrefetchScalarGridSpec(
        num_scalar_prefetch=1,
        grid=(bd,),
        in_specs=[pl.BlockSpec(memory_space=pltpu.SMEM), row, row, row,
                  pl.BlockSpec(memory_space=pl.ANY), pl.BlockSpec(memory_space=pl.ANY)],
        out_specs=row,
        scratch_shapes=[
            pltpu.VMEM((n_ring, n_heads, head_dim, page), F32),
            pltpu.SemaphoreType.DMA((n_ring,)),
            pltpu.VMEM((n_heads, head_dim, page), F32),
            pltpu.VMEM((n_pages, n_heads, page), F32),
            pltpu.VMEM((REL_BUCKETS, n_heads, page), F32),
            pltpu.VMEM((n_heads, LANES), jnp.int32),
            pltpu.SMEM((n_heads, LANES), jnp.int32),
            pltpu.SemaphoreType.DMA(()),
            pltpu.VMEM((n_slots, head_dim, page), F32),
            pltpu.SemaphoreType.DMA((n_slots,)),
        ],
    )
    out = pl.pallas_call(
        kern,
        grid_spec=grid_spec,
        out_shape=jax.ShapeDtypeStruct((bd, 1, w), F32),
        compiler_params=_cparams(1),
        name="sample_attn",
    )(page_table, rel_bias, q.reshape(bd, 1, w), k_new.reshape(bd, 1, w), v_new.reshape(bd, 1, w),
      cache_kt, cache_vt)
    return out.reshape(bd, w)


def _layer(x_prompt, x_sample, cache_k, cache_v, state_re, state_im, page_table, rel_bias, norm_g, w_in,
           q_norm_g, k_norm_g, lam_re, lam_im, log_dt, b_re, b_im, c_re, c_im, d_skip, w_glu, b_glu,
           w_branch_a, w_branch_b, w_out):
    bsz, seq, dm = x_prompt.shape
    bd, s_new, _ = x_sample.shape
    _, n_pool, page, n_heads, head_dim = cache_k.shape
    n_groups, n_state, gch = b_re.shape
    aw = n_heads * head_dim
    sw = n_groups * gch
    ns = n_groups * n_state
    n_pages = page_table.shape[1]
    past_len = n_pages * page
    ppb = MOBA_BLOCK // page
    assert s_new == 1 and seq % MOBA_BLOCK == 0 and past_len % MOBA_BLOCK == 0
    assert w_in.shape == (dm, 4 * aw + 2 * sw + 2 * dm)

    w_in_bf = w_in.astype(BF16)
    g_row = norm_g.reshape(1, dm)
    qg_t = jnp.tile(q_norm_g.reshape(1, head_dim), (1, n_heads))
    kg_t = jnp.tile(k_norm_g.reshape(1, head_dim), (1, n_heads))
    head_of = jnp.arange(aw) // head_dim
    bd_ones = (head_of[:, None] == head_of[None, :]).astype(BF16)
    halves = 2
    bt_re = jnp.tile(b_re.transpose(0, 2, 1).reshape(sw, n_state), (1, n_groups // halves))
    bt_im = jnp.tile(b_im.transpose(0, 2, 1).reshape(sw, n_state), (1, n_groups // halves))
    ct_re = jnp.tile(c_re.transpose(0, 2, 1).reshape(ns, gch), (1, n_groups // halves))
    ct_im = jnp.tile(c_im.transpose(0, 2, 1).reshape(ns, gch), (1, n_groups // halves))
    dsk = d_skip.reshape(1, sw)
    bgl = b_glu.reshape(1, 2 * sw)

    a_re, a_im, coef_re, coef_im = _ssm_prep(lam_re, lam_im, log_dt)
    a_re = a_re.reshape(1, ns)
    a_im = a_im.reshape(1, ns)
    coef_re = coef_re.reshape(halves, 1, ns // halves)
    coef_im = coef_im.reshape(halves, 1, ns // halves)

    wa_bf = w_branch_a.astype(BF16)
    wb_bf = w_branch_b.astype(BF16)
    wo_bf = w_out.astype(BF16)
    wglu_bf = w_glu.astype(BF16)

    xp2 = x_prompt.reshape(bsz * seq, dm)
    (q_p, k_p, v_p, kb_p, vb_p, ksum_p, sga_p, u_p, sgb_p, sma_p, smb_p) = _proj_in(
        xp2, g_row, w_in_bf, qg_t, kg_t, bd_ones, tm=MOBA_BLOCK, aw=aw, sw=sw, head_dim=head_dim,
        with_ksum=True, q_dtype=BF16, act_dtype=BF16)
    n_blocks = seq // MOBA_BLOCK
    bias = _bias_tiles(rel_bias, n_heads, _n_bias_tiles(n_blocks))
    attn_p = _attention(q_p.reshape(bsz, seq, aw), kb_p.reshape(bsz, seq, aw), vb_p.reshape(bsz, seq, aw),
                        ksum_p.reshape(bsz, n_blocks, aw), bias, head_dim=head_dim)
    ssm_p, fre_p, fim_p = _s5_prompt(u_p.reshape(bsz, seq, sw), sgb_p.reshape(bsz, seq, sw), a_re, a_im, coef_re,
                                     coef_im, bt_re, bt_im, ct_re, ct_im, dsk, wglu_bf, bgl, tl=MOBA_BLOCK,
                                     gch=gch, nstate=n_state)
    y_p = _proj_out(xp2, attn_p.reshape(bsz * seq, aw), sga_p, ssm_p.reshape(bsz * seq, sw), sma_p, smb_p,
                    wa_bf, wb_bf, wo_bf, tm=MOBA_BLOCK)

    xs2 = x_sample.reshape(bd, dm)
    (q_s, k_s, v_s, _, _, sga_s, u_s, sgb_s, sma_s, smb_s) = _proj_in(
        xs2, g_row, w_in_bf, qg_t, kg_t, bd_ones, tm=bd, aw=aw, sw=sw, head_dim=head_dim,
        with_ksum=False, q_dtype=F32, act_dtype=F32)
    cache_kt = cache_k.transpose(0, 1, 3, 4, 2)
    cache_vt = cache_v.transpose(0, 1, 3, 4, 2)
    attn_s = _sample_attention(page_table, rel_bias, q_s, k_s, v_s, cache_kt, cache_vt, ppb=ppb)
    ssm_s, nre_s, nim_s = _s5_sample(u_s, sgb_s, state_re.reshape(bd, ns), state_im.reshape(bd, ns), a_re, a_im,
                                     coef_re, coef_im, bt_re, bt_im, ct_re, ct_im, dsk, w_glu, bgl, gch=gch,
                                     nstate=n_state)
    y_s = _proj_out(xs2, attn_s, sga_s, ssm_s, sma_s, smb_s, wa_bf, wb_bf, wo_bf, tm=bd)

    return (
        y_p.reshape(bsz, seq, dm), y_s.reshape(bd, 1, dm),
        k_p.reshape(bsz, seq, n_heads, head_dim), v_p.reshape(bsz, seq, n_heads, head_dim),
        fre_p.reshape(bsz, n_groups, n_state), fim_p.reshape(bsz, n_groups, n_state),
        k_s.reshape(bd, 1, n_heads, head_dim), v_s.reshape(bd, 1, n_heads, head_dim),
        nre_s.reshape(bd, n_groups, n_state), nim_s.reshape(bd, n_groups, n_state),
    )


def kernel(x_prompt, x_sample, cache_k, cache_v, state_ssm_re, state_ssm_im, page_table, rel_bias, norm_g, w_in,
           q_norm_g, k_norm_g, lam_re, lam_im, log_dt, b_re, b_im, c_re, c_im, d_skip, w_glu, b_glu, w_branch_a,
           w_branch_b, w_out):
    depth = norm_g.shape[0]
    assert depth == 1, "one layer per step"
    outs = _layer(x_prompt, x_sample, cache_k, cache_v, state_ssm_re[0], state_ssm_im[0], page_table,
                  rel_bias, norm_g[0], w_in[0], q_norm_g[0], k_norm_g[0], lam_re[0], lam_im[0], log_dt[0],
                  b_re[0], b_im[0], c_re[0], c_im[0], d_skip[0], w_glu[0], b_glu[0], w_branch_a[0],
                  w_branch_b[0], w_out[0])
    y_p, y_s, k_p, v_p, fre_p, fim_p, k_s, v_s, nre_s, nim_s = outs
    add = lambda a: a[None]
    return (y_p, y_s, add(k_p), add(v_p), add(fre_p), add(fim_p), add(k_s), add(v_s), add(nre_s), add(nim_s))
```

```python
import functools
import math

import jax
import jax.numpy as jnp
from jax import lax
from jax.experimental import pallas as pl
from jax.experimental.pallas import tpu as pltpu

F32 = jnp.float32
BF16 = jnp.bfloat16
HIGHEST = lax.Precision.HIGHEST

MOBA_BLOCK = 256
MOBA_TOPK = 3
REL_BUCKETS = 32
REL_MAX_DIST = 4096
RMS_EPS = 1e-6
LOG2E = math.log2(math.e)
LANES = 128
BF16_SUBLANES = 16
VMEM_LIMIT = 56 * 1024 * 1024

NEG_INF = float("-inf")


def _cparams(n_axes, vmem=VMEM_LIMIT):
    return pltpu.CompilerParams(dimension_semantics=("arbitrary",) * n_axes, vmem_limit_bytes=vmem)


def _const_spec(shape, n_grid):
    zeros = (0,) * len(shape)
    if n_grid == 1:
        imap = lambda i: zeros
    elif n_grid == 2:
        imap = lambda i, j: zeros
    else:
        imap = lambda i, j, k: zeros
    return pl.BlockSpec(shape, imap, pipeline_mode=pl.Buffered(1))


def _proj_in_kernel(x_ref, g_ref, w_ref, qg_ref, kg_ref, bd_ref, *out_refs, aw, sw, dm, head_dim, prompt):
    if prompt:
        (q_ref, kb_ref, ksum_ref, kt_ref, vt_ref, vta_ref, sga_ref, u_ref, sgb_ref, sma_ref, smb_ref) = out_refs
    else:
        (q_ref, k_ref, v_ref, sga_ref, u_ref, sgb_ref, sma_ref, smb_ref) = out_refs
    x = x_ref[...]
    ms = jnp.mean(x * x, axis=-1, keepdims=True)
    h = (x * lax.rsqrt(ms + RMS_EPS) * g_ref[...]).astype(BF16)

    def seg(a, b):
        return jnp.dot(h, w_ref[:, a:b], preferred_element_type=F32)

    def headnorm(z, gain):
        ss = jnp.dot((z * z).astype(BF16), bd_ref[...], preferred_element_type=F32)
        return z * lax.rsqrt(ss * (1.0 / head_dim) + RMS_EPS) * gain

    o = 0
    q = headnorm(seg(o, o + aw), qg_ref[...])
    q_ref[...] = q.astype(q_ref.dtype)
    o += aw
    k = headnorm(seg(o, o + aw), kg_ref[...])
    o += aw
    v = seg(o, o + aw)
    o += aw
    if prompt:
        kb_ref[...] = k.astype(BF16)
        ksum_ref[0] = jnp.sum(k, axis=0, keepdims=True)
        kt_ref[0] = k.T
        vt = v.T
        vt_ref[0] = vt
        ones = jnp.ones((BF16_SUBLANES, vt.shape[1]), BF16)
        rows = head_dim + BF16_SUBLANES
        for hd in range(aw // head_dim):
            vta_ref[0, 0, hd * rows:hd * rows + head_dim, :] = vt[hd * head_dim:(hd + 1) * head_dim, :].astype(BF16)
            vta_ref[0, 0, hd * rows + head_dim:(hd + 1) * rows, :] = ones
    else:
        k_ref[...] = k
        v_ref[...] = v
    sga_ref[...] = jax.nn.silu(seg(o, o + aw)).astype(sga_ref.dtype)
    o += aw
    u_ref[...] = seg(o, o + sw).astype(u_ref.dtype)
    o += sw
    sgb_ref[...] = jax.nn.silu(seg(o, o + sw)).astype(sgb_ref.dtype)
    o += sw
    sma_ref[...] = jax.nn.sigmoid(seg(o, o + dm)).astype(sma_ref.dtype)
    o += dm
    smb_ref[...] = jax.nn.sigmoid(seg(o, o + dm)).astype(smb_ref.dtype)


def _proj_in(x2, g, w_bf, qg_t, kg_t, bd, *, tm, aw, sw, head_dim, q_dtype, act_dtype, prompt_batch=None):
    t, dm = x2.shape
    n_tiles = t // tm
    row = lambda w: pl.BlockSpec((tm, w), lambda i: (i, 0))
    prompt = prompt_batch is not None
    if prompt:
        assert tm == MOBA_BLOCK and n_tiles % prompt_batch == 0
        per_seq = n_tiles // prompt_batch
        seq = per_seq * tm
        n_heads = aw // head_dim
        va_rows = n_heads * (head_dim + BF16_SUBLANES)
        t_spec = pl.BlockSpec((1, aw, tm), lambda i: (i // per_seq, 0, i % per_seq))
        out_shape = [
            jax.ShapeDtypeStruct((t, aw), q_dtype),
            jax.ShapeDtypeStruct((t, aw), BF16),
            jax.ShapeDtypeStruct((n_tiles, 1, aw), F32),
            jax.ShapeDtypeStruct((prompt_batch, aw, seq), F32),
            jax.ShapeDtypeStruct((prompt_batch, aw, seq), F32),
            jax.ShapeDtypeStruct((prompt_batch, per_seq, va_rows, tm), BF16),
        ]
        out_specs = [row(aw), row(aw), pl.BlockSpec((1, 1, aw), lambda i: (i, 0, 0)), t_spec, t_spec,
                     pl.BlockSpec((1, 1, va_rows, tm), lambda i: (i // per_seq, i % per_seq, 0, 0))]
    else:
        out_shape = [
            jax.ShapeDtypeStruct((t, aw), q_dtype),
            jax.ShapeDtypeStruct((t, aw), F32),
            jax.ShapeDtypeStruct((t, aw), F32),
        ]
        out_specs = [row(aw)] * 3
    out_shape += [
        jax.ShapeDtypeStruct((t, aw), act_dtype),
        jax.ShapeDtypeStruct((t, sw), act_dtype),
        jax.ShapeDtypeStruct((t, sw), act_dtype),
        jax.ShapeDtypeStruct((t, dm), act_dtype),
        jax.ShapeDtypeStruct((t, dm), act_dtype),
    ]
    out_specs += [row(aw), row(sw), row(sw), row(dm), row(dm)]
    kern = functools.partial(_proj_in_kernel, aw=aw, sw=sw, dm=dm, head_dim=head_dim, prompt=prompt)
    return pl.pallas_call(
        kern,
        grid=(n_tiles,),
        in_specs=[
            row(dm),
            _const_spec((1, dm), 1),
            _const_spec(w_bf.shape, 1),
            _const_spec((1, aw), 1),
            _const_spec((1, aw), 1),
            _const_spec((aw, aw), 1),
        ],
        out_specs=out_specs,
        out_shape=out_shape,
        compiler_params=_cparams(1),
        name="proj_in",
    )(x2, g, w_bf, qg_t, kg_t, bd)


def _rel_bucket(dist):
    n = jnp.maximum(dist, 0)
    max_exact = REL_BUCKETS // 2
    nf = jnp.maximum(n, 1).astype(F32)
    large = max_exact + (
        jnp.log(nf / max_exact) * ((REL_BUCKETS - max_exact) / math.log(REL_MAX_DIST / max_exact))
    ).astype(jnp.int32)
    return jnp.where(n < max_exact, n, jnp.minimum(large, REL_BUCKETS - 1))


def _bias_lookup(bucket, relb_ref, head):
    val = jnp.zeros(bucket.shape, F32)
    for b in range(REL_BUCKETS):
        val = jnp.where(bucket == b, relb_ref[b, head], val)
    return val


def _bias_tiles_kernel(relb_ref, o_ref):
    h = pl.program_id(0)
    d = pl.program_id(1)
    key = lax.broadcasted_iota(jnp.int32, (MOBA_BLOCK, MOBA_BLOCK), 0)
    qry = lax.broadcasted_iota(jnp.int32, (MOBA_BLOCK, MOBA_BLOCK), 1)
    dist = d * MOBA_BLOCK + qry - key
    bias = _bias_lookup(_rel_bucket(dist), relb_ref, h) * LOG2E
    o_ref[0, 0] = jnp.where(dist >= 0, bias, NEG_INF)


def _n_bias_tiles(n_blocks):
    saturated = -(-(REL_MAX_DIST + MOBA_BLOCK - 1) // MOBA_BLOCK) + 1
    return min(n_blocks, saturated)


def _bias_tiles(rel_bias, n_heads, n_dist):
    return pl.pallas_call(
        _bias_tiles_kernel,
        grid=(n_heads, n_dist),
        in_specs=[pl.BlockSpec(memory_space=pltpu.SMEM)],
        out_specs=pl.BlockSpec((1, 1, MOBA_BLOCK, MOBA_BLOCK), lambda h, d: (h, d, 0, 0)),
        out_shape=jax.ShapeDtypeStruct((n_heads, n_dist, MOBA_BLOCK, MOBA_BLOCK), F32),
        compiler_params=_cparams(2),
        name="bias_tiles",
    )(rel_bias)


def _select_topk(s):
    rowi = lax.broadcasted_iota(jnp.int32, s.shape, 0)
    sel = jnp.zeros(s.shape, jnp.bool_)
    for _ in range(MOBA_TOPK):
        mx = jnp.max(s, axis=0, keepdims=True)
        cand = jnp.where(s == mx, rowi, s.shape[0])
        idx = jnp.min(cand, axis=0, keepdims=True)
        hit = (rowi == idx) & (mx > NEG_INF)
        sel = sel | hit
        s = jnp.where(hit, NEG_INF, s)
    return sel


def _attn_kernel(q_ref, kb_ref, vta_ref, ksum_ref, bias_ref, o_ref,
                 qt0, qt1, sel0, sel1, m0, m1, acc0, acc1, sbuf0, sbuf1, cbuf0, cbuf1,
                 *, head_dim, n_dist, scale):
    t = pl.program_id(2)
    blk = MOBA_BLOCK
    n_blocks = vta_ref.shape[1]
    pair = 2 * head_dim
    v_rows = head_dim + BF16_SUBLANES
    qts, sels, ms, accs = (qt0, qt1), (sel0, sel1), (m0, m1), (acc0, acc1)
    s_bufs, c_bufs = (sbuf0, sbuf1), (cbuf0, cbuf1)

    qt = q_ref[0].astype(F32).T
    feat = lax.broadcasted_iota(jnp.int32, (pair, blk), 0)
    means = ksum_ref[0] * (1.0 / blk)
    blk_i = lax.broadcasted_iota(jnp.int32, (n_blocks, blk), 0)
    for h in range(2):
        in_head = (feat >= h * head_dim) & (feat < (h + 1) * head_dim)
        qt_h = jnp.where(in_head, qt, 0.0)
        qts[h][...] = (qt_h * (scale * LOG2E)).astype(BF16)
        sc = jnp.dot(means, qt_h, preferred_element_type=F32, precision=HIGHEST)
        sc = jnp.where(blk_i < t, sc, NEG_INF)
        sels[h][...] = (_select_topk(sc) | (blk_i == t)).astype(F32)
        ms[h][...] = jnp.full(ms[h].shape, NEG_INF, F32)
        accs[h][...] = jnp.zeros(accs[h].shape, F32)

    def block_of(n):
        return jnp.where(n == 0, t, jnp.where(n <= t, n - 1, 0))

    def stage_a(n, slot):
        j = block_of(n)
        kj = kb_ref[0, pl.ds(pl.multiple_of(j * blk, blk), blk), :]
        tile = jnp.minimum(t - j, n_dist - 1)
        for h in range(2):
            s = jnp.dot(kj, qts[h][...], preferred_element_type=F32) + bias_ref[h, tile]
            s_bufs[h][slot] = s
            c_bufs[h][slot] = jnp.max(s, axis=0, keepdims=True)

    def stage_b(n, slot):
        j = block_of(n)
        for h in range(2):
            picked = jnp.where(n <= t, sels[h][pl.ds(j, 1), :], 0.0)
            m_old = ms[h][...]
            m_blk = jnp.maximum(m_old, c_bufs[h][slot])
            m_new = jnp.where(picked > 0.0, m_blk, m_old)
            p = jnp.exp2(s_bufs[h][slot] - m_blk).astype(BF16)
            upd = jnp.dot(vta_ref[0, j, h * v_rows:(h + 1) * v_rows, :], p, preferred_element_type=F32)
            accs[h][...] = accs[h][...] * jnp.exp2(m_old - m_new) + upd * picked
            ms[h][...] = m_new

    stage_a(jnp.int32(0), 0)
    stage_a(jnp.int32(1), 1)

    n_iter = (t + 2) // 2

    def two_blocks(i, carry):
        j = 2 * i
        for parity in range(2):
            cur, nxt = 2 * parity, 2 - 2 * parity

            @pl.when((i % 2 == parity) & (i < n_iter - 1))
            def _():
                stage_a(j + 2, nxt)
                stage_a(j + 3, nxt + 1)
                stage_b(j, cur)
                stage_b(j + 1, cur + 1)

            @pl.when((i % 2 == parity) & (i == n_iter - 1))
            def _():
                stage_b(j, cur)
                stage_b(j + 1, cur + 1)

        return carry

    lax.fori_loop(0, n_iter, two_blocks, 0)

    outs = [accs[h][0:head_dim, :] / accs[h][head_dim:head_dim + 1, :] for h in range(2)]
    o_ref[0] = jnp.concatenate(outs, axis=0).T.astype(o_ref.dtype)


def _attention(q_bf, k_bf, vt_aug, ksum, bias, *, head_dim):
    b, l, aw = q_bf.shape
    pair = 2 * head_dim
    assert pair == LANES and aw % pair == 0 and l % MOBA_BLOCK == 0
    n_pairs = aw // pair
    n_blocks = l // MOBA_BLOCK
    n_dist = _n_bias_tiles(n_blocks)
    assert bias.shape[1] == n_dist
    v_rows = head_dim + BF16_SUBLANES
    assert vt_aug.shape == (b, n_blocks, 2 * n_pairs * v_rows, MOBA_BLOCK)
    kern = functools.partial(_attn_kernel, head_dim=head_dim, n_dist=n_dist, scale=head_dim ** -0.5)
    one = pl.Buffered(1)
    return pl.pallas_call(
        kern,
        grid=(b, n_pairs, n_blocks),
        in_specs=[
            pl.BlockSpec((1, MOBA_BLOCK, pair), lambda bi, hp, t: (bi, t, hp)),
            pl.BlockSpec((1, l, pair), lambda bi, hp, t: (bi, 0, hp), pipeline_mode=one),
            pl.BlockSpec((1, n_blocks, 2 * v_rows, MOBA_BLOCK), lambda bi, hp, t: (bi, 0, hp, 0), pipeline_mode=one),
            pl.BlockSpec((1, n_blocks, pair), lambda bi, hp, t: (bi, 0, hp), pipeline_mode=one),
            pl.BlockSpec((2, n_dist, MOBA_BLOCK, MOBA_BLOCK), lambda bi, hp, t: (hp, 0, 0, 0), pipeline_mode=one),
        ],
        out_specs=pl.BlockSpec((1, MOBA_BLOCK, pair), lambda bi, hp, t: (bi, t, hp)),
        out_shape=jax.ShapeDtypeStruct((b, l, aw), BF16),
        scratch_shapes=(
            [pltpu.VMEM((pair, MOBA_BLOCK), BF16)] * 2
            + [pltpu.VMEM((n_blocks, MOBA_BLOCK), F32)] * 2
            + [pltpu.VMEM((1, MOBA_BLOCK), F32)] * 2
            + [pltpu.VMEM((v_rows, MOBA_BLOCK), F32)] * 2
            + [pltpu.VMEM((4, MOBA_BLOCK, MOBA_BLOCK), F32)] * 2
            + [pltpu.VMEM((4, 1, MOBA_BLOCK), F32)] * 2
        ),
        compiler_params=_cparams(3),
        name="moba_attn",
    )(q_bf, k_bf, vt_aug, ksum, bias)


def _ssm_prep_kernel(lre_ref, lim_ref, ldt_ref, are_ref, aim_ref, cre_ref, cim_ref):
    lre = lre_ref[...]
    lim = lim_ref[...]
    dt = jnp.exp(ldt_ref[...])
    mag = jnp.exp(lre * dt)
    a_re = mag * jnp.cos(lim * dt)
    a_im = mag * jnp.sin(lim * dt)
    are_ref[...] = a_re
    aim_ref[...] = a_im
    nr = a_re - 1.0
    den = lre * lre + lim * lim
    cre_ref[...] = (nr * lre + a_im * lim) / den
    cim_ref[...] = (a_im * lre - nr * lim) / den


def _ssm_prep(lam_re, lam_im, log_dt):
    g, p = lam_re.shape
    shp = jax.ShapeDtypeStruct((g, p), F32)
    return pl.pallas_call(_ssm_prep_kernel, out_shape=[shp] * 4, name="ssm_prep")(
        lam_re, lam_im, log_dt.reshape(g, 1))


def _block_diag_mask(rows, cols, row_group, col_group):
    r = lax.broadcasted_iota(jnp.int32, (rows, cols), 0) // row_group
    c = lax.broadcasted_iota(jnp.int32, (rows, cols), 1) // col_group
    return r == c


def _build_ssm_weights(coef_re_ref, coef_im_ref, bt_re_ref, bt_im_ref, ct_re_ref, ct_im_ref,
                       wb_re_ref, wb_im_ref, wc_re_ref, wc_im_ref, *, gch, nstate):
    halves, kh, nh = wb_re_ref.shape
    mask_b = _block_diag_mask(kh, nh, gch, nstate)
    mask_c = _block_diag_mask(nh, kh, nstate, gch)
    for i in range(halves):
        cr = coef_re_ref[i]
        ci = coef_im_ref[i]
        br = bt_re_ref[i * kh:(i + 1) * kh, :]
        bi = bt_im_ref[i * kh:(i + 1) * kh, :]
        wb_re_ref[i] = jnp.where(mask_b, cr * br - ci * bi, 0.0).astype(wb_re_ref.dtype)
        wb_im_ref[i] = jnp.where(mask_b, cr * bi + ci * br, 0.0).astype(wb_im_ref.dtype)
        wc_re_ref[i] = jnp.where(mask_c, ct_re_ref[i * nh:(i + 1) * nh, :], 0.0).astype(wc_re_ref.dtype)
        wc_im_ref[i] = jnp.where(mask_c, -ct_im_ref[i * nh:(i + 1) * nh, :], 0.0).astype(wc_im_ref.dtype)


def _ssm_in(u, wb_re_ref, wb_im_ref, precision=None):
    halves, kh, _ = wb_re_ref.shape
    re, im = [], []
    for i in range(halves):
        ui = u[:, i * kh:(i + 1) * kh]
        re.append(jnp.dot(ui, wb_re_ref[i], preferred_element_type=F32, precision=precision))
        im.append(jnp.dot(ui, wb_im_ref[i], preferred_element_type=F32, precision=precision))
    return jnp.concatenate(re, axis=1), jnp.concatenate(im, axis=1)


def _ssm_out(x_re, x_im, wc_re_ref, wc_im_ref, precision=None):
    halves, nh, _ = wc_re_ref.shape
    ys = []
    for i in range(halves):
        ys.append(jnp.dot(x_re[:, i * nh:(i + 1) * nh], wc_re_ref[i], preferred_element_type=F32, precision=precision)
                  + jnp.dot(x_im[:, i * nh:(i + 1) * nh], wc_im_ref[i], preferred_element_type=F32,
                            precision=precision))
    return jnp.concatenate(ys, axis=1)


def _glu_tail(y, u32, dskip_ref, wglu_ref, bglu_ref, sgb, sw, precision=None):
    y = y + dskip_ref[...] * u32
    za = jax.nn.gelu(y).astype(wglu_ref.dtype)
    glu = jnp.dot(za, wglu_ref[...], preferred_element_type=F32, precision=precision) + bglu_ref[...]
    return glu[:, :sw] * jax.nn.sigmoid(glu[:, sw:]) * sgb


def _s5_prompt_kernel(u_ref, sgb_ref, are_ref, aim_ref, coef_re_ref, coef_im_ref, bt_re_ref, bt_im_ref,
                      ct_re_ref, ct_im_ref, dskip_ref, wglu_ref, bglu_ref,
                      out_ref, fre_ref, fim_ref,
                      wb_re_ref, wb_im_ref, wc_re_ref, wc_im_ref, bu_re_ref, bu_im_ref, xs_re_ref, xs_im_ref,
                      st_re_ref, st_im_ref, *, gch, nstate, sw):
    b = pl.program_id(0)
    t = pl.program_id(1)
    tl = u_ref.shape[1]

    @pl.when((b == 0) & (t == 0))
    def _():
        _build_ssm_weights(coef_re_ref, coef_im_ref, bt_re_ref, bt_im_ref, ct_re_ref, ct_im_ref,
                           wb_re_ref, wb_im_ref, wc_re_ref, wc_im_ref, gch=gch, nstate=nstate)

    @pl.when(t == 0)
    def _():
        st_re_ref[...] = jnp.zeros_like(st_re_ref)
        st_im_ref[...] = jnp.zeros_like(st_im_ref)

    u = u_ref[0]
    bu_re, bu_im = _ssm_in(u, wb_re_ref, wb_im_ref)
    bu_re_ref[...] = bu_re
    bu_im_ref[...] = bu_im
    a_re = are_ref[...]
    a_im = aim_ref[...]

    def step(i, carry):
        xr, xi = carry
        nr = a_re * xr - a_im * xi + bu_re_ref[pl.ds(i, 1), :]
        ni = a_re * xi + a_im * xr + bu_im_ref[pl.ds(i, 1), :]
        xs_re_ref[pl.ds(i, 1), :] = nr
        xs_im_ref[pl.ds(i, 1), :] = ni
        return nr, ni

    xr, xi = lax.fori_loop(0, tl, step, (st_re_ref[...], st_im_ref[...]), unroll=8)
    st_re_ref[...] = xr
    st_im_ref[...] = xi
    fre_ref[0] = xr
    fim_ref[0] = xi

    y = _ssm_out(xs_re_ref[...].astype(BF16), xs_im_ref[...].astype(BF16), wc_re_ref, wc_im_ref)
    out = _glu_tail(y, u.astype(F32), dskip_ref, wglu_ref, bglu_ref, sgb_ref[0].astype(F32), sw)
    out_ref[0] = out.astype(out_ref.dtype)


def _s5_prompt(u, sgb, a_re, a_im, coef_re, coef_im, bt_re, bt_im, ct_re, ct_im, d_skip, w_glu, b_glu, *, tl,
               gch, nstate):
    b, l, sw = u.shape
    ns = a_re.shape[1]
    halves = 2
    kh = sw // halves
    nh = ns // halves
    kern = functools.partial(_s5_prompt_kernel, gch=gch, nstate=nstate, sw=sw)
    tile = pl.BlockSpec((1, tl, sw), lambda bi, t: (bi, t, 0))
    c2 = lambda shape: _const_spec(shape, 2)
    return pl.pallas_call(
        kern,
        grid=(b, l // tl),
        in_specs=[tile, tile, c2((1, ns)), c2((1, ns)), c2((halves, 1, nh)), c2((halves, 1, nh)),
                  c2(bt_re.shape), c2(bt_im.shape), c2(ct_re.shape), c2(ct_im.shape),
                  c2((1, sw)), c2(w_glu.shape), c2((1, 2 * sw))],
        out_specs=[tile,
                   pl.BlockSpec((1, 1, ns), lambda bi, t: (bi, 0, 0)),
                   pl.BlockSpec((1, 1, ns), lambda bi, t: (bi, 0, 0))],
        out_shape=[jax.ShapeDtypeStruct((b, l, sw), BF16),
                   jax.ShapeDtypeStruct((b, 1, ns), F32),
                   jax.ShapeDtypeStruct((b, 1, ns), F32)],
        scratch_shapes=[
            pltpu.VMEM((halves, kh, nh), BF16), pltpu.VMEM((halves, kh, nh), BF16),
            pltpu.VMEM((halves, nh, kh), BF16), pltpu.VMEM((halves, nh, kh), BF16),
            pltpu.VMEM((tl, ns), F32), pltpu.VMEM((tl, ns), F32),
            pltpu.VMEM((tl, ns), F32), pltpu.VMEM((tl, ns), F32),
            pltpu.VMEM((1, ns), F32), pltpu.VMEM((1, ns), F32),
        ],
        compiler_params=_cparams(2),
        name="s5_prompt",
    )(u, sgb, a_re, a_im, coef_re, coef_im, bt_re, bt_im, ct_re, ct_im, d_skip, w_glu, b_glu)


def _s5_sample_kernel(u_ref, sgb_ref, sre_ref, sim_ref, are_ref, aim_ref, coef_re_ref, coef_im_ref,
                      bt_re_ref, bt_im_ref, ct_re_ref, ct_im_ref, dskip_ref, wglu_ref, bglu_ref,
                      out_ref, nre_ref, nim_ref, wb_re_ref, wb_im_ref, wc_re_ref, wc_im_ref, *, gch, nstate, sw):
    _build_ssm_weights(coef_re_ref, coef_im_ref, bt_re_ref, bt_im_ref, ct_re_ref, ct_im_ref,
                       wb_re_ref, wb_im_ref, wc_re_ref, wc_im_ref, gch=gch, nstate=nstate)
    u = u_ref[...]
    bu_re, bu_im = _ssm_in(u, wb_re_ref, wb_im_ref, precision=HIGHEST)
    a_re = are_ref[...]
    a_im = aim_ref[...]
    s_re = sre_ref[...]
    s_im = sim_ref[...]
    x_re = a_re * s_re - a_im * s_im + bu_re
    x_im = a_re * s_im + a_im * s_re + bu_im
    nre_ref[...] = x_re
    nim_ref[...] = x_im
    y = _ssm_out(x_re, x_im, wc_re_ref, wc_im_ref, precision=HIGHEST)
    out_ref[...] = _glu_tail(y, u, dskip_ref, wglu_ref, bglu_ref, sgb_ref[...], sw, precision=HIGHEST)


def _s5_sample(u, sgb, s_re, s_im, a_re, a_im, coef_re, coef_im, bt_re, bt_im, ct_re, ct_im, d_skip, w_glu,
               b_glu, *, gch, nstate):
    n, sw = u.shape
    ns = a_re.shape[1]
    halves = 2
    kh = sw // halves
    nh = ns // halves
    kern = functools.partial(_s5_sample_kernel, gch=gch, nstate=nstate, sw=sw)
    return pl.pallas_call(
        kern,
        out_shape=[jax.ShapeDtypeStruct((n, sw), F32),
                   jax.ShapeDtypeStruct((n, ns), F32),
                   jax.ShapeDtypeStruct((n, ns), F32)],
        scratch_shapes=[
            pltpu.VMEM((halves, kh, nh), F32), pltpu.VMEM((halves, kh, nh), F32),
            pltpu.VMEM((halves, nh, kh), F32), pltpu.VMEM((halves, nh, kh), F32),
        ],
        compiler_params=pltpu.CompilerParams(vmem_limit_bytes=VMEM_LIMIT),
        name="s5_sample",
    )(u, sgb, s_re, s_im, a_re, a_im, coef_re, coef_im, bt_re, bt_im, ct_re, ct_im, d_skip, w_glu, b_glu)


def _proj_out_kernel(x_ref, attn_ref, sga_ref, ssm_ref, sma_ref, smb_ref, wa_ref, wb_ref, wo_ref, o_ref):
    attn_out = (attn_ref[...].astype(F32) * sga_ref[...].astype(F32)).astype(BF16)
    ya = jnp.dot(attn_out, wa_ref[...], preferred_element_type=F32)
    yb = jnp.dot(ssm_ref[...].astype(BF16), wb_ref[...], preferred_element_type=F32)
    merged = sma_ref[...].astype(F32) * ya + smb_ref[...].astype(F32) * yb
    o_ref[...] = x_ref[...] + jnp.dot(merged.astype(BF16), wo_ref[...], preferred_element_type=F32)


def _proj_out(x2, attn, sga, ssm, sma, smb, wa, wb, wo, *, tm):
    t, dm = x2.shape
    aw = attn.shape[1]
    sw = ssm.shape[1]
    row = lambda w: pl.BlockSpec((tm, w), lambda i: (i, 0))
    return pl.pallas_call(
        _proj_out_kernel,
        grid=(t // tm,),
        in_specs=[row(dm), row(aw), row(aw), row(sw), row(dm), row(dm),
                  _const_spec(wa.shape, 1), _const_spec(wb.shape, 1), _const_spec(wo.shape, 1)],
        out_specs=row(dm),
        out_shape=jax.ShapeDtypeStruct((t, dm), F32),
        compiler_params=_cparams(1),
        name="proj_out",
    )(x2, attn, sga, ssm, sma, smb, wa, wb, wo)


def _col_from_row(row, n):
    eye = lax.broadcasted_iota(jnp.int32, (n, n), 0) == lax.broadcasted_iota(jnp.int32, (n, n), 1)
    return jnp.sum(jnp.where(eye, jnp.broadcast_to(row, (n, n)), 0.0), axis=1, keepdims=True)


def _row_from_col(col, n):
    eye = lax.broadcasted_iota(jnp.int32, (n, n), 0) == lax.broadcasted_iota(jnp.int32, (n, n), 1)
    return jnp.sum(jnp.where(eye, jnp.broadcast_to(col, (n, n)), 0.0), axis=0, keepdims=True)


def _sample_attn_kernel(pt_ref, relb_ref, q_ref, kn_ref, vn_ref, ck_ref, cv_ref, o_ref,
                        kbuf, ksem, qb_ref, lg_ref, tb_ref, selv_ref, sels_ref, selsem, vbuf, vsem,
                        *, n_heads, head_dim, ppb, page, n_pages, n_ring, scale):
    b = pl.program_id(0)
    total = pl.num_programs(0) * n_pages
    n_blocks = n_pages // ppb
    past_len = n_pages * page
    n_sel = MOBA_TOPK

    def k_copy(g):
        phys = pt_ref[lax.div(g, n_pages), lax.rem(g, n_pages)]
        slot = lax.rem(g, n_ring)
        return pltpu.make_async_copy(ck_ref.at[0, phys], kbuf.at[slot], ksem.at[slot])

    head_i = lax.broadcasted_iota(jnp.int32, (n_heads, page), 0)

    @pl.when(b == 0)
    def _():
        for g in range(n_ring - 1):
            k_copy(jnp.int32(g)).start()
        for bk in range(REL_BUCKETS):
            tab = jnp.zeros((n_heads, page), F32)
            for h in range(n_heads):
                tab = jnp.where(head_i == h, relb_ref[bk, h], tab)
            tb_ref[bk] = tab

    q = q_ref[0]
    for h in range(n_heads):
        col = _col_from_row(q[:, h * head_dim:(h + 1) * head_dim], head_dim)
        qb_ref[h] = jnp.broadcast_to(col, (head_dim, page))

    def stream(pg, carry):
        g = b * n_pages + pg
        nxt = g + (n_ring - 1)

        @pl.when(nxt < total)
        def _():
            k_copy(nxt).start()

        k_copy(g).wait()
        kt = kbuf[lax.rem(g, n_ring)]
        lg_ref[pg] = jnp.sum(kt * qb_ref[...], axis=1)
        return carry

    lax.fori_loop(0, n_pages, stream, 0)

    lane = lax.broadcasted_iota(jnp.int32, (n_heads, LANES), 1)
    sc = jnp.full((n_heads, LANES), NEG_INF, F32)
    for jb in range(n_blocks):
        tot = lg_ref[jb * ppb]
        for i in range(1, ppb):
            tot = tot + lg_ref[jb * ppb + i]
        sc = jnp.where(lane == jb, jnp.sum(tot, axis=1, keepdims=True) * (1.0 / MOBA_BLOCK), sc)
    sel = jnp.full((n_heads, LANES), -1, jnp.int32)
    for r in range(n_sel):
        mx = jnp.max(sc, axis=1, keepdims=True)
        idx = jnp.min(jnp.where(sc == mx, lane, LANES), axis=1, keepdims=True)
        ok = mx > NEG_INF
        sel = jnp.where(lane == r, jnp.where(ok, idx, -1), sel)
        sc = jnp.where((lane == idx) & ok, NEG_INF, sc)
    selv_ref[...] = sel
    to_smem = pltpu.make_async_copy(selv_ref, sels_ref, selsem)
    to_smem.start()
    to_smem.wait()

    def chosen(h, r, i):
        j = sels_ref[h, r]
        pgl = jnp.maximum(j, 0) * ppb + i
        return j, pgl, (h * n_sel + r) * ppb + i

    def v_copy(h, r, i):
        _, pgl, slot = chosen(h, r, i)
        return pltpu.make_async_copy(cv_ref.at[0, pt_ref[b, pgl], h], vbuf.at[slot], vsem.at[slot])

    for h in range(n_heads):
        for r in range(n_sel):
            for i in range(ppb):
                @pl.when(chosen(h, r, i)[0] >= 0)
                def _():
                    v_copy(h, r, i).start()

    pos = lax.broadcasted_iota(jnp.int32, (n_heads, page), 1)
    qk_new = q * kn_ref[0]
    s_self = jnp.zeros((n_heads, 1), F32)
    for h in range(n_heads):
        dot_h = jnp.sum(qk_new[:, h * head_dim:(h + 1) * head_dim], axis=1, keepdims=True)
        s_self = jnp.where(head_i[:, 0:1] == h, dot_h, s_self)
    s_self = s_self * scale + tb_ref[0][:, 0:1]
    tiles = []
    for r in range(n_sel):
        blk_r = jnp.sum(jnp.where(lane == r, sel, 0), axis=1, keepdims=True)
        for i in range(ppb):
            raw = jnp.zeros((n_heads, page), F32)
            for h in range(n_heads):
                raw = jnp.where(head_i == h, lg_ref[chosen(h, r, i)[1]][h:h + 1, :], raw)
            bucket = _rel_bucket(past_len - (jnp.maximum(blk_r, 0) * ppb + i) * page - pos)
            bias = jnp.zeros((n_heads, page), F32)
            for bk in range(REL_BUCKETS):
                bias = jnp.where(bucket == bk, tb_ref[bk], bias)
            tiles.append(jnp.where(blk_r >= 0, raw * scale + bias, NEG_INF))
    m = s_self
    for s in tiles:
        m = jnp.maximum(m, jnp.max(s, axis=1, keepdims=True))
    p_self = jnp.exp(s_self - m)
    den = p_self
    probs = []
    for s in tiles:
        p = jnp.exp(s - m)
        den = den + jnp.sum(p, axis=1, keepdims=True)
        probs.append(p)
    rows = []
    for h in range(n_heads):
        acc = jnp.zeros((head_dim, page), F32)
        for n, p in enumerate(probs):
            j, _, slot = chosen(h, n // ppb, n % ppb)

            @pl.when(j >= 0)
            def _():
                v_copy(h, n // ppb, n % ppb).wait()

            acc = acc + jnp.where(j >= 0, vbuf[slot], 0.0) * p[h:h + 1, :]
        o_row = _row_from_col(jnp.sum(acc, axis=1, keepdims=True), head_dim)
        v_self = vn_ref[0][:, h * head_dim:(h + 1) * head_dim]
        rows.append((o_row + p_self[h:h + 1, :] * v_self) / den[h:h + 1, :])
    o_ref[0] = jnp.concatenate(rows, axis=1)


def _sample_attention(page_table, rel_bias, q, k_new, v_new, cache_kt, cache_vt, *, ppb, n_ring=32):
    bd, w = q.shape
    _, n_pool, n_heads, head_dim, page = cache_kt.shape
    n_pages = page_table.shape[1]
    assert n_pages % ppb == 0 and n_pages // ppb <= LANES and n_heads * head_dim == w
    n_slots = n_heads * MOBA_TOPK * ppb
    n_ring = min(n_ring, bd * n_pages)
    kern = functools.partial(_sample_attn_kernel, n_heads=n_heads, head_dim=head_dim, ppb=ppb, page=page,
                             n_pages=n_pages, n_ring=n_ring, scale=head_dim ** -0.5)
    row = pl.BlockSpec((1, 1, w), lambda b, pt: (b, 0, 0))
    grid_spec = pltpu.PrefetchScalarGridSpec(
        num_scalar_prefetch=1,
        grid=(bd,),
        in_specs=[pl.BlockSpec(memory_space=pltpu.SMEM), row, row, row,
                  pl.BlockSpec(memory_space=pl.ANY), pl.BlockSpec(memory_space=pl.ANY)],
        out_specs=row,
        scratch_shapes=[
            pltpu.VMEM((n_ring, n_heads, head_dim, page), F32),
            pltpu.SemaphoreType.DMA((n_ring,)),
            pltpu.VMEM((n_heads, head_dim, page), F32),
            pltpu.VMEM((n_pages, n_heads, page), F32),
            pltpu.VMEM((REL_BUCKETS, n_heads, page), F32),
            pltpu.VMEM((n_heads, LANES), jnp.int32),
            pltpu.SMEM((n_heads, LANES), jnp.int32),
            pltpu.SemaphoreType.DMA(()),
            pltpu.VMEM((n_slots, head_dim, page), F32),
            pltpu.SemaphoreType.DMA((n_slots,)),
        ],
    )
    out = pl.pallas_call(
        kern,
        grid_spec=grid_spec,
        out_shape=jax.ShapeDtypeStruct((bd, 1, w), F32),
        compiler_params=_cparams(1),
        name="sample_attn",
    )(page_table, rel_bias, q.reshape(bd, 1, w), k_new.reshape(bd, 1, w), v_new.reshape(bd, 1, w),
      cache_kt, cache_vt)
    return out.reshape(bd, w)


def _layer(x_prompt, x_sample, cache_k, cache_v, state_re, state_im, page_table, rel_bias, norm_g, w_in,
           q_norm_g, k_norm_g, lam_re, lam_im, log_dt, b_re, b_im, c_re, c_im, d_skip, w_glu, b_glu,
           w_branch_a, w_branch_b, w_out):
    bsz, seq, dm = x_prompt.shape
    bd, s_new, _ = x_sample.shape
    _, n_pool, page, n_heads, head_dim = cache_k.shape
    n_groups, n_state, gch = b_re.shape
    aw = n_heads * head_dim
    sw = n_groups * gch
    ns = n_groups * n_state
    n_pages = page_table.shape[1]
    past_len = n_pages * page
    ppb = MOBA_BLOCK // page
    assert s_new == 1 and seq % MOBA_BLOCK == 0 and past_len % MOBA_BLOCK == 0
    assert w_in.shape == (dm, 4 * aw + 2 * sw + 2 * dm)

    w_in_bf = w_in.astype(BF16)
    g_row = norm_g.reshape(1, dm)
    qg_t = jnp.tile(q_norm_g.reshape(1, head_dim), (1, n_heads))
    kg_t = jnp.tile(k_norm_g.reshape(1, head_dim), (1, n_heads))
    head_of = jnp.arange(aw) // head_dim
    bd_ones = (head_of[:, None] == head_of[None, :]).astype(BF16)
    halves = 2
    bt_re = jnp.tile(b_re.transpose(0, 2, 1).reshape(sw, n_state), (1, n_groups // halves))
    bt_im = jnp.tile(b_im.transpose(0, 2, 1).reshape(sw, n_state), (1, n_groups // halves))
    ct_re = jnp.tile(c_re.transpose(0, 2, 1).reshape(ns, gch), (1, n_groups // halves))
    ct_im = jnp.tile(c_im.transpose(0, 2, 1).reshape(ns, gch), (1, n_groups // halves))
    dsk = d_skip.reshape(1, sw)
    bgl = b_glu.reshape(1, 2 * sw)

    a_re, a_im, coef_re, coef_im = _ssm_prep(lam_re, lam_im, log_dt)
    a_re = a_re.reshape(1, ns)
    a_im = a_im.reshape(1, ns)
    coef_re = coef_re.reshape(halves, 1, ns // halves)
    coef_im = coef_im.reshape(halves, 1, ns // halves)

    wa_bf = w_branch_a.astype(BF16)
    wb_bf = w_branch_b.astype(BF16)
    wo_bf = w_out.astype(BF16)
    wglu_bf = w_glu.astype(BF16)

    xp2 = x_prompt.reshape(bsz * seq, dm)
    (q_p, kb_p, ksum_p, kt_p, vt_p, vta_p, sga_p, u_p, sgb_p, sma_p, smb_p) = _proj_in(
        xp2, g_row, w_in_bf, qg_t, kg_t, bd_ones, tm=MOBA_BLOCK, aw=aw, sw=sw, head_dim=head_dim,
        q_dtype=BF16, act_dtype=BF16, prompt_batch=bsz)
    n_blocks = seq // MOBA_BLOCK
    bias = _bias_tiles(rel_bias, n_heads, _n_bias_tiles(n_blocks))
    attn_p = _attention(q_p.reshape(bsz, seq, aw), kb_p.reshape(bsz, seq, aw), vta_p,
                        ksum_p.reshape(bsz, n_blocks, aw), bias, head_dim=head_dim)
    k_p = kt_p.reshape(bsz, n_heads, head_dim, seq).transpose(0, 3, 1, 2)
    v_p = vt_p.reshape(bsz, n_heads, head_dim, seq).transpose(0, 3, 1, 2)
    ssm_p, fre_p, fim_p = _s5_prompt(u_p.reshape(bsz, seq, sw), sgb_p.reshape(bsz, seq, sw), a_re, a_im, coef_re,
                                     coef_im, bt_re, bt_im, ct_re, ct_im, dsk, wglu_bf, bgl, tl=MOBA_BLOCK,
                                     gch=gch, nstate=n_state)
    y_p = _proj_out(xp2, attn_p.reshape(bsz * seq, aw), sga_p, ssm_p.reshape(bsz * seq, sw), sma_p, smb_p,
                    wa_bf, wb_bf, wo_bf, tm=MOBA_BLOCK)

    xs2 = x_sample.reshape(bd, dm)
    (q_s, k_s, v_s, sga_s, u_s, sgb_s, sma_s, smb_s) = _proj_in(
        xs2, g_row, w_in_bf, qg_t, kg_t, bd_ones, tm=bd, aw=aw, sw=sw, head_dim=head_dim,
        q_dtype=F32, act_dtype=F32)
    cache_kt = cache_k.transpose(0, 1, 3, 4, 2)
    cache_vt = cache_v.transpose(0, 1, 3, 4, 2)
    attn_s = _sample_attention(page_table, rel_bias, q_s, k_s, v_s, cache_kt, cache_vt, ppb=ppb)
    ssm_s, nre_s, nim_s = _s5_sample(u_s, sgb_s, state_re.reshape(bd, ns), state_im.reshape(bd, ns), a_re, a_im,
                                     coef_re, coef_im, bt_re, bt_im, ct_re, ct_im, dsk, w_glu, bgl, gch=gch,
                                     nstate=n_state)
    y_s = _proj_out(xs2, attn_s, sga_s, ssm_s, sma_s, smb_s, wa_bf, wb_bf, wo_bf, tm=bd)

    return (
        y_p.reshape(bsz, seq, dm), y_s.reshape(bd, 1, dm),
        k_p, v_p,
        fre_p.reshape(bsz, n_groups, n_state), fim_p.reshape(bsz, n_groups, n_state),
        k_s.reshape(bd, 1, n_heads, head_dim), v_s.reshape(bd, 1, n_heads, head_dim),
        nre_s.reshape(bd, n_groups, n_state), nim_s.reshape(bd, n_groups, n_state),
    )


def kernel(x_prompt, x_sample, cache_k, cache_v, state_ssm_re, state_ssm_im, page_table, rel_bias, norm_g, w_in,
           q_norm_g, k_norm_g, lam_re, lam_im, log_dt, b_re, b_im, c_re, c_im, d_skip, w_glu, b_glu, w_branch_a,
           w_branch_b, w_out):
    depth = norm_g.shape[0]
    assert depth == 1, "one layer per step"
    outs = _layer(x_prompt, x_sample, cache_k, cache_v, state_ssm_re[0], state_ssm_im[0], page_table,
                  rel_bias, norm_g[0], w_in[0], q_norm_g[0], k_norm_g[0], lam_re[0], lam_im[0], log_dt[0],
                  b_re[0], b_im[0], c_re[0], c_im[0], d_skip[0], w_glu[0], b_glu[0], w_branch_a[0],
                  w_branch_b[0], w_out[0])
    y_p, y_s, k_p, v_p, fre_p, fim_p, k_s, v_s, nre_s, nim_s = outs
    add = lambda a: a[None]
    return (y_p, y_s, add(k_p), add(v_p), add(fre_p), add(fim_p), add(k_s), add(v_s), add(nre_s), add(nim_s))
```

```python
import functools
import math

import jax
import jax.numpy as jnp
from jax import lax
from jax.experimental import pallas as pl
from jax.experimental.pallas import tpu as pltpu

F32 = jnp.float32
BF16 = jnp.bfloat16
HIGHEST = lax.Precision.HIGHEST

MOBA_BLOCK = 256
MOBA_TOPK = 3
REL_BUCKETS = 32
REL_MAX_DIST = 4096
RMS_EPS = 1e-6
LOG2E = math.log2(math.e)
ATTN_HEADS_PER_STEP = 2
LANES = 128
BF16_SUBLANES = 16
VMEM_LIMIT = 56 * 1024 * 1024

NEG_INF = float("-inf")


def _cparams(n_axes, vmem=VMEM_LIMIT):
    return pltpu.CompilerParams(dimension_semantics=("arbitrary",) * n_axes, vmem_limit_bytes=vmem)


def _const_spec(shape, n_grid):
    zeros = (0,) * len(shape)
    if n_grid == 1:
        imap = lambda i: zeros
    elif n_grid == 2:
        imap = lambda i, j: zeros
    else:
        imap = lambda i, j, k: zeros
    return pl.BlockSpec(shape, imap, pipeline_mode=pl.Buffered(1))


def _proj_in_kernel(x_ref, g_ref, w_ref, qg_ref, kg_ref, bd_ref, *out_refs, aw, sw, dm, head_dim, prompt):
    if prompt:
        (q_ref, kb_ref, ksum_ref, kt_ref, vt_ref, vta_ref, sga_ref, u_ref, sgb_ref, sma_ref, smb_ref) = out_refs
    else:
        (q_ref, k_ref, v_ref, sga_ref, u_ref, sgb_ref, sma_ref, smb_ref) = out_refs
    x = x_ref[...]
    ms = jnp.mean(x * x, axis=-1, keepdims=True)
    h = (x * lax.rsqrt(ms + RMS_EPS) * g_ref[...]).astype(BF16)

    def seg(a, b):
        return jnp.dot(h, w_ref[:, a:b], preferred_element_type=F32)

    def headnorm(z, gain):
        ss = jnp.dot((z * z).astype(BF16), bd_ref[...], preferred_element_type=F32)
        return z * lax.rsqrt(ss * (1.0 / head_dim) + RMS_EPS) * gain

    o = 0
    q = headnorm(seg(o, o + aw), qg_ref[...])
    q_ref[...] = q.astype(q_ref.dtype)
    o += aw
    k = headnorm(seg(o, o + aw), kg_ref[...])
    o += aw
    v = seg(o, o + aw)
    o += aw
    if prompt:
        kb_ref[...] = k.astype(BF16)
        ksum_ref[0] = jnp.sum(k, axis=0, keepdims=True)
        kt_ref[0] = k.T
        vt = v.T
        vt_ref[0] = vt
        ones = jnp.ones((BF16_SUBLANES, vt.shape[1]), BF16)
        rows = head_dim + BF16_SUBLANES
        for hd in range(aw // head_dim):
            vta_ref[0, 0, hd * rows:hd * rows + head_dim, :] = vt[hd * head_dim:(hd + 1) * head_dim, :].astype(BF16)
            vta_ref[0, 0, hd * rows + head_dim:(hd + 1) * rows, :] = ones
    else:
        k_ref[...] = k
        v_ref[...] = v
    sga_ref[...] = jax.nn.silu(seg(o, o + aw)).astype(sga_ref.dtype)
    o += aw
    u_ref[...] = seg(o, o + sw).astype(u_ref.dtype)
    o += sw
    sgb_ref[...] = jax.nn.silu(seg(o, o + sw)).astype(sgb_ref.dtype)
    o += sw
    sma_ref[...] = jax.nn.sigmoid(seg(o, o + dm)).astype(sma_ref.dtype)
    o += dm
    smb_ref[...] = jax.nn.sigmoid(seg(o, o + dm)).astype(smb_ref.dtype)


def _proj_in(x2, g, w_bf, qg_t, kg_t, bd, *, tm, aw, sw, head_dim, q_dtype, act_dtype, prompt_batch=None):
    t, dm = x2.shape
    n_tiles = t // tm
    row = lambda w: pl.BlockSpec((tm, w), lambda i: (i, 0))
    prompt = prompt_batch is not None
    if prompt:
        assert tm == MOBA_BLOCK and n_tiles % prompt_batch == 0
        per_seq = n_tiles // prompt_batch
        seq = per_seq * tm
        n_heads = aw // head_dim
        va_rows = n_heads * (head_dim + BF16_SUBLANES)
        t_spec = pl.BlockSpec((1, aw, tm), lambda i: (i // per_seq, 0, i % per_seq))
        out_shape = [
            jax.ShapeDtypeStruct((t, aw), q_dtype),
            jax.ShapeDtypeStruct((t, aw), BF16),
            jax.ShapeDtypeStruct((n_tiles, 1, aw), F32),
            jax.ShapeDtypeStruct((prompt_batch, aw, seq), F32),
            jax.ShapeDtypeStruct((prompt_batch, aw, seq), F32),
            jax.ShapeDtypeStruct((prompt_batch, per_seq, va_rows, tm), BF16),
        ]
        out_specs = [row(aw), row(aw), pl.BlockSpec((1, 1, aw), lambda i: (i, 0, 0)), t_spec, t_spec,
                     pl.BlockSpec((1, 1, va_rows, tm), lambda i: (i // per_seq, i % per_seq, 0, 0))]
    else:
        out_shape = [
            jax.ShapeDtypeStruct((t, aw), q_dtype),
            jax.ShapeDtypeStruct((t, aw), F32),
            jax.ShapeDtypeStruct((t, aw), F32),
        ]
        out_specs = [row(aw)] * 3
    out_shape += [
        jax.ShapeDtypeStruct((t, aw), act_dtype),
        jax.ShapeDtypeStruct((t, sw), act_dtype),
        jax.ShapeDtypeStruct((t, sw), act_dtype),
        jax.ShapeDtypeStruct((t, dm), act_dtype),
        jax.ShapeDtypeStruct((t, dm), act_dtype),
    ]
    out_specs += [row(aw), row(sw), row(sw), row(dm), row(dm)]
    kern = functools.partial(_proj_in_kernel, aw=aw, sw=sw, dm=dm, head_dim=head_dim, prompt=prompt)
    return pl.pallas_call(
        kern,
        grid=(n_tiles,),
        in_specs=[
            row(dm),
            _const_spec((1, dm), 1),
            _const_spec(w_bf.shape, 1),
            _const_spec((1, aw), 1),
            _const_spec((1, aw), 1),
            _const_spec((aw, aw), 1),
        ],
        out_specs=out_specs,
        out_shape=out_shape,
        compiler_params=_cparams(1),
        name="proj_in",
    )(x2, g, w_bf, qg_t, kg_t, bd)


def _rel_bucket(dist):
    n = jnp.maximum(dist, 0)
    max_exact = REL_BUCKETS // 2
    nf = jnp.maximum(n, 1).astype(F32)
    large = max_exact + (
        jnp.log(nf / max_exact) * ((REL_BUCKETS - max_exact) / math.log(REL_MAX_DIST / max_exact))
    ).astype(jnp.int32)
    return jnp.where(n < max_exact, n, jnp.minimum(large, REL_BUCKETS - 1))


def _bias_lookup(bucket, relb_ref, head):
    val = jnp.zeros(bucket.shape, F32)
    for b in range(REL_BUCKETS):
        val = jnp.where(bucket == b, relb_ref[b, head], val)
    return val


def _bias_tiles_kernel(relb_ref, o_ref):
    h = pl.program_id(0)
    d = pl.program_id(1)
    key = lax.broadcasted_iota(jnp.int32, (MOBA_BLOCK, MOBA_BLOCK), 0)
    qry = lax.broadcasted_iota(jnp.int32, (MOBA_BLOCK, MOBA_BLOCK), 1)
    dist = d * MOBA_BLOCK + qry - key
    bias = _bias_lookup(_rel_bucket(dist), relb_ref, h) * LOG2E
    o_ref[0, 0] = jnp.where(dist >= 0, bias, NEG_INF)


def _n_bias_tiles(n_blocks):
    saturated = -(-(REL_MAX_DIST + MOBA_BLOCK - 1) // MOBA_BLOCK) + 1
    return min(n_blocks, saturated), n_blocks >= saturated


def _bias_tiles(rel_bias, n_heads, n_dist):
    return pl.pallas_call(
        _bias_tiles_kernel,
        grid=(n_heads, n_dist),
        in_specs=[pl.BlockSpec(memory_space=pltpu.SMEM)],
        out_specs=pl.BlockSpec((1, 1, MOBA_BLOCK, MOBA_BLOCK), lambda h, d: (h, d, 0, 0)),
        out_shape=jax.ShapeDtypeStruct((n_heads, n_dist, MOBA_BLOCK, MOBA_BLOCK), F32),
        compiler_params=_cparams(2),
        name="bias_tiles",
    )(rel_bias)


def _select_topk(s):
    rowi = lax.broadcasted_iota(jnp.int32, s.shape, 0)
    sel = jnp.zeros(s.shape, jnp.bool_)
    for _ in range(MOBA_TOPK):
        mx = jnp.max(s, axis=0, keepdims=True)
        cand = jnp.where(s == mx, rowi, s.shape[0])
        idx = jnp.min(cand, axis=0, keepdims=True)
        hit = (rowi == idx) & (mx > NEG_INF)
        sel = sel | hit
        s = jnp.where(hit, NEG_INF, s)
    return sel


def _attn_kernel(q_ref, kb_ref, vta_ref, ksum_ref, bias_ref, o_ref,
                 *scratch, head_dim, n_dist, far_const, scale, nh):
    t = pl.program_id(2)
    blk = MOBA_BLOCK
    n_blocks = vta_ref.shape[1]
    pair = nh * head_dim
    v_rows = head_dim + BF16_SUBLANES
    qts, sels, ms, accs, s_bufs, c_bufs, o_bufs = (scratch[i * nh:(i + 1) * nh] for i in range(7))

    qt = q_ref[0].astype(F32).T
    feat = lax.broadcasted_iota(jnp.int32, (pair, blk), 0)
    means = ksum_ref[0] * (1.0 / blk)
    blk_i = lax.broadcasted_iota(jnp.int32, (n_blocks, blk), 0)
    for h in range(nh):
        in_head = (feat >= h * head_dim) & (feat < (h + 1) * head_dim)
        qt_h = jnp.where(in_head, qt, 0.0)
        qts[h][...] = (qt_h * (scale * LOG2E)).astype(BF16)
        sc = jnp.dot(means, qt_h, preferred_element_type=F32, precision=HIGHEST)
        sc = jnp.where(blk_i < t, sc, NEG_INF)
        sels[h][...] = (_select_topk(sc) | (blk_i == t)).astype(F32)
        ms[h][...] = jnp.full(ms[h].shape, NEG_INF, F32)
        accs[h][...] = jnp.zeros(accs[h].shape, F32)

    def block_of(n):
        return jnp.where(n == 0, t, jnp.where(n <= t, n - 1, 0))

    def stage_a(n, slot, far):
        j = block_of(n)
        kj = kb_ref[0, pl.ds(pl.multiple_of(j * blk, blk), blk), :]
        tile = jnp.minimum(t - j, n_dist - 1)
        for h in range(nh):
            s = jnp.dot(kj, qts[h][...], preferred_element_type=F32)
            if far:
                off = bias_ref[h, n_dist - 1, 0:1, :]
            else:
                s = s + bias_ref[h, tile]
                off = jnp.zeros((1, blk), F32)
            s_bufs[h][slot] = s
            c_bufs[h][slot] = jnp.max(s, axis=0, keepdims=True) + off
            o_bufs[h][slot] = off

    def stage_b(n, slot):
        j = block_of(n)
        for h in range(nh):
            picked = jnp.where(n <= t, sels[h][pl.ds(j, 1), :], 0.0)
            m_old = ms[h][...]
            m_blk = jnp.maximum(m_old, c_bufs[h][slot])
            m_new = jnp.where(picked > 0.0, m_blk, m_old)
            p = jnp.exp2(s_bufs[h][slot] - (m_blk - o_bufs[h][slot])).astype(BF16)
            upd = jnp.dot(vta_ref[0, j, h * v_rows:(h + 1) * v_rows, :], p, preferred_element_type=F32)
            accs[h][...] = accs[h][...] * jnp.exp2(m_old - m_new) + upd * picked
            ms[h][...] = m_new

    stage_a(jnp.int32(0), 0, False)
    stage_a(jnp.int32(1), 1, False)

    n_iter = (t + 2) // 2

    def two_blocks(i, carry):
        j = 2 * i
        both_far = (t - (j + 2) >= n_dist - 1) if far_const else False
        for parity in range(2):
            cur, nxt = 2 * parity, 2 - 2 * parity
            mine = (i % 2 == parity)
            for far in ((False, True) if far_const else (False,)):
                @pl.when(mine & (i < n_iter - 1) & (both_far == far))
                def _():
                    stage_a(j + 2, nxt, far)
                    stage_a(j + 3, nxt + 1, far)
                    stage_b(j, cur)
                    stage_b(j + 1, cur + 1)

            @pl.when(mine & (i == n_iter - 1))
            def _():
                stage_b(j, cur)
                stage_b(j + 1, cur + 1)

        return carry

    lax.fori_loop(0, n_iter, two_blocks, 0)

    outs = [accs[h][0:head_dim, :] / accs[h][head_dim:head_dim + 1, :] for h in range(nh)]
    o_ref[0] = jnp.concatenate(outs, axis=0).T.astype(o_ref.dtype)


def _attention(q_bf, k_bf, vt_aug, ksum, bias, *, head_dim, nh):
    b, l, aw = q_bf.shape
    width = nh * head_dim
    assert width % LANES == 0 and aw % width == 0 and l % MOBA_BLOCK == 0
    n_groups = aw // width
    n_blocks = l // MOBA_BLOCK
    n_dist, far_const = _n_bias_tiles(n_blocks)
    assert bias.shape[1] == n_dist
    v_rows = head_dim + BF16_SUBLANES
    assert vt_aug.shape == (b, n_blocks, nh * n_groups * v_rows, MOBA_BLOCK)
    kern = functools.partial(_attn_kernel, head_dim=head_dim, n_dist=n_dist, far_const=far_const,
                             scale=head_dim ** -0.5, nh=nh)
    one = pl.Buffered(1)
    return pl.pallas_call(
        kern,
        grid=(b, n_groups, n_blocks),
        in_specs=[
            pl.BlockSpec((1, MOBA_BLOCK, width), lambda bi, hg, t: (bi, t, hg)),
            pl.BlockSpec((1, l, width), lambda bi, hg, t: (bi, 0, hg), pipeline_mode=one),
            pl.BlockSpec((1, n_blocks, nh * v_rows, MOBA_BLOCK), lambda bi, hg, t: (bi, 0, hg, 0), pipeline_mode=one),
            pl.BlockSpec((1, n_blocks, width), lambda bi, hg, t: (bi, 0, hg), pipeline_mode=one),
            pl.BlockSpec((nh, n_dist, MOBA_BLOCK, MOBA_BLOCK), lambda bi, hg, t: (hg, 0, 0, 0), pipeline_mode=one),
        ],
        out_specs=pl.BlockSpec((1, MOBA_BLOCK, width), lambda bi, hg, t: (bi, t, hg)),
        out_shape=jax.ShapeDtypeStruct((b, l, aw), BF16),
        scratch_shapes=(
            [pltpu.VMEM((width, MOBA_BLOCK), BF16)] * nh
            + [pltpu.VMEM((n_blocks, MOBA_BLOCK), F32)] * nh
            + [pltpu.VMEM((1, MOBA_BLOCK), F32)] * nh
            + [pltpu.VMEM((v_rows, MOBA_BLOCK), F32)] * nh
            + [pltpu.VMEM((4, MOBA_BLOCK, MOBA_BLOCK), F32)] * nh
            + [pltpu.VMEM((4, 1, MOBA_BLOCK), F32)] * nh
            + [pltpu.VMEM((4, 1, MOBA_BLOCK), F32)] * nh
        ),
        compiler_params=_cparams(3),
        name="moba_attn",
    )(q_bf, k_bf, vt_aug, ksum, bias)


def _ssm_prep_kernel(lre_ref, lim_ref, ldt_ref, are_ref, aim_ref, cre_ref, cim_ref):
    lre = lre_ref[...]
    lim = lim_ref[...]
    dt = jnp.exp(ldt_ref[...])
    mag = jnp.exp(lre * dt)
    a_re = mag * jnp.cos(lim * dt)
    a_im = mag * jnp.sin(lim * dt)
    are_ref[...] = a_re
    aim_ref[...] = a_im
    nr = a_re - 1.0
    den = lre * lre + lim * lim
    cre_ref[...] = (nr * lre + a_im * lim) / den
    cim_ref[...] = (a_im * lre - nr * lim) / den


def _ssm_prep(lam_re, lam_im, log_dt):
    g, p = lam_re.shape
    shp = jax.ShapeDtypeStruct((g, p), F32)
    return pl.pallas_call(_ssm_prep_kernel, out_shape=[shp] * 4, name="ssm_prep")(
        lam_re, lam_im, log_dt.reshape(g, 1))


def _block_diag_mask(rows, cols, row_group, col_group):
    r = lax.broadcasted_iota(jnp.int32, (rows, cols), 0) // row_group
    c = lax.broadcasted_iota(jnp.int32, (rows, cols), 1) // col_group
    return r == c


def _build_ssm_weights(coef_re_ref, coef_im_ref, bt_re_ref, bt_im_ref, ct_re_ref, ct_im_ref,
                       wb_re_ref, wb_im_ref, wc_re_ref, wc_im_ref, *, gch, nstate):
    halves, kh, nh = wb_re_ref.shape
    mask_b = _block_diag_mask(kh, nh, gch, nstate)
    mask_c = _block_diag_mask(nh, kh, nstate, gch)
    for i in range(halves):
        cr = coef_re_ref[i]
        ci = coef_im_ref[i]
        br = bt_re_ref[i * kh:(i + 1) * kh, :]
        bi = bt_im_ref[i * kh:(i + 1) * kh, :]
        wb_re_ref[i] = jnp.where(mask_b, cr * br - ci * bi, 0.0).astype(wb_re_ref.dtype)
        wb_im_ref[i] = jnp.where(mask_b, cr * bi + ci * br, 0.0).astype(wb_im_ref.dtype)
        wc_re_ref[i] = jnp.where(mask_c, ct_re_ref[i * nh:(i + 1) * nh, :], 0.0).astype(wc_re_ref.dtype)
        wc_im_ref[i] = jnp.where(mask_c, -ct_im_ref[i * nh:(i + 1) * nh, :], 0.0).astype(wc_im_ref.dtype)


def _ssm_in(u, wb_re_ref, wb_im_ref, precision=None):
    halves, kh, _ = wb_re_ref.shape
    re, im = [], []
    for i in range(halves):
        ui = u[:, i * kh:(i + 1) * kh]
        re.append(jnp.dot(ui, wb_re_ref[i], preferred_element_type=F32, precision=precision))
        im.append(jnp.dot(ui, wb_im_ref[i], preferred_element_type=F32, precision=precision))
    return jnp.concatenate(re, axis=1), jnp.concatenate(im, axis=1)


def _ssm_out(x_re, x_im, wc_re_ref, wc_im_ref, precision=None):
    halves, nh, _ = wc_re_ref.shape
    ys = []
    for i in range(halves):
        ys.append(jnp.dot(x_re[:, i * nh:(i + 1) * nh], wc_re_ref[i], preferred_element_type=F32, precision=precision)
                  + jnp.dot(x_im[:, i * nh:(i + 1) * nh], wc_im_ref[i], preferred_element_type=F32,
                            precision=precision))
    return jnp.concatenate(ys, axis=1)


def _glu_tail(y, u32, dskip_ref, wglu_ref, bglu_ref, sgb, sw, precision=None):
    y = y + dskip_ref[...] * u32
    za = jax.nn.gelu(y).astype(wglu_ref.dtype)
    glu = jnp.dot(za, wglu_ref[...], preferred_element_type=F32, precision=precision) + bglu_ref[...]
    return glu[:, :sw] * jax.nn.sigmoid(glu[:, sw:]) * sgb


def _s5_prompt_kernel(u_ref, sgb_ref, are_ref, aim_ref, coef_re_ref, coef_im_ref, bt_re_ref, bt_im_ref,
                      ct_re_ref, ct_im_ref, dskip_ref, wglu_ref, bglu_ref,
                      out_ref, fre_ref, fim_ref,
                      wb_re_ref, wb_im_ref, wc_re_ref, wc_im_ref, bu_re_ref, bu_im_ref, xs_re_ref, xs_im_ref,
                      st_re_ref, st_im_ref, *, gch, nstate, sw):
    b = pl.program_id(0)
    t = pl.program_id(1)
    tl = u_ref.shape[1]

    @pl.when((b == 0) & (t == 0))
    def _():
        _build_ssm_weights(coef_re_ref, coef_im_ref, bt_re_ref, bt_im_ref, ct_re_ref, ct_im_ref,
                           wb_re_ref, wb_im_ref, wc_re_ref, wc_im_ref, gch=gch, nstate=nstate)

    @pl.when(t == 0)
    def _():
        st_re_ref[...] = jnp.zeros_like(st_re_ref)
        st_im_ref[...] = jnp.zeros_like(st_im_ref)

    u = u_ref[0]
    bu_re, bu_im = _ssm_in(u, wb_re_ref, wb_im_ref)
    bu_re_ref[...] = bu_re
    bu_im_ref[...] = bu_im
    a_re = are_ref[...]
    a_im = aim_ref[...]

    def step(i, carry):
        xr, xi = carry
        nr = a_re * xr - a_im * xi + bu_re_ref[pl.ds(i, 1), :]
        ni = a_re * xi + a_im * xr + bu_im_ref[pl.ds(i, 1), :]
        xs_re_ref[pl.ds(i, 1), :] = nr
        xs_im_ref[pl.ds(i, 1), :] = ni
        return nr, ni

    xr, xi = lax.fori_loop(0, tl, step, (st_re_ref[...], st_im_ref[...]), unroll=8)
    st_re_ref[...] = xr
    st_im_ref[...] = xi
    fre_ref[0] = xr
    fim_ref[0] = xi

    y = _ssm_out(xs_re_ref[...].astype(BF16), xs_im_ref[...].astype(BF16), wc_re_ref, wc_im_ref)
    out = _glu_tail(y, u.astype(F32), dskip_ref, wglu_ref, bglu_ref, sgb_ref[0].astype(F32), sw)
    out_ref[0] = out.astype(out_ref.dtype)


def _s5_prompt(u, sgb, a_re, a_im, coef_re, coef_im, bt_re, bt_im, ct_re, ct_im, d_skip, w_glu, b_glu, *, tl,
               gch, nstate):
    b, l, sw = u.shape
    ns = a_re.shape[1]
    halves = 2
    kh = sw // halves
    nh = ns // halves
    kern = functools.partial(_s5_prompt_kernel, gch=gch, nstate=nstate, sw=sw)
    tile = pl.BlockSpec((1, tl, sw), lambda bi, t: (bi, t, 0))
    c2 = lambda shape: _const_spec(shape, 2)
    return pl.pallas_call(
        kern,
        grid=(b, l // tl),
        in_specs=[tile, tile, c2((1, ns)), c2((1, ns)), c2((halves, 1, nh)), c2((halves, 1, nh)),
                  c2(bt_re.shape), c2(bt_im.shape), c2(ct_re.shape), c2(ct_im.shape),
                  c2((1, sw)), c2(w_glu.shape), c2((1, 2 * sw))],
        out_specs=[tile,
                   pl.BlockSpec((1, 1, ns), lambda bi, t: (bi, 0, 0)),
                   pl.BlockSpec((1, 1, ns), lambda bi, t: (bi, 0, 0))],
        out_shape=[jax.ShapeDtypeStruct((b, l, sw), BF16),
                   jax.ShapeDtypeStruct((b, 1, ns), F32),
                   jax.ShapeDtypeStruct((b, 1, ns), F32)],
        scratch_shapes=[
            pltpu.VMEM((halves, kh, nh), BF16), pltpu.VMEM((halves, kh, nh), BF16),
            pltpu.VMEM((halves, nh, kh), BF16), pltpu.VMEM((halves, nh, kh), BF16),
            pltpu.VMEM((tl, ns), F32), pltpu.VMEM((tl, ns), F32),
            pltpu.VMEM((tl, ns), F32), pltpu.VMEM((tl, ns), F32),
            pltpu.VMEM((1, ns), F32), pltpu.VMEM((1, ns), F32),
        ],
        compiler_params=_cparams(2),
        name="s5_prompt",
    )(u, sgb, a_re, a_im, coef_re, coef_im, bt_re, bt_im, ct_re, ct_im, d_skip, w_glu, b_glu)


def _s5_sample_kernel(u_ref, sgb_ref, sre_ref, sim_ref, are_ref, aim_ref, coef_re_ref, coef_im_ref,
                      bt_re_ref, bt_im_ref, ct_re_ref, ct_im_ref, dskip_ref, wglu_ref, bglu_ref,
                      out_ref, nre_ref, nim_ref, wb_re_ref, wb_im_ref, wc_re_ref, wc_im_ref, *, gch, nstate, sw):
    _build_ssm_weights(coef_re_ref, coef_im_ref, bt_re_ref, bt_im_ref, ct_re_ref, ct_im_ref,
                       wb_re_ref, wb_im_ref, wc_re_ref, wc_im_ref, gch=gch, nstate=nstate)
    u = u_ref[...]
    bu_re, bu_im = _ssm_in(u, wb_re_ref, wb_im_ref, precision=HIGHEST)
    a_re = are_ref[...]
    a_im = aim_ref[...]
    s_re = sre_ref[...]
    s_im = sim_ref[...]
    x_re = a_re * s_re - a_im * s_im + bu_re
    x_im = a_re * s_im + a_im * s_re + bu_im
    nre_ref[...] = x_re
    nim_ref[...] = x_im
    y = _ssm_out(x_re, x_im, wc_re_ref, wc_im_ref, precision=HIGHEST)
    out_ref[...] = _glu_tail(y, u, dskip_ref, wglu_ref, bglu_ref, sgb_ref[...], sw, precision=HIGHEST)


def _s5_sample(u, sgb, s_re, s_im, a_re, a_im, coef_re, coef_im, bt_re, bt_im, ct_re, ct_im, d_skip, w_glu,
               b_glu, *, gch, nstate):
    n, sw = u.shape
    ns = a_re.shape[1]
    halves = 2
    kh = sw // halves
    nh = ns // halves
    kern = functools.partial(_s5_sample_kernel, gch=gch, nstate=nstate, sw=sw)
    return pl.pallas_call(
        kern,
        out_shape=[jax.ShapeDtypeStruct((n, sw), F32),
                   jax.ShapeDtypeStruct((n, ns), F32),
                   jax.ShapeDtypeStruct((n, ns), F32)],
        scratch_shapes=[
            pltpu.VMEM((halves, kh, nh), F32), pltpu.VMEM((halves, kh, nh), F32),
            pltpu.VMEM((halves, nh, kh), F32), pltpu.VMEM((halves, nh, kh), F32),
        ],
        compiler_params=pltpu.CompilerParams(vmem_limit_bytes=VMEM_LIMIT),
        name="s5_sample",
    )(u, sgb, s_re, s_im, a_re, a_im, coef_re, coef_im, bt_re, bt_im, ct_re, ct_im, d_skip, w_glu, b_glu)


def _proj_out_kernel(x_ref, attn_ref, sga_ref, ssm_ref, sma_ref, smb_ref, wa_ref, wb_ref, wo_ref, o_ref):
    attn_out = (attn_ref[...].astype(F32) * sga_ref[...].astype(F32)).astype(BF16)
    ya = jnp.dot(attn_out, wa_ref[...], preferred_element_type=F32)
    yb = jnp.dot(ssm_ref[...].astype(BF16), wb_ref[...], preferred_element_type=F32)
    merged = sma_ref[...].astype(F32) * ya + smb_ref[...].astype(F32) * yb
    o_ref[...] = x_ref[...] + jnp.dot(merged.astype(BF16), wo_ref[...], preferred_element_type=F32)


def _proj_out(x2, attn, sga, ssm, sma, smb, wa, wb, wo, *, tm):
    t, dm = x2.shape
    aw = attn.shape[1]
    sw = ssm.shape[1]
    row = lambda w: pl.BlockSpec((tm, w), lambda i: (i, 0))
    return pl.pallas_call(
        _proj_out_kernel,
        grid=(t // tm,),
        in_specs=[row(dm), row(aw), row(aw), row(sw), row(dm), row(dm),
                  _const_spec(wa.shape, 1), _const_spec(wb.shape, 1), _const_spec(wo.shape, 1)],
        out_specs=row(dm),
        out_shape=jax.ShapeDtypeStruct((t, dm), F32),
        compiler_params=_cparams(1),
        name="proj_out",
    )(x2, attn, sga, ssm, sma, smb, wa, wb, wo)


def _col_from_row(row, n):
    eye = lax.broadcasted_iota(jnp.int32, (n, n), 0) == lax.broadcasted_iota(jnp.int32, (n, n), 1)
    return jnp.sum(jnp.where(eye, jnp.broadcast_to(row, (n, n)), 0.0), axis=1, keepdims=True)


def _row_from_col(col, n):
    eye = lax.broadcasted_iota(jnp.int32, (n, n), 0) == lax.broadcasted_iota(jnp.int32, (n, n), 1)
    return jnp.sum(jnp.where(eye, jnp.broadcast_to(col, (n, n)), 0.0), axis=0, keepdims=True)


def _sample_attn_kernel(pt_ref, relb_ref, q_ref, kn_ref, vn_ref, ck_ref, cv_ref, o_ref,
                        kbuf, ksem, qb_ref, lg_ref, tb_ref, selv_ref, sels_ref, selsem, vbuf, vsem,
                        *, n_heads, head_dim, ppb, page, n_pages, n_ring, group, scale):
    b = pl.program_id(0)
    total = pl.num_programs(0) * n_pages
    n_blocks = n_pages // ppb
    past_len = n_pages * page
    n_sel = MOBA_TOPK

    def k_copy(g):
        phys = pt_ref[lax.div(g, n_pages), lax.rem(g, n_pages)]
        slot = lax.rem(g, n_ring)
        return pltpu.make_async_copy(ck_ref.at[0, phys], kbuf.at[slot], ksem.at[slot])

    head_i = lax.broadcasted_iota(jnp.int32, (n_heads, page), 0)

    @pl.when(b == 0)
    def _():
        for g in range(n_ring - group):
            k_copy(jnp.int32(g)).start()
        for bk in range(REL_BUCKETS):
            tab = jnp.zeros((n_heads, page), F32)
            for h in range(n_heads):
                tab = jnp.where(head_i == h, relb_ref[bk, h], tab)
            tb_ref[bk] = tab

    q = q_ref[0]
    for h in range(n_heads):
        col = _col_from_row(q[:, h * head_dim:(h + 1) * head_dim], head_dim)
        qb_ref[h] = jnp.broadcast_to(col, (head_dim, page))

    def stream(i, carry):
        g0 = b * n_pages + i * group
        for u in range(group):
            nxt = g0 + u + (n_ring - group)

            @pl.when(nxt < total)
            def _():
                k_copy(nxt).start()

        for u in range(group):
            k_copy(g0 + u).wait()
        slot0 = lax.rem(g0, n_ring)
        rows = [[] for _ in range(group)]
        for h in range(n_heads):
            qh = qb_ref[h]
            for u in range(group):
                rows[u].append(jnp.sum(kbuf[slot0 + u, h] * qh, axis=0, keepdims=True))
        for u in range(group):
            lg_ref[i * group + u] = jnp.concatenate(rows[u], axis=0)
        return carry

    lax.fori_loop(0, n_pages // group, stream, 0)

    lane = lax.broadcasted_iota(jnp.int32, (n_heads, LANES), 1)
    sc = jnp.full((n_heads, LANES), NEG_INF, F32)
    for jb in range(n_blocks):
        tot = lg_ref[jb * ppb]
        for i in range(1, ppb):
            tot = tot + lg_ref[jb * ppb + i]
        sc = jnp.where(lane == jb, jnp.sum(tot, axis=1, keepdims=True) * (1.0 / MOBA_BLOCK), sc)
    sel = jnp.full((n_heads, LANES), -1, jnp.int32)
    for r in range(n_sel):
        mx = jnp.max(sc, axis=1, keepdims=True)
        idx = jnp.min(jnp.where(sc == mx, lane, LANES), axis=1, keepdims=True)
        ok = mx > NEG_INF
        sel = jnp.where(lane == r, jnp.where(ok, idx, -1), sel)
        sc = jnp.where((lane == idx) & ok, NEG_INF, sc)
    selv_ref[...] = sel
    to_smem = pltpu.make_async_copy(selv_ref, sels_ref, selsem)
    to_smem.start()
    to_smem.wait()

    def chosen(h, r, i):
        j = sels_ref[h, r]
        pgl = jnp.maximum(j, 0) * ppb + i
        return j, pgl, (h * n_sel + r) * ppb + i

    def v_copy(h, r, i):
        _, pgl, slot = chosen(h, r, i)
        return pltpu.make_async_copy(cv_ref.at[0, pt_ref[b, pgl], h], vbuf.at[slot], vsem.at[slot])

    for h in range(n_heads):
        for r in range(n_sel):
            for i in range(ppb):
                @pl.when(chosen(h, r, i)[0] >= 0)
                def _():
                    v_copy(h, r, i).start()

    pos = lax.broadcasted_iota(jnp.int32, (n_heads, page), 1)
    qk_new = q * kn_ref[0]
    s_self = jnp.zeros((n_heads, 1), F32)
    for h in range(n_heads):
        dot_h = jnp.sum(qk_new[:, h * head_dim:(h + 1) * head_dim], axis=1, keepdims=True)
        s_self = jnp.where(head_i[:, 0:1] == h, dot_h, s_self)
    s_self = s_self * scale + tb_ref[0][:, 0:1]
    tiles = []
    for r in range(n_sel):
        blk_r = jnp.sum(jnp.where(lane == r, sel, 0), axis=1, keepdims=True)
        for i in range(ppb):
            raw = jnp.zeros((n_heads, page), F32)
            for h in range(n_heads):
                raw = jnp.where(head_i == h, lg_ref[chosen(h, r, i)[1]][h:h + 1, :], raw)
            bucket = _rel_bucket(past_len - (jnp.maximum(blk_r, 0) * ppb + i) * page - pos)
            bias = jnp.zeros((n_heads, page), F32)
            for bk in range(REL_BUCKETS):
                bias = jnp.where(bucket == bk, tb_ref[bk], bias)
            tiles.append(jnp.where(blk_r >= 0, raw * scale + bias, NEG_INF))
    m = s_self
    for s in tiles:
        m = jnp.maximum(m, jnp.max(s, axis=1, keepdims=True))
    p_self = jnp.exp(s_self - m)
    den = p_self
    probs = []
    for s in tiles:
        p = jnp.exp(s - m)
        den = den + jnp.sum(p, axis=1, keepdims=True)
        probs.append(p)
    rows = []
    for h in range(n_heads):
        acc = jnp.zeros((head_dim, page), F32)
        for n, p in enumerate(probs):
            j, _, slot = chosen(h, n // ppb, n % ppb)

            @pl.when(j >= 0)
            def _():
                v_copy(h, n // ppb, n % ppb).wait()

            acc = acc + jnp.where(j >= 0, vbuf[slot], 0.0) * p[h:h + 1, :]
        o_row = _row_from_col(jnp.sum(acc, axis=1, keepdims=True), head_dim)
        v_self = vn_ref[0][:, h * head_dim:(h + 1) * head_dim]
        rows.append((o_row + p_self[h:h + 1, :] * v_self) / den[h:h + 1, :])
    o_ref[0] = jnp.concatenate(rows, axis=1)


def _sample_attention(page_table, rel_bias, q, k_new, v_new, cache_kt, cache_vt, *, ppb, n_ring=32):
    bd, w = q.shape
    _, n_pool, n_heads, head_dim, page = cache_kt.shape
    n_pages = page_table.shape[1]
    assert n_pages % ppb == 0 and n_pages // ppb <= LANES and n_heads * head_dim == w
    n_slots = n_heads * MOBA_TOPK * ppb
    group = math.gcd(n_pages, 4)
    n_ring = min(n_ring, bd * n_pages)
    assert n_ring % group == 0 and n_ring > group
    kern = functools.partial(_sample_attn_kernel, n_heads=n_heads, head_dim=head_dim, ppb=ppb, page=page,
                             n_pages=n_pages, n_ring=n_ring, group=group, scale=head_dim ** -0.5)
    row = pl.BlockSpec((1, 1, w), lambda b, pt: (b, 0, 0))
    grid_spec = pltpu.PrefetchScalarGridSpec(
        num_scalar_prefetch=1,
        grid=(bd,),
        in_specs=[pl.BlockSpec(memory_space=pltpu.SMEM), row, row, row,
                  pl.BlockSpec(memory_space=pl.ANY), pl.BlockSpec(memory_space=pl.ANY)],
        out_specs=row,
        scratch_shapes=[
            pltpu.VMEM((n_ring, n_heads, head_dim, page), F32),
            pltpu.SemaphoreType.DMA((n_ring,)),
            pltpu.VMEM((n_heads, head_dim, page), F32),
            pltpu.VMEM((n_pages, n_heads, page), F32),
            pltpu.VMEM((REL_BUCKETS, n_heads, page), F32),
            pltpu.VMEM((n_heads, LANES), jnp.int32),
            pltpu.SMEM((n_heads, LANES), jnp.int32),
            pltpu.SemaphoreType.DMA(()),
            pltpu.VMEM((n_slots, head_dim, page), F32),
            pltpu.SemaphoreType.DMA((n_slots,)),
        ],
    )
    out = pl.pallas_call(
        kern,
        grid_spec=grid_spec,
        out_shape=jax.ShapeDtypeStruct((bd, 1, w), F32),
        compiler_params=_cparams(1),
        name="sample_attn",
    )(page_table, rel_bias, q.reshape(bd, 1, w), k_new.reshape(bd, 1, w), v_new.reshape(bd, 1, w),
      cache_kt, cache_vt)
    return out.reshape(bd, w)


def _layer(x_prompt, x_sample, cache_k, cache_v, state_re, state_im, page_table, rel_bias, norm_g, w_in,
           q_norm_g, k_norm_g, lam_re, lam_im, log_dt, b_re, b_im, c_re, c_im, d_skip, w_glu, b_glu,
           w_branch_a, w_branch_b, w_out):
    bsz, seq, dm = x_prompt.shape
    bd, s_new, _ = x_sample.shape
    _, n_pool, page, n_heads, head_dim = cache_k.shape
    n_groups, n_state, gch = b_re.shape
    aw = n_heads * head_dim
    sw = n_groups * gch
    ns = n_groups * n_state
    n_pages = page_table.shape[1]
    past_len = n_pages * page
    ppb = MOBA_BLOCK // page
    assert s_new == 1 and seq % MOBA_BLOCK == 0 and past_len % MOBA_BLOCK == 0
    assert w_in.shape == (dm, 4 * aw + 2 * sw + 2 * dm)

    w_in_bf = w_in.astype(BF16)
    g_row = norm_g.reshape(1, dm)
    qg_t = jnp.tile(q_norm_g.reshape(1, head_dim), (1, n_heads))
    kg_t = jnp.tile(k_norm_g.reshape(1, head_dim), (1, n_heads))
    head_of = jnp.arange(aw) // head_dim
    bd_ones = (head_of[:, None] == head_of[None, :]).astype(BF16)
    halves = 2
    bt_re = jnp.tile(b_re.transpose(0, 2, 1).reshape(sw, n_state), (1, n_groups // halves))
    bt_im = jnp.tile(b_im.transpose(0, 2, 1).reshape(sw, n_state), (1, n_groups // halves))
    ct_re = jnp.tile(c_re.transpose(0, 2, 1).reshape(ns, gch), (1, n_groups // halves))
    ct_im = jnp.tile(c_im.transpose(0, 2, 1).reshape(ns, gch), (1, n_groups // halves))
    dsk = d_skip.reshape(1, sw)
    bgl = b_glu.reshape(1, 2 * sw)

    a_re, a_im, coef_re, coef_im = _ssm_prep(lam_re, lam_im, log_dt)
    a_re = a_re.reshape(1, ns)
    a_im = a_im.reshape(1, ns)
    coef_re = coef_re.reshape(halves, 1, ns // halves)
    coef_im = coef_im.reshape(halves, 1, ns // halves)

    wa_bf = w_branch_a.astype(BF16)
    wb_bf = w_branch_b.astype(BF16)
    wo_bf = w_out.astype(BF16)
    wglu_bf = w_glu.astype(BF16)

    xp2 = x_prompt.reshape(bsz * seq, dm)
    (q_p, kb_p, ksum_p, kt_p, vt_p, vta_p, sga_p, u_p, sgb_p, sma_p, smb_p) = _proj_in(
        xp2, g_row, w_in_bf, qg_t, kg_t, bd_ones, tm=MOBA_BLOCK, aw=aw, sw=sw, head_dim=head_dim,
        q_dtype=BF16, act_dtype=BF16, prompt_batch=bsz)
    n_blocks = seq // MOBA_BLOCK
    bias = _bias_tiles(rel_bias, n_heads, _n_bias_tiles(n_blocks)[0])
    attn_p = _attention(q_p.reshape(bsz, seq, aw), kb_p.reshape(bsz, seq, aw), vta_p,
                        ksum_p.reshape(bsz, n_blocks, aw), bias, head_dim=head_dim, nh=ATTN_HEADS_PER_STEP)
    k_p = kt_p.reshape(bsz, n_heads, head_dim, seq).transpose(0, 3, 1, 2)
    v_p = vt_p.reshape(bsz, n_heads, head_dim, seq).transpose(0, 3, 1, 2)
    ssm_p, fre_p, fim_p = _s5_prompt(u_p.reshape(bsz, seq, sw), sgb_p.reshape(bsz, seq, sw), a_re, a_im, coef_re,
                                     coef_im, bt_re, bt_im, ct_re, ct_im, dsk, wglu_bf, bgl, tl=MOBA_BLOCK,
                                     gch=gch, nstate=n_state)
    y_p = _proj_out(xp2, attn_p.reshape(bsz * seq, aw), sga_p, ssm_p.reshape(bsz * seq, sw), sma_p, smb_p,
                    wa_bf, wb_bf, wo_bf, tm=MOBA_BLOCK)

    xs2 = x_sample.reshape(bd, dm)
    (q_s, k_s, v_s, sga_s, u_s, sgb_s, sma_s, smb_s) = _proj_in(
        xs2, g_row, w_in_bf, qg_t, kg_t, bd_ones, tm=bd, aw=aw, sw=sw, head_dim=head_dim,
        q_dtype=F32, act_dtype=F32)
    cache_kt = cache_k.transpose(0, 1, 3, 4, 2)
    cache_vt = cache_v.transpose(0, 1, 3, 4, 2)
    attn_s = _sample_attention(page_table, rel_bias, q_s, k_s, v_s, cache_kt, cache_vt, ppb=ppb)
    ssm_s, nre_s, nim_s = _s5_sample(u_s, sgb_s, state_re.reshape(bd, ns), state_im.reshape(bd, ns), a_re, a_im,
                                     coef_re, coef_im, bt_re, bt_im, ct_re, ct_im, dsk, w_glu, bgl, gch=gch,
                                     nstate=n_state)
    y_s = _proj_out(xs2, attn_s, sga_s, ssm_s, sma_s, smb_s, wa_bf, wb_bf, wo_bf, tm=bd)

    return (
        y_p.reshape(bsz, seq, dm), y_s.reshape(bd, 1, dm),
        k_p, v_p,
        fre_p.reshape(bsz, n_groups, n_state), fim_p.reshape(bsz, n_groups, n_state),
        k_s.reshape(bd, 1, n_heads, head_dim), v_s.reshape(bd, 1, n_heads, head_dim),
        nre_s.reshape(bd, n_groups, n_state), nim_s.reshape(bd, n_groups, n_state),
    )


def kernel(x_prompt, x_sample, cache_k, cache_v, state_ssm_re, state_ssm_im, page_table, rel_bias, norm_g, w_in,
           q_norm_g, k_norm_g, lam_re, lam_im, log_dt, b_re, b_im, c_re, c_im, d_skip, w_glu, b_glu, w_branch_a,
           w_branch_b, w_out):
    depth = norm_g.shape[0]
    assert depth == 1, "one layer per step"
    outs = _layer(x_prompt, x_sample, cache_k, cache_v, state_ssm_re[0], state_ssm_im[0], page_table,
                  rel_bias, norm_g[0], w_in[0], q_norm_g[0], k_norm_g[0], lam_re[0], lam_im[0], log_dt[0],
                  b_re[0], b_im[0], c_re[0], c_im[0], d_skip[0], w_glu[0], b_glu[0], w_branch_a[0],
                  w_branch_b[0], w_out[0])
    y_p, y_s, k_p, v_p, fre_p, fim_p, k_s, v_s, nre_s, nim_s = outs
    add = lambda a: a[None]
    return (y_p, y_s, add(k_p), add(v_p), add(fre_p), add(fim_p), add(k_s), add(v_s), add(nre_s), add(nim_s))
```

```python
import functools
import math

import jax
import jax.numpy as jnp
from jax import lax
from jax.experimental import pallas as pl
from jax.experimental.pallas import tpu as pltpu

F32 = jnp.float32
BF16 = jnp.bfloat16
HIGHEST = lax.Precision.HIGHEST

MOBA_BLOCK = 256
MOBA_TOPK = 3
REL_BUCKETS = 32
REL_MAX_DIST = 4096
RMS_EPS = 1e-6
LOG2E = math.log2(math.e)
ATTN_HEADS_PER_STEP = 2
LANES = 128
BF16_SUBLANES = 16
VMEM_LIMIT = 56 * 1024 * 1024

NEG_INF = float("-inf")


def _cparams(n_axes, vmem=VMEM_LIMIT):
    return pltpu.CompilerParams(dimension_semantics=("arbitrary",) * n_axes, vmem_limit_bytes=vmem)


def _const_spec(shape, n_grid):
    zeros = (0,) * len(shape)
    if n_grid == 1:
        imap = lambda i: zeros
    elif n_grid == 2:
        imap = lambda i, j: zeros
    else:
        imap = lambda i, j, k: zeros
    return pl.BlockSpec(shape, imap, pipeline_mode=pl.Buffered(1))


def _proj_in_kernel(x_ref, g_ref, w_ref, qg_ref, kg_ref, bd_ref, *out_refs, aw, sw, dm, head_dim, prompt):
    if prompt:
        (q_ref, kb_ref, ksum_ref, kt_ref, vt_ref, vta_ref, sga_ref, u_ref, sgb_ref, sma_ref, smb_ref) = out_refs
    else:
        (q_ref, k_ref, v_ref, sga_ref, u_ref, sgb_ref, sma_ref, smb_ref) = out_refs
    x = x_ref[...]
    ms = jnp.mean(x * x, axis=-1, keepdims=True)
    h = (x * lax.rsqrt(ms + RMS_EPS) * g_ref[...]).astype(BF16)

    def seg(a, b):
        return jnp.dot(h, w_ref[:, a:b], preferred_element_type=F32)

    def headnorm(z, gain):
        ss = jnp.dot((z * z).astype(BF16), bd_ref[...], preferred_element_type=F32)
        return z * lax.rsqrt(ss * (1.0 / head_dim) + RMS_EPS) * gain

    o = 0
    q = headnorm(seg(o, o + aw), qg_ref[...])
    q_ref[...] = q.astype(q_ref.dtype)
    o += aw
    k = headnorm(seg(o, o + aw), kg_ref[...])
    o += aw
    v = seg(o, o + aw)
    o += aw
    if prompt:
        kb_ref[...] = k.astype(BF16)
        ksum_ref[0] = jnp.sum(k, axis=0, keepdims=True)
        kt_ref[0] = k.T
        vt = v.T
        vt_ref[0] = vt
        ones = jnp.ones((BF16_SUBLANES, vt.shape[1]), BF16)
        rows = head_dim + BF16_SUBLANES
        for hd in range(aw // head_dim):
            vta_ref[0, 0, hd * rows:hd * rows + head_dim, :] = vt[hd * head_dim:(hd + 1) * head_dim, :].astype(BF16)
            vta_ref[0, 0, hd * rows + head_dim:(hd + 1) * rows, :] = ones
    else:
        k_ref[...] = k
        v_ref[...] = v
    sga_ref[...] = jax.nn.silu(seg(o, o + aw)).astype(sga_ref.dtype)
    o += aw
    u_ref[...] = seg(o, o + sw).astype(u_ref.dtype)
    o += sw
    sgb_ref[...] = jax.nn.silu(seg(o, o + sw)).astype(sgb_ref.dtype)
    o += sw
    sma_ref[...] = jax.nn.sigmoid(seg(o, o + dm)).astype(sma_ref.dtype)
    o += dm
    smb_ref[...] = jax.nn.sigmoid(seg(o, o + dm)).astype(smb_ref.dtype)


def _proj_in(x2, g, w_bf, qg_t, kg_t, bd, *, tm, aw, sw, head_dim, q_dtype, act_dtype, prompt_batch=None):
    t, dm = x2.shape
    n_tiles = t // tm
    row = lambda w: pl.BlockSpec((tm, w), lambda i: (i, 0))
    prompt = prompt_batch is not None
    if prompt:
        assert tm == MOBA_BLOCK and n_tiles % prompt_batch == 0
        per_seq = n_tiles // prompt_batch
        seq = per_seq * tm
        n_heads = aw // head_dim
        va_rows = n_heads * (head_dim + BF16_SUBLANES)
        t_spec = pl.BlockSpec((1, aw, tm), lambda i: (i // per_seq, 0, i % per_seq))
        out_shape = [
            jax.ShapeDtypeStruct((t, aw), q_dtype),
            jax.ShapeDtypeStruct((t, aw), BF16),
            jax.ShapeDtypeStruct((n_tiles, 1, aw), F32),
            jax.ShapeDtypeStruct((prompt_batch, aw, seq), F32),
            jax.ShapeDtypeStruct((prompt_batch, aw, seq), F32),
            jax.ShapeDtypeStruct((prompt_batch, per_seq, va_rows, tm), BF16),
        ]
        out_specs = [row(aw), row(aw), pl.BlockSpec((1, 1, aw), lambda i: (i, 0, 0)), t_spec, t_spec,
                     pl.BlockSpec((1, 1, va_rows, tm), lambda i: (i // per_seq, i % per_seq, 0, 0))]
    else:
        out_shape = [
            jax.ShapeDtypeStruct((t, aw), q_dtype),
            jax.ShapeDtypeStruct((t, aw), F32),
            jax.ShapeDtypeStruct((t, aw), F32),
        ]
        out_specs = [row(aw)] * 3
    out_shape += [
        jax.ShapeDtypeStruct((t, aw), act_dtype),
        jax.ShapeDtypeStruct((t, sw), act_dtype),
        jax.ShapeDtypeStruct((t, sw), act_dtype),
        jax.ShapeDtypeStruct((t, dm), act_dtype),
        jax.ShapeDtypeStruct((t, dm), act_dtype),
    ]
    out_specs += [row(aw), row(sw), row(sw), row(dm), row(dm)]
    kern = functools.partial(_proj_in_kernel, aw=aw, sw=sw, dm=dm, head_dim=head_dim, prompt=prompt)
    return pl.pallas_call(
        kern,
        grid=(n_tiles,),
        in_specs=[
            row(dm),
            _const_spec((1, dm), 1),
            _const_spec(w_bf.shape, 1),
            _const_spec((1, aw), 1),
            _const_spec((1, aw), 1),
            _const_spec((aw, aw), 1),
        ],
        out_specs=out_specs,
        out_shape=out_shape,
        compiler_params=_cparams(1),
        name="proj_in",
    )(x2, g, w_bf, qg_t, kg_t, bd)


def _rel_bucket(dist):
    n = jnp.maximum(dist, 0)
    max_exact = REL_BUCKETS // 2
    nf = jnp.maximum(n, 1).astype(F32)
    large = max_exact + (
        jnp.log(nf / max_exact) * ((REL_BUCKETS - max_exact) / math.log(REL_MAX_DIST / max_exact))
    ).astype(jnp.int32)
    return jnp.where(n < max_exact, n, jnp.minimum(large, REL_BUCKETS - 1))


def _bias_lookup(bucket, relb_ref, head):
    val = jnp.zeros(bucket.shape, F32)
    for b in range(REL_BUCKETS):
        val = jnp.where(bucket == b, relb_ref[b, head], val)
    return val


def _bias_tiles_kernel(relb_ref, o_ref):
    h = pl.program_id(0)
    d = pl.program_id(1)
    key = lax.broadcasted_iota(jnp.int32, (MOBA_BLOCK, MOBA_BLOCK), 0)
    qry = lax.broadcasted_iota(jnp.int32, (MOBA_BLOCK, MOBA_BLOCK), 1)
    dist = d * MOBA_BLOCK + qry - key
    bias = _bias_lookup(_rel_bucket(dist), relb_ref, h) * LOG2E
    o_ref[0, 0] = jnp.where(dist >= 0, bias, NEG_INF)


def _n_bias_tiles(n_blocks):
    saturated = -(-(REL_MAX_DIST + MOBA_BLOCK - 1) // MOBA_BLOCK) + 1
    return min(n_blocks, saturated), n_blocks >= saturated


def _bias_tiles(rel_bias, n_heads, n_dist):
    return pl.pallas_call(
        _bias_tiles_kernel,
        grid=(n_heads, n_dist),
        in_specs=[pl.BlockSpec(memory_space=pltpu.SMEM)],
        out_specs=pl.BlockSpec((1, 1, MOBA_BLOCK, MOBA_BLOCK), lambda h, d: (h, d, 0, 0)),
        out_shape=jax.ShapeDtypeStruct((n_heads, n_dist, MOBA_BLOCK, MOBA_BLOCK), F32),
        compiler_params=_cparams(2),
        name="bias_tiles",
    )(rel_bias)


def _select_topk(s):
    rowi = lax.broadcasted_iota(jnp.int32, s.shape, 0)
    sel = jnp.zeros(s.shape, jnp.bool_)
    for _ in range(MOBA_TOPK):
        mx = jnp.max(s, axis=0, keepdims=True)
        cand = jnp.where(s == mx, rowi, s.shape[0])
        idx = jnp.min(cand, axis=0, keepdims=True)
        hit = (rowi == idx) & (mx > NEG_INF)
        sel = sel | hit
        s = jnp.where(hit, NEG_INF, s)
    return sel


def _attn_kernel(q_ref, kb_ref, vta_ref, ksum_ref, bias_ref, o_ref,
                 *scratch, head_dim, n_dist, far_const, scale, nh):
    t = pl.program_id(2)
    blk = MOBA_BLOCK
    n_blocks = vta_ref.shape[1]
    pair = nh * head_dim
    v_rows = head_dim + BF16_SUBLANES
    qts, sels, ms, accs, s_bufs, c_bufs, o_bufs = (scratch[i * nh:(i + 1) * nh] for i in range(7))

    qt = q_ref[0].astype(F32).T
    feat = lax.broadcasted_iota(jnp.int32, (pair, blk), 0)
    means = ksum_ref[0] * (1.0 / blk)
    blk_i = lax.broadcasted_iota(jnp.int32, (n_blocks, blk), 0)
    for h in range(nh):
        in_head = (feat >= h * head_dim) & (feat < (h + 1) * head_dim)
        qt_h = jnp.where(in_head, qt, 0.0)
        qts[h][...] = (qt_h * (scale * LOG2E)).astype(BF16)
        sc = jnp.dot(means, qt_h, preferred_element_type=F32, precision=HIGHEST)
        sc = jnp.where(blk_i < t, sc, NEG_INF)
        sels[h][...] = (_select_topk(sc) | (blk_i == t)).astype(F32)
        ms[h][...] = jnp.full(ms[h].shape, NEG_INF, F32)
        accs[h][...] = jnp.zeros(accs[h].shape, F32)

    def block_of(n):
        return jnp.where(n == 0, t, jnp.where(n <= t, n - 1, 0))

    def stage_a(n, slot, far):
        j = block_of(n)
        kj = kb_ref[0, pl.ds(pl.multiple_of(j * blk, blk), blk), :]
        tile = jnp.minimum(t - j, n_dist - 1)
        for h in range(nh):
            s = jnp.dot(kj, qts[h][...], preferred_element_type=F32)
            if far:
                off = bias_ref[h, n_dist - 1, 0:1, :]
            else:
                s = s + bias_ref[h, tile]
                off = jnp.zeros((1, blk), F32)
            s_bufs[h][slot] = s.astype(BF16)
            c_bufs[h][slot] = jnp.max(s, axis=0, keepdims=True) + off
            o_bufs[h][slot] = off

    def stage_b(n, slot):
        j = block_of(n)
        for h in range(nh):
            picked = jnp.where(n <= t, sels[h][pl.ds(j, 1), :], 0.0)
            m_old = ms[h][...]
            off = o_bufs[h][slot]
            shift = (jnp.maximum(m_old, c_bufs[h][slot]) - off).astype(BF16)
            m_blk = shift.astype(F32) + off
            m_new = jnp.where(picked > 0.0, m_blk, m_old)
            p = jnp.exp2(s_bufs[h][slot] - shift)
            upd = jnp.dot(vta_ref[0, j, h * v_rows:(h + 1) * v_rows, :], p, preferred_element_type=F32)
            accs[h][...] = accs[h][...] * jnp.exp2(m_old - m_new) + upd * picked
            ms[h][...] = m_new

    stage_a(jnp.int32(0), 0, False)
    stage_a(jnp.int32(1), 1, False)

    n_iter = (t + 2) // 2

    def two_blocks(i, carry):
        j = 2 * i
        both_far = (t - (j + 2) >= n_dist - 1) if far_const else False
        for parity in range(2):
            cur, nxt = 2 * parity, 2 - 2 * parity
            mine = (i % 2 == parity)
            for far in ((False, True) if far_const else (False,)):
                @pl.when(mine & (i < n_iter - 1) & (both_far == far))
                def _():
                    stage_a(j + 2, nxt, far)
                    stage_a(j + 3, nxt + 1, far)
                    stage_b(j, cur)
                    stage_b(j + 1, cur + 1)

            @pl.when(mine & (i == n_iter - 1))
            def _():
                stage_b(j, cur)
                stage_b(j + 1, cur + 1)

        return carry

    lax.fori_loop(0, n_iter, two_blocks, 0)

    outs = [accs[h][0:head_dim, :] / accs[h][head_dim:head_dim + 1, :] for h in range(nh)]
    o_ref[0] = jnp.concatenate(outs, axis=0).T.astype(o_ref.dtype)


def _attention(q_bf, k_bf, vt_aug, ksum, bias, *, head_dim, nh):
    b, l, aw = q_bf.shape
    width = nh * head_dim
    assert width % LANES == 0 and aw % width == 0 and l % MOBA_BLOCK == 0
    n_groups = aw // width
    n_blocks = l // MOBA_BLOCK
    n_dist, far_const = _n_bias_tiles(n_blocks)
    assert bias.shape[1] == n_dist
    v_rows = head_dim + BF16_SUBLANES
    assert vt_aug.shape == (b, n_blocks, nh * n_groups * v_rows, MOBA_BLOCK)
    kern = functools.partial(_attn_kernel, head_dim=head_dim, n_dist=n_dist, far_const=far_const,
                             scale=head_dim ** -0.5, nh=nh)
    one = pl.Buffered(1)
    return pl.pallas_call(
        kern,
        grid=(b, n_groups, n_blocks),
        in_specs=[
            pl.BlockSpec((1, MOBA_BLOCK, width), lambda bi, hg, t: (bi, t, hg)),
            pl.BlockSpec((1, l, width), lambda bi, hg, t: (bi, 0, hg), pipeline_mode=one),
            pl.BlockSpec((1, n_blocks, nh * v_rows, MOBA_BLOCK), lambda bi, hg, t: (bi, 0, hg, 0), pipeline_mode=one),
            pl.BlockSpec((1, n_blocks, width), lambda bi, hg, t: (bi, 0, hg), pipeline_mode=one),
            pl.BlockSpec((nh, n_dist, MOBA_BLOCK, MOBA_BLOCK), lambda bi, hg, t: (hg, 0, 0, 0), pipeline_mode=one),
        ],
        out_specs=pl.BlockSpec((1, MOBA_BLOCK, width), lambda bi, hg, t: (bi, t, hg)),
        out_shape=jax.ShapeDtypeStruct((b, l, aw), BF16),
        scratch_shapes=(
            [pltpu.VMEM((width, MOBA_BLOCK), BF16)] * nh
            + [pltpu.VMEM((n_blocks, MOBA_BLOCK), F32)] * nh
            + [pltpu.VMEM((1, MOBA_BLOCK), F32)] * nh
            + [pltpu.VMEM((v_rows, MOBA_BLOCK), F32)] * nh
            + [pltpu.VMEM((4, MOBA_BLOCK, MOBA_BLOCK), BF16)] * nh
            + [pltpu.VMEM((4, 1, MOBA_BLOCK), F32)] * nh
            + [pltpu.VMEM((4, 1, MOBA_BLOCK), F32)] * nh
        ),
        compiler_params=_cparams(3),
        name="moba_attn",
    )(q_bf, k_bf, vt_aug, ksum, bias)


def _ssm_prep_kernel(lre_ref, lim_ref, ldt_ref, are_ref, aim_ref, cre_ref, cim_ref):
    lre = lre_ref[...]
    lim = lim_ref[...]
    dt = jnp.exp(ldt_ref[...])
    mag = jnp.exp(lre * dt)
    a_re = mag * jnp.cos(lim * dt)
    a_im = mag * jnp.sin(lim * dt)
    are_ref[...] = a_re
    aim_ref[...] = a_im
    nr = a_re - 1.0
    den = lre * lre + lim * lim
    cre_ref[...] = (nr * lre + a_im * lim) / den
    cim_ref[...] = (a_im * lre - nr * lim) / den


def _ssm_prep(lam_re, lam_im, log_dt):
    g, p = lam_re.shape
    shp = jax.ShapeDtypeStruct((g, p), F32)
    return pl.pallas_call(_ssm_prep_kernel, out_shape=[shp] * 4, name="ssm_prep")(
        lam_re, lam_im, log_dt.reshape(g, 1))


def _block_diag_mask(rows, cols, row_group, col_group):
    r = lax.broadcasted_iota(jnp.int32, (rows, cols), 0) // row_group
    c = lax.broadcasted_iota(jnp.int32, (rows, cols), 1) // col_group
    return r == c


def _build_ssm_weights(coef_re_ref, coef_im_ref, bt_re_ref, bt_im_ref, ct_re_ref, ct_im_ref,
                       wb_re_ref, wb_im_ref, wc_re_ref, wc_im_ref, *, gch, nstate):
    halves, kh, nh = wb_re_ref.shape
    mask_b = _block_diag_mask(kh, nh, gch, nstate)
    mask_c = _block_diag_mask(nh, kh, nstate, gch)
    for i in range(halves):
        cr = coef_re_ref[i]
        ci = coef_im_ref[i]
        br = bt_re_ref[i * kh:(i + 1) * kh, :]
        bi = bt_im_ref[i * kh:(i + 1) * kh, :]
        wb_re_ref[i] = jnp.where(mask_b, cr * br - ci * bi, 0.0).astype(wb_re_ref.dtype)
        wb_im_ref[i] = jnp.where(mask_b, cr * bi + ci * br, 0.0).astype(wb_im_ref.dtype)
        wc_re_ref[i] = jnp.where(mask_c, ct_re_ref[i * nh:(i + 1) * nh, :], 0.0).astype(wc_re_ref.dtype)
        wc_im_ref[i] = jnp.where(mask_c, -ct_im_ref[i * nh:(i + 1) * nh, :], 0.0).astype(wc_im_ref.dtype)


def _ssm_in(u, wb_re_ref, wb_im_ref, precision=None):
    halves, kh, _ = wb_re_ref.shape
    re, im = [], []
    for i in range(halves):
        ui = u[:, i * kh:(i + 1) * kh]
        re.append(jnp.dot(ui, wb_re_ref[i], preferred_element_type=F32, precision=precision))
        im.append(jnp.dot(ui, wb_im_ref[i], preferred_element_type=F32, precision=precision))
    return jnp.concatenate(re, axis=1), jnp.concatenate(im, axis=1)


def _ssm_out(x_re, x_im, wc_re_ref, wc_im_ref, precision=None):
    halves, nh, _ = wc_re_ref.shape
    ys = []
    for i in range(halves):
        ys.append(jnp.dot(x_re[:, i * nh:(i + 1) * nh], wc_re_ref[i], preferred_element_type=F32, precision=precision)
                  + jnp.dot(x_im[:, i * nh:(i + 1) * nh], wc_im_ref[i], preferred_element_type=F32,
                            precision=precision))
    return jnp.concatenate(ys, axis=1)


def _glu_tail(y, u32, dskip_ref, wglu_ref, bglu_ref, sgb, sw, precision=None):
    y = y + dskip_ref[...] * u32
    za = jax.nn.gelu(y).astype(wglu_ref.dtype)
    glu = jnp.dot(za, wglu_ref[...], preferred_element_type=F32, precision=precision) + bglu_ref[...]
    return glu[:, :sw] * jax.nn.sigmoid(glu[:, sw:]) * sgb


def _s5_prompt_kernel(u_ref, sgb_ref, are_ref, aim_ref, coef_re_ref, coef_im_ref, bt_re_ref, bt_im_ref,
                      ct_re_ref, ct_im_ref, dskip_ref, wglu_ref, bglu_ref,
                      out_ref, fre_ref, fim_ref,
                      wb_re_ref, wb_im_ref, wc_re_ref, wc_im_ref, bu_re_ref, bu_im_ref, xs_re_ref, xs_im_ref,
                      st_re_ref, st_im_ref, *, gch, nstate, sw):
    b = pl.program_id(0)
    t = pl.program_id(1)
    tl = u_ref.shape[1]

    @pl.when((b == 0) & (t == 0))
    def _():
        _build_ssm_weights(coef_re_ref, coef_im_ref, bt_re_ref, bt_im_ref, ct_re_ref, ct_im_ref,
                           wb_re_ref, wb_im_ref, wc_re_ref, wc_im_ref, gch=gch, nstate=nstate)

    @pl.when(t == 0)
    def _():
        st_re_ref[...] = jnp.zeros_like(st_re_ref)
        st_im_ref[...] = jnp.zeros_like(st_im_ref)

    u = u_ref[0]
    bu_re, bu_im = _ssm_in(u, wb_re_ref, wb_im_ref)
    bu_re_ref[...] = bu_re
    bu_im_ref[...] = bu_im
    a_re = are_ref[...]
    a_im = aim_ref[...]

    def step(i, carry):
        xr, xi = carry
        nr = a_re * xr - a_im * xi + bu_re_ref[pl.ds(i, 1), :]
        ni = a_re * xi + a_im * xr + bu_im_ref[pl.ds(i, 1), :]
        xs_re_ref[pl.ds(i, 1), :] = nr
        xs_im_ref[pl.ds(i, 1), :] = ni
        return nr, ni

    xr, xi = lax.fori_loop(0, tl, step, (st_re_ref[...], st_im_ref[...]), unroll=8)
    st_re_ref[...] = xr
    st_im_ref[...] = xi
    fre_ref[0] = xr
    fim_ref[0] = xi

    y = _ssm_out(xs_re_ref[...].astype(BF16), xs_im_ref[...].astype(BF16), wc_re_ref, wc_im_ref)
    out = _glu_tail(y, u.astype(F32), dskip_ref, wglu_ref, bglu_ref, sgb_ref[0].astype(F32), sw)
    out_ref[0] = out.astype(out_ref.dtype)


def _s5_prompt(u, sgb, a_re, a_im, coef_re, coef_im, bt_re, bt_im, ct_re, ct_im, d_skip, w_glu, b_glu, *, tl,
               gch, nstate):
    b, l, sw = u.shape
    ns = a_re.shape[1]
    halves = 2
    kh = sw // halves
    nh = ns // halves
    kern = functools.partial(_s5_prompt_kernel, gch=gch, nstate=nstate, sw=sw)
    tile = pl.BlockSpec((1, tl, sw), lambda bi, t: (bi, t, 0))
    c2 = lambda shape: _const_spec(shape, 2)
    return pl.pallas_call(
        kern,
        grid=(b, l // tl),
        in_specs=[tile, tile, c2((1, ns)), c2((1, ns)), c2((halves, 1, nh)), c2((halves, 1, nh)),
                  c2(bt_re.shape), c2(bt_im.shape), c2(ct_re.shape), c2(ct_im.shape),
                  c2((1, sw)), c2(w_glu.shape), c2((1, 2 * sw))],
        out_specs=[tile,
                   pl.BlockSpec((1, 1, ns), lambda bi, t: (bi, 0, 0)),
                   pl.BlockSpec((1, 1, ns), lambda bi, t: (bi, 0, 0))],
        out_shape=[jax.ShapeDtypeStruct((b, l, sw), BF16),
                   jax.ShapeDtypeStruct((b, 1, ns), F32),
                   jax.ShapeDtypeStruct((b, 1, ns), F32)],
        scratch_shapes=[
            pltpu.VMEM((halves, kh, nh), BF16), pltpu.VMEM((halves, kh, nh), BF16),
            pltpu.VMEM((halves, nh, kh), BF16), pltpu.VMEM((halves, nh, kh), BF16),
            pltpu.VMEM((tl, ns), F32), pltpu.VMEM((tl, ns), F32),
            pltpu.VMEM((tl, ns), F32), pltpu.VMEM((tl, ns), F32),
            pltpu.VMEM((1, ns), F32), pltpu.VMEM((1, ns), F32),
        ],
        compiler_params=_cparams(2),
        name="s5_prompt",
    )(u, sgb, a_re, a_im, coef_re, coef_im, bt_re, bt_im, ct_re, ct_im, d_skip, w_glu, b_glu)


def _s5_sample_kernel(u_ref, sgb_ref, sre_ref, sim_ref, are_ref, aim_ref, coef_re_ref, coef_im_ref,
                      bt_re_ref, bt_im_ref, ct_re_ref, ct_im_ref, dskip_ref, wglu_ref, bglu_ref,
                      out_ref, nre_ref, nim_ref, wb_re_ref, wb_im_ref, wc_re_ref, wc_im_ref, *, gch, nstate, sw):
    _build_ssm_weights(coef_re_ref, coef_im_ref, bt_re_ref, bt_im_ref, ct_re_ref, ct_im_ref,
                       wb_re_ref, wb_im_ref, wc_re_ref, wc_im_ref, gch=gch, nstate=nstate)
    u = u_ref[...]
    bu_re, bu_im = _ssm_in(u, wb_re_ref, wb_im_ref, precision=HIGHEST)
    a_re = are_ref[...]
    a_im = aim_ref[...]
    s_re = sre_ref[...]
    s_im = sim_ref[...]
    x_re = a_re * s_re - a_im * s_im + bu_re
    x_im = a_re * s_im + a_im * s_re + bu_im
    nre_ref[...] = x_re
    nim_ref[...] = x_im
    y = _ssm_out(x_re, x_im, wc_re_ref, wc_im_ref, precision=HIGHEST)
    out_ref[...] = _glu_tail(y, u, dskip_ref, wglu_ref, bglu_ref, sgb_ref[...], sw, precision=HIGHEST)


def _s5_sample(u, sgb, s_re, s_im, a_re, a_im, coef_re, coef_im, bt_re, bt_im, ct_re, ct_im, d_skip, w_glu,
               b_glu, *, gch, nstate):
    n, sw = u.shape
    ns = a_re.shape[1]
    halves = 2
    kh = sw // halves
    nh = ns // halves
    kern = functools.partial(_s5_sample_kernel, gch=gch, nstate=nstate, sw=sw)
    return pl.pallas_call(
        kern,
        out_shape=[jax.ShapeDtypeStruct((n, sw), F32),
                   jax.ShapeDtypeStruct((n, ns), F32),
                   jax.ShapeDtypeStruct((n, ns), F32)],
        scratch_shapes=[
            pltpu.VMEM((halves, kh, nh), F32), pltpu.VMEM((halves, kh, nh), F32),
            pltpu.VMEM((halves, nh, kh), F32), pltpu.VMEM((halves, nh, kh), F32),
        ],
        compiler_params=pltpu.CompilerParams(vmem_limit_bytes=VMEM_LIMIT),
        name="s5_sample",
    )(u, sgb, s_re, s_im, a_re, a_im, coef_re, coef_im, bt_re, bt_im, ct_re, ct_im, d_skip, w_glu, b_glu)


def _proj_out_kernel(x_ref, attn_ref, sga_ref, ssm_ref, sma_ref, smb_ref, wa_ref, wb_ref, wo_ref, o_ref):
    attn_out = (attn_ref[...].astype(F32) * sga_ref[...].astype(F32)).astype(BF16)
    ya = jnp.dot(attn_out, wa_ref[...], preferred_element_type=F32)
    yb = jnp.dot(ssm_ref[...].astype(BF16), wb_ref[...], preferred_element_type=F32)
    merged = sma_ref[...].astype(F32) * ya + smb_ref[...].astype(F32) * yb
    o_ref[...] = x_ref[...] + jnp.dot(merged.astype(BF16), wo_ref[...], preferred_element_type=F32)


def _proj_out(x2, attn, sga, ssm, sma, smb, wa, wb, wo, *, tm):
    t, dm = x2.shape
    aw = attn.shape[1]
    sw = ssm.shape[1]
    row = lambda w: pl.BlockSpec((tm, w), lambda i: (i, 0))
    return pl.pallas_call(
        _proj_out_kernel,
        grid=(t // tm,),
        in_specs=[row(dm), row(aw), row(aw), row(sw), row(dm), row(dm),
                  _const_spec(wa.shape, 1), _const_spec(wb.shape, 1), _const_spec(wo.shape, 1)],
        out_specs=row(dm),
        out_shape=jax.ShapeDtypeStruct((t, dm), F32),
        compiler_params=_cparams(1),
        name="proj_out",
    )(x2, attn, sga, ssm, sma, smb, wa, wb, wo)


def _col_from_row(row, n):
    eye = lax.broadcasted_iota(jnp.int32, (n, n), 0) == lax.broadcasted_iota(jnp.int32, (n, n), 1)
    return jnp.sum(jnp.where(eye, jnp.broadcast_to(row, (n, n)), 0.0), axis=1, keepdims=True)


def _row_from_col(col, n):
    eye = lax.broadcasted_iota(jnp.int32, (n, n), 0) == lax.broadcasted_iota(jnp.int32, (n, n), 1)
    return jnp.sum(jnp.where(eye, jnp.broadcast_to(col, (n, n)), 0.0), axis=0, keepdims=True)


def _sample_attn_kernel(pt_ref, relb_ref, q_ref, kn_ref, vn_ref, ck_ref, cv_ref, o_ref,
                        kbuf, ksem, qb_ref, lg_ref, tb_ref, selv_ref, sels_ref, selsem, vbuf, vsem,
                        *, n_heads, head_dim, ppb, page, n_pages, n_ring, scale):
    b = pl.program_id(0)
    total = pl.num_programs(0) * n_pages
    n_blocks = n_pages // ppb
    past_len = n_pages * page
    n_sel = MOBA_TOPK

    def k_copy(g):
        phys = pt_ref[lax.div(g, n_pages), lax.rem(g, n_pages)]
        slot = lax.rem(g, n_ring)
        return pltpu.make_async_copy(ck_ref.at[0, phys], kbuf.at[slot], ksem.at[slot])

    head_i = lax.broadcasted_iota(jnp.int32, (n_heads, page), 0)

    @pl.when(b == 0)
    def _():
        for g in range(n_ring - 1):
            k_copy(jnp.int32(g)).start()
        for bk in range(REL_BUCKETS):
            tab = jnp.zeros((n_heads, page), F32)
            for h in range(n_heads):
                tab = jnp.where(head_i == h, relb_ref[bk, h], tab)
            tb_ref[bk] = tab

    q = q_ref[0]
    for h in range(n_heads):
        col = _col_from_row(q[:, h * head_dim:(h + 1) * head_dim], head_dim)
        qb_ref[h] = jnp.broadcast_to(col, (head_dim, page))

    def stream(pg, carry):
        g = b * n_pages + pg
        nxt = g + (n_ring - 1)

        @pl.when(nxt < total)
        def _():
            k_copy(nxt).start()

        k_copy(g).wait()
        kt = kbuf[lax.rem(g, n_ring)]
        lg_ref[pg] = jnp.sum(kt * qb_ref[...], axis=1)
        return carry

    lax.fori_loop(0, n_pages, stream, 0)

    lane = lax.broadcasted_iota(jnp.int32, (n_heads, LANES), 1)
    sc = jnp.full((n_heads, LANES), NEG_INF, F32)
    for jb in range(n_blocks):
        tot = lg_ref[jb * ppb]
        for i in range(1, ppb):
            tot = tot + lg_ref[jb * ppb + i]
        sc = jnp.where(lane == jb, jnp.sum(tot, axis=1, keepdims=True) * (1.0 / MOBA_BLOCK), sc)
    sel = jnp.full((n_heads, LANES), -1, jnp.int32)
    for r in range(n_sel):
        mx = jnp.max(sc, axis=1, keepdims=True)
        idx = jnp.min(jnp.where(sc == mx, lane, LANES), axis=1, keepdims=True)
        ok = mx > NEG_INF
        sel = jnp.where(lane == r, jnp.where(ok, idx, -1), sel)
        sc = jnp.where((lane == idx) & ok, NEG_INF, sc)
    selv_ref[...] = sel
    to_smem = pltpu.make_async_copy(selv_ref, sels_ref, selsem)
    to_smem.start()
    to_smem.wait()

    def chosen(h, r, i):
        j = sels_ref[h, r]
        pgl = jnp.maximum(j, 0) * ppb + i
        return j, pgl, (h * n_sel + r) * ppb + i

    def v_copy(h, r, i):
        _, pgl, slot = chosen(h, r, i)
        return pltpu.make_async_copy(cv_ref.at[0, pt_ref[b, pgl], h], vbuf.at[slot], vsem.at[slot])

    for h in range(n_heads):
        for r in range(n_sel):
            for i in range(ppb):
                @pl.when(chosen(h, r, i)[0] >= 0)
                def _():
                    v_copy(h, r, i).start()

    pos = lax.broadcasted_iota(jnp.int32, (n_heads, page), 1)
    qk_new = q * kn_ref[0]
    s_self = jnp.zeros((n_heads, 1), F32)
    for h in range(n_heads):
        dot_h = jnp.sum(qk_new[:, h * head_dim:(h + 1) * head_dim], axis=1, keepdims=True)
        s_self = jnp.where(head_i[:, 0:1] == h, dot_h, s_self)
    s_self = s_self * scale + tb_ref[0][:, 0:1]
    tiles = []
    for r in range(n_sel):
        blk_r = jnp.sum(jnp.where(lane == r, sel, 0), axis=1, keepdims=True)
        for i in range(ppb):
            raw = jnp.zeros((n_heads, page), F32)
            for h in range(n_heads):
                raw = jnp.where(head_i == h, lg_ref[chosen(h, r, i)[1]][h:h + 1, :], raw)
            bucket = _rel_bucket(past_len - (jnp.maximum(blk_r, 0) * ppb + i) * page - pos)
            bias = jnp.zeros((n_heads, page), F32)
            for bk in range(REL_BUCKETS):
                bias = jnp.where(bucket == bk, tb_ref[bk], bias)
            tiles.append(jnp.where(blk_r >= 0, raw * scale + bias, NEG_INF))
    m = s_self
    for s in tiles:
        m = jnp.maximum(m, jnp.max(s, axis=1, keepdims=True))
    p_self = jnp.exp(s_self - m)
    den = p_self
    probs = []
    for s in tiles:
        p = jnp.exp(s - m)
        den = den + jnp.sum(p, axis=1, keepdims=True)
        probs.append(p)
    rows = []
    for h in range(n_heads):
        acc = jnp.zeros((head_dim, page), F32)
        for n, p in enumerate(probs):
            j, _, slot = chosen(h, n // ppb, n % ppb)

            @pl.when(j >= 0)
            def _():
                v_copy(h, n // ppb, n % ppb).wait()

            acc = acc + jnp.where(j >= 0, vbuf[slot], 0.0) * p[h:h + 1, :]
        o_row = _row_from_col(jnp.sum(acc, axis=1, keepdims=True), head_dim)
        v_self = vn_ref[0][:, h * head_dim:(h + 1) * head_dim]
        rows.append((o_row + p_self[h:h + 1, :] * v_self) / den[h:h + 1, :])
    o_ref[0] = jnp.concatenate(rows, axis=1)


def _sample_attention(page_table, rel_bias, q, k_new, v_new, cache_kt, cache_vt, *, ppb, n_ring=32):
    bd, w = q.shape
    _, n_pool, n_heads, head_dim, page = cache_kt.shape
    n_pages = page_table.shape[1]
    assert n_pages % ppb == 0 and n_pages // ppb <= LANES and n_heads * head_dim == w
    n_slots = n_heads * MOBA_TOPK * ppb
    n_ring = min(n_ring, bd * n_pages)
    kern = functools.partial(_sample_attn_kernel, n_heads=n_heads, head_dim=head_dim, ppb=ppb, page=page,
                             n_pages=n_pages, n_ring=n_ring, scale=head_dim ** -0.5)
    row = pl.BlockSpec((1, 1, w), lambda b, pt: (b, 0, 0))
    grid_spec = pltpu.PrefetchScalarGridSpec(
        num_scalar_prefetch=1,
        grid=(bd,),
        in_specs=[pl.BlockSpec(memory_space=pltpu.SMEM), row, row, row,
                  pl.BlockSpec(memory_space=pl.ANY), pl.BlockSpec(memory_space=pl.ANY)],
        out_specs=row,
        scratch_shapes=[
            pltpu.VMEM((n_ring, n_heads, head_dim, page), F32),
            pltpu.SemaphoreType.DMA((n_ring,)),
            pltpu.VMEM((n_heads, head_dim, page), F32),
            pltpu.VMEM((n_pages, n_heads, page), F32),
            pltpu.VMEM((REL_BUCKETS, n_heads, page), F32),
            pltpu.VMEM((n_heads, LANES), jnp.int32),
            pltpu.SMEM((n_heads, LANES), jnp.int32),
            pltpu.SemaphoreType.DMA(()),
            pltpu.VMEM((n_slots, head_dim, page), F32),
            pltpu.SemaphoreType.DMA((n_slots,)),
        ],
    )
    out = pl.pallas_call(
        kern,
        grid_spec=grid_spec,
        out_shape=jax.ShapeDtypeStruct((bd, 1, w), F32),
        compiler_params=_cparams(1),
        name="sample_attn",
    )(page_table, rel_bias, q.reshape(bd, 1, w), k_new.reshape(bd, 1, w), v_new.reshape(bd, 1, w),
      cache_kt, cache_vt)
    return out.reshape(bd, w)


def _layer(x_prompt, x_sample, cache_k, cache_v, state_re, state_im, page_table, rel_bias, norm_g, w_in,
           q_norm_g, k_norm_g, lam_re, lam_im, log_dt, b_re, b_im, c_re, c_im, d_skip, w_glu, b_glu,
           w_branch_a, w_branch_b, w_out):
    bsz, seq, dm = x_prompt.shape
    bd, s_new, _ = x_sample.shape
    _, n_pool, page, n_heads, head_dim = cache_k.shape
    n_groups, n_state, gch = b_re.shape
    aw = n_heads * head_dim
    sw = n_groups * gch
    ns = n_groups * n_state
    n_pages = page_table.shape[1]
    past_len = n_pages * page
    ppb = MOBA_BLOCK // page
    assert s_new == 1 and seq % MOBA_BLOCK == 0 and past_len % MOBA_BLOCK == 0
    assert w_in.shape == (dm, 4 * aw + 2 * sw + 2 * dm)

    w_in_bf = w_in.astype(BF16)
    g_row = norm_g.reshape(1, dm)
    qg_t = jnp.tile(q_norm_g.reshape(1, head_dim), (1, n_heads))
    kg_t = jnp.tile(k_norm_g.reshape(1, head_dim), (1, n_heads))
    head_of = jnp.arange(aw) // head_dim
    bd_ones = (head_of[:, None] == head_of[None, :]).astype(BF16)
    halves = 2
    bt_re = jnp.tile(b_re.transpose(0, 2, 1).reshape(sw, n_state), (1, n_groups // halves))
    bt_im = jnp.tile(b_im.transpose(0, 2, 1).reshape(sw, n_state), (1, n_groups // halves))
    ct_re = jnp.tile(c_re.transpose(0, 2, 1).reshape(ns, gch), (1, n_groups // halves))
    ct_im = jnp.tile(c_im.transpose(0, 2, 1).reshape(ns, gch), (1, n_groups // halves))
    dsk = d_skip.reshape(1, sw)
    bgl = b_glu.reshape(1, 2 * sw)

    a_re, a_im, coef_re, coef_im = _ssm_prep(lam_re, lam_im, log_dt)
    a_re = a_re.reshape(1, ns)
    a_im = a_im.reshape(1, ns)
    coef_re = coef_re.reshape(halves, 1, ns // halves)
    coef_im = coef_im.reshape(halves, 1, ns // halves)

    wa_bf = w_branch_a.astype(BF16)
    wb_bf = w_branch_b.astype(BF16)
    wo_bf = w_out.astype(BF16)
    wglu_bf = w_glu.astype(BF16)

    xp2 = x_prompt.reshape(bsz * seq, dm)
    (q_p, kb_p, ksum_p, kt_p, vt_p, vta_p, sga_p, u_p, sgb_p, sma_p, smb_p) = _proj_in(
        xp2, g_row, w_in_bf, qg_t, kg_t, bd_ones, tm=MOBA_BLOCK, aw=aw, sw=sw, head_dim=head_dim,
        q_dtype=BF16, act_dtype=BF16, prompt_batch=bsz)
    n_blocks = seq // MOBA_BLOCK
    bias = _bias_tiles(rel_bias, n_heads, _n_bias_tiles(n_blocks)[0])
    attn_p = _attention(q_p.reshape(bsz, seq, aw), kb_p.reshape(bsz, seq, aw), vta_p,
                        ksum_p.reshape(bsz, n_blocks, aw), bias, head_dim=head_dim, nh=ATTN_HEADS_PER_STEP)
    k_p = kt_p.reshape(bsz, n_heads, head_dim, seq).transpose(0, 3, 1, 2)
    v_p = vt_p.reshape(bsz, n_heads, head_dim, seq).transpose(0, 3, 1, 2)
    ssm_p, fre_p, fim_p = _s5_prompt(u_p.reshape(bsz, seq, sw), sgb_p.reshape(bsz, seq, sw), a_re, a_im, coef_re,
                                     coef_im, bt_re, bt_im, ct_re, ct_im, dsk, wglu_bf, bgl, tl=MOBA_BLOCK,
                                     gch=gch, nstate=n_state)
    y_p = _proj_out(xp2, attn_p.reshape(bsz * seq, aw), sga_p, ssm_p.reshape(bsz * seq, sw), sma_p, smb_p,
                    wa_bf, wb_bf, wo_bf, tm=MOBA_BLOCK)

    xs2 = x_sample.reshape(bd, dm)
    (q_s, k_s, v_s, sga_s, u_s, sgb_s, sma_s, smb_s) = _proj_in(
        xs2, g_row, w_in_bf, qg_t, kg_t, bd_ones, tm=bd, aw=aw, sw=sw, head_dim=head_dim,
        q_dtype=F32, act_dtype=F32)
    cache_kt = cache_k.transpose(0, 1, 3, 4, 2)
    cache_vt = cache_v.transpose(0, 1, 3, 4, 2)
    attn_s = _sample_attention(page_table, rel_bias, q_s, k_s, v_s, cache_kt, cache_vt, ppb=ppb)
    ssm_s, nre_s, nim_s = _s5_sample(u_s, sgb_s, state_re.reshape(bd, ns), state_im.reshape(bd, ns), a_re, a_im,
                                     coef_re, coef_im, bt_re, bt_im, ct_re, ct_im, dsk, w_glu, bgl, gch=gch,
                                     nstate=n_state)
    y_s = _proj_out(xs2, attn_s, sga_s, ssm_s, sma_s, smb_s, wa_bf, wb_bf, wo_bf, tm=bd)

    return (
        y_p.reshape(bsz, seq, dm), y_s.reshape(bd, 1, dm),
        k_p, v_p,
        fre_p.reshape(bsz, n_groups, n_state), fim_p.reshape(bsz, n_groups, n_state),
        k_s.reshape(bd, 1, n_heads, head_dim), v_s.reshape(bd, 1, n_heads, head_dim),
        nre_s.reshape(bd, n_groups, n_state), nim_s.reshape(bd, n_groups, n_state),
    )


def kernel(x_prompt, x_sample, cache_k, cache_v, state_ssm_re, state_ssm_im, page_table, rel_bias, norm_g, w_in,
           q_norm_g, k_norm_g, lam_re, lam_im, log_dt, b_re, b_im, c_re, c_im, d_skip, w_glu, b_glu, w_branch_a,
           w_branch_b, w_out):
    depth = norm_g.shape[0]
    assert depth == 1, "one layer per step"
    outs = _layer(x_prompt, x_sample, cache_k, cache_v, state_ssm_re[0], state_ssm_im[0], page_table,
                  rel_bias, norm_g[0], w_in[0], q_norm_g[0], k_norm_g[0], lam_re[0], lam_im[0], log_dt[0],
                  b_re[0], b_im[0], c_re[0], c_im[0], d_skip[0], w_glu[0], b_glu[0], w_branch_a[0],
                  w_branch_b[0], w_out[0])
    y_p, y_s, k_p, v_p, fre_p, fim_p, k_s, v_s, nre_s, nim_s = outs
    add = lambda a: a[None]
    return (y_p, y_s, add(k_p), add(v_p), add(fre_p), add(fim_p), add(k_s), add(v_s), add(nre_s), add(nim_s))
```

```python
import functools
import math

import jax
import jax.numpy as jnp
from jax import lax
from jax.experimental import pallas as pl
from jax.experimental.pallas import tpu as pltpu

F32 = jnp.float32
BF16 = jnp.bfloat16
HIGHEST = lax.Precision.HIGHEST

MOBA_BLOCK = 256
MOBA_TOPK = 3
REL_BUCKETS = 32
REL_MAX_DIST = 4096
RMS_EPS = 1e-6
LOG2E = math.log2(math.e)
ATTN_HEADS_PER_STEP = 4
LANES = 128
BF16_SUBLANES = 16
VMEM_LIMIT = 56 * 1024 * 1024

NEG_INF = float("-inf")


def _cparams(n_axes, vmem=VMEM_LIMIT):
    return pltpu.CompilerParams(dimension_semantics=("arbitrary",) * n_axes, vmem_limit_bytes=vmem)


def _const_spec(shape, n_grid):
    zeros = (0,) * len(shape)
    if n_grid == 1:
        imap = lambda i: zeros
    elif n_grid == 2:
        imap = lambda i, j: zeros
    else:
        imap = lambda i, j, k: zeros
    return pl.BlockSpec(shape, imap, pipeline_mode=pl.Buffered(1))


def _proj_in_kernel(x_ref, g_ref, w_ref, qg_ref, kg_ref, bd_ref, *out_refs, aw, sw, dm, head_dim, prompt):
    if prompt:
        (q_ref, kb_ref, ksum_ref, kt_ref, vt_ref, vta_ref, sga_ref, u_ref, sgb_ref, sma_ref, smb_ref) = out_refs
    else:
        (q_ref, k_ref, v_ref, sga_ref, u_ref, sgb_ref, sma_ref, smb_ref) = out_refs
    x = x_ref[...]
    ms = jnp.mean(x * x, axis=-1, keepdims=True)
    h = (x * lax.rsqrt(ms + RMS_EPS) * g_ref[...]).astype(BF16)

    def seg(a, b):
        return jnp.dot(h, w_ref[:, a:b], preferred_element_type=F32)

    def headnorm(z, gain):
        ss = jnp.dot((z * z).astype(BF16), bd_ref[...], preferred_element_type=F32)
        return z * lax.rsqrt(ss * (1.0 / head_dim) + RMS_EPS) * gain

    o = 0
    q = headnorm(seg(o, o + aw), qg_ref[...])
    q_ref[...] = q.astype(q_ref.dtype)
    o += aw
    k = headnorm(seg(o, o + aw), kg_ref[...])
    o += aw
    v = seg(o, o + aw)
    o += aw
    if prompt:
        kb_ref[...] = k.astype(BF16)
        ksum_ref[0] = jnp.sum(k, axis=0, keepdims=True)
        kt_ref[0] = k.T
        vt = v.T
        vt_ref[0] = vt
        ones = jnp.ones((BF16_SUBLANES, vt.shape[1]), BF16)
        rows = head_dim + BF16_SUBLANES
        for hd in range(aw // head_dim):
            vta_ref[0, 0, hd * rows:hd * rows + head_dim, :] = vt[hd * head_dim:(hd + 1) * head_dim, :].astype(BF16)
            vta_ref[0, 0, hd * rows + head_dim:(hd + 1) * rows, :] = ones
    else:
        k_ref[...] = k
        v_ref[...] = v
    sga_ref[...] = jax.nn.silu(seg(o, o + aw)).astype(sga_ref.dtype)
    o += aw
    u_ref[...] = seg(o, o + sw).astype(u_ref.dtype)
    o += sw
    sgb_ref[...] = jax.nn.silu(seg(o, o + sw)).astype(sgb_ref.dtype)
    o += sw
    sma_ref[...] = jax.nn.sigmoid(seg(o, o + dm)).astype(sma_ref.dtype)
    o += dm
    smb_ref[...] = jax.nn.sigmoid(seg(o, o + dm)).astype(smb_ref.dtype)


def _proj_in(x2, g, w_bf, qg_t, kg_t, bd, *, tm, aw, sw, head_dim, q_dtype, act_dtype, prompt_batch=None):
    t, dm = x2.shape
    n_tiles = t // tm
    row = lambda w: pl.BlockSpec((tm, w), lambda i: (i, 0))
    prompt = prompt_batch is not None
    if prompt:
        assert tm == MOBA_BLOCK and n_tiles % prompt_batch == 0
        per_seq = n_tiles // prompt_batch
        seq = per_seq * tm
        n_heads = aw // head_dim
        va_rows = n_heads * (head_dim + BF16_SUBLANES)
        t_spec = pl.BlockSpec((1, aw, tm), lambda i: (i // per_seq, 0, i % per_seq))
        out_shape = [
            jax.ShapeDtypeStruct((t, aw), q_dtype),
            jax.ShapeDtypeStruct((t, aw), BF16),
            jax.ShapeDtypeStruct((n_tiles, 1, aw), F32),
            jax.ShapeDtypeStruct((prompt_batch, aw, seq), F32),
            jax.ShapeDtypeStruct((prompt_batch, aw, seq), F32),
            jax.ShapeDtypeStruct((prompt_batch, per_seq, va_rows, tm), BF16),
        ]
        out_specs = [row(aw), row(aw), pl.BlockSpec((1, 1, aw), lambda i: (i, 0, 0)), t_spec, t_spec,
                     pl.BlockSpec((1, 1, va_rows, tm), lambda i: (i // per_seq, i % per_seq, 0, 0))]
    else:
        out_shape = [
            jax.ShapeDtypeStruct((t, aw), q_dtype),
            jax.ShapeDtypeStruct((t, aw), F32),
            jax.ShapeDtypeStruct((t, aw), F32),
        ]
        out_specs = [row(aw)] * 3
    out_shape += [
        jax.ShapeDtypeStruct((t, aw), act_dtype),
        jax.ShapeDtypeStruct((t, sw), act_dtype),
        jax.ShapeDtypeStruct((t, sw), act_dtype),
        jax.ShapeDtypeStruct((t, dm), act_dtype),
        jax.ShapeDtypeStruct((t, dm), act_dtype),
    ]
    out_specs += [row(aw), row(sw), row(sw), row(dm), row(dm)]
    kern = functools.partial(_proj_in_kernel, aw=aw, sw=sw, dm=dm, head_dim=head_dim, prompt=prompt)
    return pl.pallas_call(
        kern,
        grid=(n_tiles,),
        in_specs=[
            row(dm),
            _const_spec((1, dm), 1),
            _const_spec(w_bf.shape, 1),
            _const_spec((1, aw), 1),
            _const_spec((1, aw), 1),
            _const_spec((aw, aw), 1),
        ],
        out_specs=out_specs,
        out_shape=out_shape,
        compiler_params=_cparams(1),
        name="proj_in",
    )(x2, g, w_bf, qg_t, kg_t, bd)


def _rel_bucket(dist):
    n = jnp.maximum(dist, 0)
    max_exact = REL_BUCKETS // 2
    nf = jnp.maximum(n, 1).astype(F32)
    large = max_exact + (
        jnp.log(nf / max_exact) * ((REL_BUCKETS - max_exact) / math.log(REL_MAX_DIST / max_exact))
    ).astype(jnp.int32)
    return jnp.where(n < max_exact, n, jnp.minimum(large, REL_BUCKETS - 1))


def _bias_lookup(bucket, relb_ref, head):
    val = jnp.zeros(bucket.shape, F32)
    for b in range(REL_BUCKETS):
        val = jnp.where(bucket == b, relb_ref[b, head], val)
    return val


def _bias_tiles_kernel(relb_ref, o_ref):
    h = pl.program_id(0)
    d = pl.program_id(1)
    key = lax.broadcasted_iota(jnp.int32, (MOBA_BLOCK, MOBA_BLOCK), 0)
    qry = lax.broadcasted_iota(jnp.int32, (MOBA_BLOCK, MOBA_BLOCK), 1)
    dist = d * MOBA_BLOCK + qry - key
    bias = _bias_lookup(_rel_bucket(dist), relb_ref, h) * LOG2E
    o_ref[0, 0] = jnp.where(dist >= 0, bias, NEG_INF)


def _n_bias_tiles(n_blocks):
    saturated = -(-(REL_MAX_DIST + MOBA_BLOCK - 1) // MOBA_BLOCK) + 1
    return min(n_blocks, saturated), n_blocks >= saturated


def _bias_tiles(rel_bias, n_heads, n_dist):
    return pl.pallas_call(
        _bias_tiles_kernel,
        grid=(n_heads, n_dist),
        in_specs=[pl.BlockSpec(memory_space=pltpu.SMEM)],
        out_specs=pl.BlockSpec((1, 1, MOBA_BLOCK, MOBA_BLOCK), lambda h, d: (h, d, 0, 0)),
        out_shape=jax.ShapeDtypeStruct((n_heads, n_dist, MOBA_BLOCK, MOBA_BLOCK), F32),
        compiler_params=_cparams(2),
        name="bias_tiles",
    )(rel_bias)


def _select_topk(s):
    rowi = lax.broadcasted_iota(jnp.int32, s.shape, 0)
    sel = jnp.zeros(s.shape, jnp.bool_)
    for _ in range(MOBA_TOPK):
        mx = jnp.max(s, axis=0, keepdims=True)
        cand = jnp.where(s == mx, rowi, s.shape[0])
        idx = jnp.min(cand, axis=0, keepdims=True)
        hit = (rowi == idx) & (mx > NEG_INF)
        sel = sel | hit
        s = jnp.where(hit, NEG_INF, s)
    return sel


def _attn_kernel(q_ref, kb_ref, vta_ref, ksum_ref, bias_ref, o_ref,
                 *scratch, head_dim, n_dist, far_const, scale, nh):
    t = pl.program_id(2)
    blk = MOBA_BLOCK
    n_blocks = vta_ref.shape[1]
    pair = nh * head_dim
    v_rows = head_dim + BF16_SUBLANES
    qts, sels, ms, accs, s_bufs, c_bufs, o_bufs = (scratch[i * nh:(i + 1) * nh] for i in range(7))

    qt = q_ref[0].astype(F32).T
    feat = lax.broadcasted_iota(jnp.int32, (pair, blk), 0)
    means = ksum_ref[0] * (1.0 / blk)
    blk_i = lax.broadcasted_iota(jnp.int32, (n_blocks, blk), 0)
    for h in range(nh):
        in_head = (feat >= h * head_dim) & (feat < (h + 1) * head_dim)
        qt_h = jnp.where(in_head, qt, 0.0)
        qts[h][...] = (qt_h * (scale * LOG2E)).astype(BF16)
        sc = jnp.dot(means, qt_h, preferred_element_type=F32, precision=HIGHEST)
        sc = jnp.where(blk_i < t, sc, NEG_INF)
        sels[h][...] = (_select_topk(sc) | (blk_i == t)).astype(F32)
        ms[h][...] = jnp.full(ms[h].shape, NEG_INF, F32)
        accs[h][...] = jnp.zeros(accs[h].shape, F32)

    def block_of(n):
        return jnp.where(n == 0, t, jnp.where(n <= t, n - 1, 0))

    def stage_a(n, slot, far):
        j = block_of(n)
        kj = kb_ref[0, pl.ds(pl.multiple_of(j * blk, blk), blk), :]
        tile = jnp.minimum(t - j, n_dist - 1)
        for h in range(nh):
            s = jnp.dot(kj, qts[h][...], preferred_element_type=F32)
            if far:
                off = bias_ref[h, n_dist - 1, 0:1, :]
            else:
                s = s + bias_ref[h, tile]
                off = jnp.zeros((1, blk), F32)
            s_bufs[h][slot] = s.astype(BF16)
            c_bufs[h][slot] = jnp.max(s, axis=0, keepdims=True) + off
            o_bufs[h][slot] = off

    def stage_b(n, slot):
        j = block_of(n)
        for h in range(nh):
            picked = jnp.where(n <= t, sels[h][pl.ds(j, 1), :], 0.0)
            m_old = ms[h][...]
            off = o_bufs[h][slot]
            shift = (jnp.maximum(m_old, c_bufs[h][slot]) - off).astype(BF16)
            m_blk = shift.astype(F32) + off
            m_new = jnp.where(picked > 0.0, m_blk, m_old)
            p = jnp.exp2(s_bufs[h][slot] - shift)
            upd = jnp.dot(vta_ref[0, j, h * v_rows:(h + 1) * v_rows, :], p, preferred_element_type=F32)
            accs[h][...] = accs[h][...] * jnp.exp2(m_old - m_new) + upd * picked
            ms[h][...] = m_new

    stage_a(jnp.int32(0), 0, False)
    stage_a(jnp.int32(1), 1, False)

    n_iter = (t + 2) // 2

    def two_blocks(i, carry):
        j = 2 * i
        both_far = (t - (j + 2) >= n_dist - 1) if far_const else False
        for parity in range(2):
            cur, nxt = 2 * parity, 2 - 2 * parity
            mine = (i % 2 == parity)
            for far in ((False, True) if far_const else (False,)):
                @pl.when(mine & (i < n_iter - 1) & (both_far == far))
                def _():
                    stage_a(j + 2, nxt, far)
                    stage_a(j + 3, nxt + 1, far)
                    stage_b(j, cur)
                    stage_b(j + 1, cur + 1)

            @pl.when(mine & (i == n_iter - 1))
            def _():
                stage_b(j, cur)
                stage_b(j + 1, cur + 1)

        return carry

    lax.fori_loop(0, n_iter, two_blocks, 0)

    outs = [accs[h][0:head_dim, :] / accs[h][head_dim:head_dim + 1, :] for h in range(nh)]
    o_ref[0] = jnp.concatenate(outs, axis=0).T.astype(o_ref.dtype)


def _attention(q_bf, k_bf, vt_aug, ksum, bias, *, head_dim, nh):
    b, l, aw = q_bf.shape
    width = nh * head_dim
    assert width % LANES == 0 and aw % width == 0 and l % MOBA_BLOCK == 0
    n_groups = aw // width
    n_blocks = l // MOBA_BLOCK
    n_dist, far_const = _n_bias_tiles(n_blocks)
    assert bias.shape[1] == n_dist
    v_rows = head_dim + BF16_SUBLANES
    assert vt_aug.shape == (b, n_blocks, nh * n_groups * v_rows, MOBA_BLOCK)
    kern = functools.partial(_attn_kernel, head_dim=head_dim, n_dist=n_dist, far_const=far_const,
                             scale=head_dim ** -0.5, nh=nh)
    one = pl.Buffered(1)
    return pl.pallas_call(
        kern,
        grid=(b, n_groups, n_blocks),
        in_specs=[
            pl.BlockSpec((1, MOBA_BLOCK, width), lambda bi, hg, t: (bi, t, hg)),
            pl.BlockSpec((1, l, width), lambda bi, hg, t: (bi, 0, hg), pipeline_mode=one),
            pl.BlockSpec((1, n_blocks, nh * v_rows, MOBA_BLOCK), lambda bi, hg, t: (bi, 0, hg, 0), pipeline_mode=one),
            pl.BlockSpec((1, n_blocks, width), lambda bi, hg, t: (bi, 0, hg), pipeline_mode=one),
            pl.BlockSpec((nh, n_dist, MOBA_BLOCK, MOBA_BLOCK), lambda bi, hg, t: (hg, 0, 0, 0), pipeline_mode=one),
        ],
        out_specs=pl.BlockSpec((1, MOBA_BLOCK, width), lambda bi, hg, t: (bi, t, hg)),
        out_shape=jax.ShapeDtypeStruct((b, l, aw), BF16),
        scratch_shapes=(
            [pltpu.VMEM((width, MOBA_BLOCK), BF16)] * nh
            + [pltpu.VMEM((n_blocks, MOBA_BLOCK), F32)] * nh
            + [pltpu.VMEM((1, MOBA_BLOCK), F32)] * nh
            + [pltpu.VMEM((v_rows, MOBA_BLOCK), F32)] * nh
            + [pltpu.VMEM((4, MOBA_BLOCK, MOBA_BLOCK), BF16)] * nh
            + [pltpu.VMEM((4, 1, MOBA_BLOCK), F32)] * nh
            + [pltpu.VMEM((4, 1, MOBA_BLOCK), F32)] * nh
        ),
        compiler_params=_cparams(3),
        name="moba_attn",
    )(q_bf, k_bf, vt_aug, ksum, bias)


def _ssm_prep_kernel(lre_ref, lim_ref, ldt_ref, are_ref, aim_ref, cre_ref, cim_ref):
    lre = lre_ref[...]
    lim = lim_ref[...]
    dt = jnp.exp(ldt_ref[...])
    mag = jnp.exp(lre * dt)
    a_re = mag * jnp.cos(lim * dt)
    a_im = mag * jnp.sin(lim * dt)
    are_ref[...] = a_re
    aim_ref[...] = a_im
    nr = a_re - 1.0
    den = lre * lre + lim * lim
    cre_ref[...] = (nr * lre + a_im * lim) / den
    cim_ref[...] = (a_im * lre - nr * lim) / den


def _ssm_prep(lam_re, lam_im, log_dt):
    g, p = lam_re.shape
    shp = jax.ShapeDtypeStruct((g, p), F32)
    return pl.pallas_call(_ssm_prep_kernel, out_shape=[shp] * 4, name="ssm_prep")(
        lam_re, lam_im, log_dt.reshape(g, 1))


def _block_diag_mask(rows, cols, row_group, col_group):
    r = lax.broadcasted_iota(jnp.int32, (rows, cols), 0) // row_group
    c = lax.broadcasted_iota(jnp.int32, (rows, cols), 1) // col_group
    return r == c


def _build_ssm_weights(coef_re_ref, coef_im_ref, bt_re_ref, bt_im_ref, ct_re_ref, ct_im_ref,
                       wb_re_ref, wb_im_ref, wc_re_ref, wc_im_ref, *, gch, nstate):
    halves, kh, nh = wb_re_ref.shape
    mask_b = _block_diag_mask(kh, nh, gch, nstate)
    mask_c = _block_diag_mask(nh, kh, nstate, gch)
    for i in range(halves):
        cr = coef_re_ref[i]
        ci = coef_im_ref[i]
        br = bt_re_ref[i * kh:(i + 1) * kh, :]
        bi = bt_im_ref[i * kh:(i + 1) * kh, :]
        wb_re_ref[i] = jnp.where(mask_b, cr * br - ci * bi, 0.0).astype(wb_re_ref.dtype)
        wb_im_ref[i] = jnp.where(mask_b, cr * bi + ci * br, 0.0).astype(wb_im_ref.dtype)
        wc_re_ref[i] = jnp.where(mask_c, ct_re_ref[i * nh:(i + 1) * nh, :], 0.0).astype(wc_re_ref.dtype)
        wc_im_ref[i] = jnp.where(mask_c, -ct_im_ref[i * nh:(i + 1) * nh, :], 0.0).astype(wc_im_ref.dtype)


def _ssm_in(u, wb_re_ref, wb_im_ref, precision=None):
    halves, kh, _ = wb_re_ref.shape
    re, im = [], []
    for i in range(halves):
        ui = u[:, i * kh:(i + 1) * kh]
        re.append(jnp.dot(ui, wb_re_ref[i], preferred_element_type=F32, precision=precision))
        im.append(jnp.dot(ui, wb_im_ref[i], preferred_element_type=F32, precision=precision))
    return jnp.concatenate(re, axis=1), jnp.concatenate(im, axis=1)


def _ssm_out(x_re, x_im, wc_re_ref, wc_im_ref, precision=None):
    halves, nh, _ = wc_re_ref.shape
    ys = []
    for i in range(halves):
        ys.append(jnp.dot(x_re[:, i * nh:(i + 1) * nh], wc_re_ref[i], preferred_element_type=F32, precision=precision)
                  + jnp.dot(x_im[:, i * nh:(i + 1) * nh], wc_im_ref[i], preferred_element_type=F32,
                            precision=precision))
    return jnp.concatenate(ys, axis=1)


def _glu_tail(y, u32, dskip_ref, wglu_ref, bglu_ref, sgb, sw, precision=None):
    y = y + dskip_ref[...] * u32
    za = jax.nn.gelu(y).astype(wglu_ref.dtype)
    glu = jnp.dot(za, wglu_ref[...], preferred_element_type=F32, precision=precision) + bglu_ref[...]
    return glu[:, :sw] * jax.nn.sigmoid(glu[:, sw:]) * sgb


def _s5_prompt_kernel(u_ref, sgb_ref, are_ref, aim_ref, coef_re_ref, coef_im_ref, bt_re_ref, bt_im_ref,
                      ct_re_ref, ct_im_ref, dskip_ref, wglu_ref, bglu_ref,
                      out_ref, fre_ref, fim_ref,
                      wb_re_ref, wb_im_ref, wc_re_ref, wc_im_ref, bu_re_ref, bu_im_ref, xs_re_ref, xs_im_ref,
                      st_re_ref, st_im_ref, *, gch, nstate, sw):
    b = pl.program_id(0)
    t = pl.program_id(1)
    tl = u_ref.shape[1]

    @pl.when((b == 0) & (t == 0))
    def _():
        _build_ssm_weights(coef_re_ref, coef_im_ref, bt_re_ref, bt_im_ref, ct_re_ref, ct_im_ref,
                           wb_re_ref, wb_im_ref, wc_re_ref, wc_im_ref, gch=gch, nstate=nstate)

    @pl.when(t == 0)
    def _():
        st_re_ref[...] = jnp.zeros_like(st_re_ref)
        st_im_ref[...] = jnp.zeros_like(st_im_ref)

    u = u_ref[0]
    bu_re, bu_im = _ssm_in(u, wb_re_ref, wb_im_ref)
    bu_re_ref[...] = bu_re
    bu_im_ref[...] = bu_im
    a_re = are_ref[...]
    a_im = aim_ref[...]

    def step(i, carry):
        xr, xi = carry
        nr = a_re * xr - a_im * xi + bu_re_ref[pl.ds(i, 1), :]
        ni = a_re * xi + a_im * xr + bu_im_ref[pl.ds(i, 1), :]
        xs_re_ref[pl.ds(i, 1), :] = nr
        xs_im_ref[pl.ds(i, 1), :] = ni
        return nr, ni

    xr, xi = lax.fori_loop(0, tl, step, (st_re_ref[...], st_im_ref[...]), unroll=8)
    st_re_ref[...] = xr
    st_im_ref[...] = xi
    fre_ref[0] = xr
    fim_ref[0] = xi

    y = _ssm_out(xs_re_ref[...].astype(BF16), xs_im_ref[...].astype(BF16), wc_re_ref, wc_im_ref)
    out = _glu_tail(y, u.astype(F32), dskip_ref, wglu_ref, bglu_ref, sgb_ref[0].astype(F32), sw)
    out_ref[0] = out.astype(out_ref.dtype)


def _s5_prompt(u, sgb, a_re, a_im, coef_re, coef_im, bt_re, bt_im, ct_re, ct_im, d_skip, w_glu, b_glu, *, tl,
               gch, nstate):
    b, l, sw = u.shape
    ns = a_re.shape[1]
    halves = 2
    kh = sw // halves
    nh = ns // halves
    kern = functools.partial(_s5_prompt_kernel, gch=gch, nstate=nstate, sw=sw)
    tile = pl.BlockSpec((1, tl, sw), lambda bi, t: (bi, t, 0))
    c2 = lambda shape: _const_spec(shape, 2)
    return pl.pallas_call(
        kern,
        grid=(b, l // tl),
        in_specs=[tile, tile, c2((1, ns)), c2((1, ns)), c2((halves, 1, nh)), c2((halves, 1, nh)),
                  c2(bt_re.shape), c2(bt_im.shape), c2(ct_re.shape), c2(ct_im.shape),
                  c2((1, sw)), c2(w_glu.shape), c2((1, 2 * sw))],
        out_specs=[tile,
                   pl.BlockSpec((1, 1, ns), lambda bi, t: (bi, 0, 0)),
                   pl.BlockSpec((1, 1, ns), lambda bi, t: (bi, 0, 0))],
        out_shape=[jax.ShapeDtypeStruct((b, l, sw), BF16),
                   jax.ShapeDtypeStruct((b, 1, ns), F32),
                   jax.ShapeDtypeStruct((b, 1, ns), F32)],
        scratch_shapes=[
            pltpu.VMEM((halves, kh, nh), BF16), pltpu.VMEM((halves, kh, nh), BF16),
            pltpu.VMEM((halves, nh, kh), BF16), pltpu.VMEM((halves, nh, kh), BF16),
            pltpu.VMEM((tl, ns), F32), pltpu.VMEM((tl, ns), F32),
            pltpu.VMEM((tl, ns), F32), pltpu.VMEM((tl, ns), F32),
            pltpu.VMEM((1, ns), F32), pltpu.VMEM((1, ns), F32),
        ],
        compiler_params=_cparams(2),
        name="s5_prompt",
    )(u, sgb, a_re, a_im, coef_re, coef_im, bt_re, bt_im, ct_re, ct_im, d_skip, w_glu, b_glu)


def _s5_sample_kernel(u_ref, sgb_ref, sre_ref, sim_ref, are_ref, aim_ref, coef_re_ref, coef_im_ref,
                      bt_re_ref, bt_im_ref, ct_re_ref, ct_im_ref, dskip_ref, wglu_ref, bglu_ref,
                      out_ref, nre_ref, nim_ref, wb_re_ref, wb_im_ref, wc_re_ref, wc_im_ref, *, gch, nstate, sw):
    _build_ssm_weights(coef_re_ref, coef_im_ref, bt_re_ref, bt_im_ref, ct_re_ref, ct_im_ref,
                       wb_re_ref, wb_im_ref, wc_re_ref, wc_im_ref, gch=gch, nstate=nstate)
    u = u_ref[...]
    bu_re, bu_im = _ssm_in(u, wb_re_ref, wb_im_ref, precision=HIGHEST)
    a_re = are_ref[...]
    a_im = aim_ref[...]
    s_re = sre_ref[...]
    s_im = sim_ref[...]
    x_re = a_re * s_re - a_im * s_im + bu_re
    x_im = a_re * s_im + a_im * s_re + bu_im
    nre_ref[...] = x_re
    nim_ref[...] = x_im
    y = _ssm_out(x_re, x_im, wc_re_ref, wc_im_ref, precision=HIGHEST)
    out_ref[...] = _glu_tail(y, u, dskip_ref, wglu_ref, bglu_ref, sgb_ref[...], sw, precision=HIGHEST)


def _s5_sample(u, sgb, s_re, s_im, a_re, a_im, coef_re, coef_im, bt_re, bt_im, ct_re, ct_im, d_skip, w_glu,
               b_glu, *, gch, nstate):
    n, sw = u.shape
    ns = a_re.shape[1]
    halves = 2
    kh = sw // halves
    nh = ns // halves
    kern = functools.partial(_s5_sample_kernel, gch=gch, nstate=nstate, sw=sw)
    return pl.pallas_call(
        kern,
        out_shape=[jax.ShapeDtypeStruct((n, sw), F32),
                   jax.ShapeDtypeStruct((n, ns), F32),
                   jax.ShapeDtypeStruct((n, ns), F32)],
        scratch_shapes=[
            pltpu.VMEM((halves, kh, nh), F32), pltpu.VMEM((halves, kh, nh), F32),
            pltpu.VMEM((halves, nh, kh), F32), pltpu.VMEM((halves, nh, kh), F32),
        ],
        compiler_params=pltpu.CompilerParams(vmem_limit_bytes=VMEM_LIMIT),
        name="s5_sample",
    )(u, sgb, s_re, s_im, a_re, a_im, coef_re, coef_im, bt_re, bt_im, ct_re, ct_im, d_skip, w_glu, b_glu)


def _proj_out_kernel(x_ref, attn_ref, sga_ref, ssm_ref, sma_ref, smb_ref, wa_ref, wb_ref, wo_ref, o_ref):
    attn_out = (attn_ref[...].astype(F32) * sga_ref[...].astype(F32)).astype(BF16)
    ya = jnp.dot(attn_out, wa_ref[...], preferred_element_type=F32)
    yb = jnp.dot(ssm_ref[...].astype(BF16), wb_ref[...], preferred_element_type=F32)
    merged = sma_ref[...].astype(F32) * ya + smb_ref[...].astype(F32) * yb
    o_ref[...] = x_ref[...] + jnp.dot(merged.astype(BF16), wo_ref[...], preferred_element_type=F32)


def _proj_out(x2, attn, sga, ssm, sma, smb, wa, wb, wo, *, tm):
    t, dm = x2.shape
    aw = attn.shape[1]
    sw = ssm.shape[1]
    row = lambda w: pl.BlockSpec((tm, w), lambda i: (i, 0))
    return pl.pallas_call(
        _proj_out_kernel,
        grid=(t // tm,),
        in_specs=[row(dm), row(aw), row(aw), row(sw), row(dm), row(dm),
                  _const_spec(wa.shape, 1), _const_spec(wb.shape, 1), _const_spec(wo.shape, 1)],
        out_specs=row(dm),
        out_shape=jax.ShapeDtypeStruct((t, dm), F32),
        compiler_params=_cparams(1),
        name="proj_out",
    )(x2, attn, sga, ssm, sma, smb, wa, wb, wo)


def _col_from_row(row, n):
    eye = lax.broadcasted_iota(jnp.int32, (n, n), 0) == lax.broadcasted_iota(jnp.int32, (n, n), 1)
    return jnp.sum(jnp.where(eye, jnp.broadcast_to(row, (n, n)), 0.0), axis=1, keepdims=True)


def _row_from_col(col, n):
    eye = lax.broadcasted_iota(jnp.int32, (n, n), 0) == lax.broadcasted_iota(jnp.int32, (n, n), 1)
    return jnp.sum(jnp.where(eye, jnp.broadcast_to(col, (n, n)), 0.0), axis=0, keepdims=True)


def _sample_attn_kernel(pt_ref, relb_ref, q_ref, qp_ref, knp_ref, vnp_ref, ck_ref, cv_ref, o_ref,
                        kbuf, ksem, qb_ref, lg_ref, tb_ref, selv_ref, sels_ref, selsem, vbuf, vsem,
                        *, n_seq, n_heads, head_dim, ppb, page, n_pages, n_ring, scale):
    step = pl.program_id(0)
    total = n_seq * n_pages
    n_blocks = n_pages // ppb
    past_len = n_pages * page
    n_sel = MOBA_TOPK
    cur = lax.rem(step, 2)
    prev = 1 - cur

    def k_copy(g):
        phys = pt_ref[lax.div(g, n_pages), lax.rem(g, n_pages)]
        slot = lax.rem(g, n_ring)
        return pltpu.make_async_copy(ck_ref.at[0, phys], kbuf.at[slot], ksem.at[slot])

    def sel_copy(par):
        return pltpu.make_async_copy(selv_ref.at[par], sels_ref.at[par], selsem)

    def chosen(par, h, r, i):
        j = sels_ref[par, h, r]
        pgl = jnp.maximum(j, 0) * ppb + i
        return j, pgl, (h * n_sel + r) * ppb + i

    def v_copy(b, par, h, r, i):
        _, pgl, slot = chosen(par, h, r, i)
        return pltpu.make_async_copy(cv_ref.at[0, pt_ref[b, pgl], h], vbuf.at[par, slot], vsem.at[par, slot])

    head_i = lax.broadcasted_iota(jnp.int32, (n_heads, page), 0)
    lane = lax.broadcasted_iota(jnp.int32, (n_heads, LANES), 1)

    @pl.when(step == 0)
    def _():
        for g in range(n_ring - 1):
            k_copy(jnp.int32(g)).start()
        for bk in range(REL_BUCKETS):
            tab = jnp.zeros((n_heads, page), F32)
            for h in range(n_heads):
                tab = jnp.where(head_i == h, relb_ref[bk, h], tab)
            tb_ref[bk] = tab

    @pl.when(step < n_seq)
    def _():
        q = q_ref[0]
        for h in range(n_heads):
            col = _col_from_row(q[:, h * head_dim:(h + 1) * head_dim], head_dim)
            qb_ref[h] = jnp.broadcast_to(col, (head_dim, page))

        def stream(pg, carry):
            g = step * n_pages + pg
            nxt = g + (n_ring - 1)

            @pl.when(nxt < total)
            def _():
                k_copy(nxt).start()

            k_copy(g).wait()
            kt = kbuf[lax.rem(g, n_ring)]
            lg_ref[cur, pg] = jnp.sum(kt * qb_ref[...], axis=1)
            return carry

        lax.fori_loop(0, n_pages, stream, 0)

        sc = jnp.full((n_heads, LANES), NEG_INF, F32)
        for jb in range(n_blocks):
            tot = lg_ref[cur, jb * ppb]
            for i in range(1, ppb):
                tot = tot + lg_ref[cur, jb * ppb + i]
            sc = jnp.where(lane == jb, jnp.sum(tot, axis=1, keepdims=True) * (1.0 / MOBA_BLOCK), sc)
        sel = jnp.full((n_heads, LANES), -1, jnp.int32)
        for r in range(n_sel):
            mx = jnp.max(sc, axis=1, keepdims=True)
            idx = jnp.min(jnp.where(sc == mx, lane, LANES), axis=1, keepdims=True)
            ok = mx > NEG_INF
            sel = jnp.where(lane == r, jnp.where(ok, idx, -1), sel)
            sc = jnp.where((lane == idx) & ok, NEG_INF, sc)
        selv_ref[cur] = sel
        sel_copy(cur).start()

    @pl.when(step >= 1)
    def _():
        b = step - 1
        q = qp_ref[0]
        sel = selv_ref[prev]
        pos = lax.broadcasted_iota(jnp.int32, (n_heads, page), 1)
        qk_new = q * knp_ref[0]
        s_self = jnp.zeros((n_heads, 1), F32)
        for h in range(n_heads):
            dot_h = jnp.sum(qk_new[:, h * head_dim:(h + 1) * head_dim], axis=1, keepdims=True)
            s_self = jnp.where(head_i[:, 0:1] == h, dot_h, s_self)
        s_self = s_self * scale + tb_ref[0][:, 0:1]
        tiles = []
        for r in range(n_sel):
            blk_r = jnp.sum(jnp.where(lane == r, sel, 0), axis=1, keepdims=True)
            for i in range(ppb):
                raw = jnp.zeros((n_heads, page), F32)
                for h in range(n_heads):
                    raw = jnp.where(head_i == h, lg_ref[prev, chosen(prev, h, r, i)[1]][h:h + 1, :], raw)
                bucket = _rel_bucket(past_len - (jnp.maximum(blk_r, 0) * ppb + i) * page - pos)
                bias = jnp.zeros((n_heads, page), F32)
                for bk in range(REL_BUCKETS):
                    bias = jnp.where(bucket == bk, tb_ref[bk], bias)
                tiles.append(jnp.where(blk_r >= 0, raw * scale + bias, NEG_INF))
        m = s_self
        for s in tiles:
            m = jnp.maximum(m, jnp.max(s, axis=1, keepdims=True))
        p_self = jnp.exp(s_self - m)
        den = p_self
        probs = []
        for s in tiles:
            p = jnp.exp(s - m)
            den = den + jnp.sum(p, axis=1, keepdims=True)
            probs.append(p)
        rows = []
        for h in range(n_heads):
            acc = jnp.zeros((head_dim, page), F32)
            for n, p in enumerate(probs):
                j, _, slot = chosen(prev, h, n // ppb, n % ppb)

                @pl.when(j >= 0)
                def _():
                    v_copy(b, prev, h, n // ppb, n % ppb).wait()

                acc = acc + jnp.where(j >= 0, vbuf[prev, slot], 0.0) * p[h:h + 1, :]
            o_row = _row_from_col(jnp.sum(acc, axis=1, keepdims=True), head_dim)
            v_self = vnp_ref[0][:, h * head_dim:(h + 1) * head_dim]
            rows.append((o_row + p_self[h:h + 1, :] * v_self) / den[h:h + 1, :])
        o_ref[0] = jnp.concatenate(rows, axis=1)

    @pl.when(step < n_seq)
    def _():
        sel_copy(cur).wait()
        for h in range(n_heads):
            for r in range(n_sel):
                for i in range(ppb):
                    @pl.when(chosen(cur, h, r, i)[0] >= 0)
                    def _():
                        v_copy(step, cur, h, r, i).start()


def _sample_attention(page_table, rel_bias, q, k_new, v_new, cache_kt, cache_vt, *, ppb, n_ring=32):
    bd, w = q.shape
    _, n_pool, n_heads, head_dim, page = cache_kt.shape
    n_pages = page_table.shape[1]
    assert n_pages % ppb == 0 and n_pages // ppb <= LANES and n_heads * head_dim == w
    n_slots = n_heads * MOBA_TOPK * ppb
    n_ring = min(n_ring, bd * n_pages)
    kern = functools.partial(_sample_attn_kernel, n_seq=bd, n_heads=n_heads, head_dim=head_dim, ppb=ppb, page=page,
                             n_pages=n_pages, n_ring=n_ring, scale=head_dim ** -0.5)
    row_cur = pl.BlockSpec((1, 1, w), lambda s, pt: (jnp.minimum(s, bd - 1), 0, 0))
    row_prev = pl.BlockSpec((1, 1, w), lambda s, pt: (jnp.maximum(s - 1, 0), 0, 0))
    grid_spec = pltpu.PrefetchScalarGridSpec(
        num_scalar_prefetch=1,
        grid=(bd + 1,),
        in_specs=[pl.BlockSpec(memory_space=pltpu.SMEM), row_cur, row_prev, row_prev, row_prev,
                  pl.BlockSpec(memory_space=pl.ANY), pl.BlockSpec(memory_space=pl.ANY)],
        out_specs=row_prev,
        scratch_shapes=[
            pltpu.VMEM((n_ring, n_heads, head_dim, page), F32),
            pltpu.SemaphoreType.DMA((n_ring,)),
            pltpu.VMEM((n_heads, head_dim, page), F32),
            pltpu.VMEM((2, n_pages, n_heads, page), F32),
            pltpu.VMEM((REL_BUCKETS, n_heads, page), F32),
            pltpu.VMEM((2, n_heads, LANES), jnp.int32),
            pltpu.SMEM((2, n_heads, LANES), jnp.int32),
            pltpu.SemaphoreType.DMA(()),
            pltpu.VMEM((2, n_slots, head_dim, page), F32),
            pltpu.SemaphoreType.DMA((2, n_slots)),
        ],
    )
    q3 = q.reshape(bd, 1, w)
    out = pl.pallas_call(
        kern,
        grid_spec=grid_spec,
        out_shape=jax.ShapeDtypeStruct((bd, 1, w), F32),
        compiler_params=_cparams(1),
        name="sample_attn",
    )(page_table, rel_bias, q3, q3, k_new.reshape(bd, 1, w), v_new.reshape(bd, 1, w), cache_kt, cache_vt)
    return out.reshape(bd, w)


def _layer(x_prompt, x_sample, cache_k, cache_v, state_re, state_im, page_table, rel_bias, norm_g, w_in,
           q_norm_g, k_norm_g, lam_re, lam_im, log_dt, b_re, b_im, c_re, c_im, d_skip, w_glu, b_glu,
           w_branch_a, w_branch_b, w_out):
    bsz, seq, dm = x_prompt.shape
    bd, s_new, _ = x_sample.shape
    _, n_pool, page, n_heads, head_dim = cache_k.shape
    n_groups, n_state, gch = b_re.shape
    aw = n_heads * head_dim
    sw = n_groups * gch
    ns = n_groups * n_state
    n_pages = page_table.shape[1]
    past_len = n_pages * page
    ppb = MOBA_BLOCK // page
    assert s_new == 1 and seq % MOBA_BLOCK == 0 and past_len % MOBA_BLOCK == 0
    assert w_in.shape == (dm, 4 * aw + 2 * sw + 2 * dm)

    w_in_bf = w_in.astype(BF16)
    g_row = norm_g.reshape(1, dm)
    qg_t = jnp.tile(q_norm_g.reshape(1, head_dim), (1, n_heads))
    kg_t = jnp.tile(k_norm_g.reshape(1, head_dim), (1, n_heads))
    head_of = jnp.arange(aw) // head_dim
    bd_ones = (head_of[:, None] == head_of[None, :]).astype(BF16)
    halves = 2
    bt_re = jnp.tile(b_re.transpose(0, 2, 1).reshape(sw, n_state), (1, n_groups // halves))
    bt_im = jnp.tile(b_im.transpose(0, 2, 1).reshape(sw, n_state), (1, n_groups // halves))
    ct_re = jnp.tile(c_re.transpose(0, 2, 1).reshape(ns, gch), (1, n_groups // halves))
    ct_im = jnp.tile(c_im.transpose(0, 2, 1).reshape(ns, gch), (1, n_groups // halves))
    dsk = d_skip.reshape(1, sw)
    bgl = b_glu.reshape(1, 2 * sw)

    a_re, a_im, coef_re, coef_im = _ssm_prep(lam_re, lam_im, log_dt)
    a_re = a_re.reshape(1, ns)
    a_im = a_im.reshape(1, ns)
    coef_re = coef_re.reshape(halves, 1, ns // halves)
    coef_im = coef_im.reshape(halves, 1, ns // halves)

    wa_bf = w_branch_a.astype(BF16)
    wb_bf = w_branch_b.astype(BF16)
    wo_bf = w_out.astype(BF16)
    wglu_bf = w_glu.astype(BF16)

    xp2 = x_prompt.reshape(bsz * seq, dm)
    (q_p, kb_p, ksum_p, kt_p, vt_p, vta_p, sga_p, u_p, sgb_p, sma_p, smb_p) = _proj_in(
        xp2, g_row, w_in_bf, qg_t, kg_t, bd_ones, tm=MOBA_BLOCK, aw=aw, sw=sw, head_dim=head_dim,
        q_dtype=BF16, act_dtype=BF16, prompt_batch=bsz)
    n_blocks = seq // MOBA_BLOCK
    bias = _bias_tiles(rel_bias, n_heads, _n_bias_tiles(n_blocks)[0])
    attn_p = _attention(q_p.reshape(bsz, seq, aw), kb_p.reshape(bsz, seq, aw), vta_p,
                        ksum_p.reshape(bsz, n_blocks, aw), bias, head_dim=head_dim, nh=ATTN_HEADS_PER_STEP)
    k_p = kt_p.reshape(bsz, n_heads, head_dim, seq).transpose(0, 3, 1, 2)
    v_p = vt_p.reshape(bsz, n_heads, head_dim, seq).transpose(0, 3, 1, 2)
    ssm_p, fre_p, fim_p = _s5_prompt(u_p.reshape(bsz, seq, sw), sgb_p.reshape(bsz, seq, sw), a_re, a_im, coef_re,
                                     coef_im, bt_re, bt_im, ct_re, ct_im, dsk, wglu_bf, bgl, tl=MOBA_BLOCK,
                                     gch=gch, nstate=n_state)
    y_p = _proj_out(xp2, attn_p.reshape(bsz * seq, aw), sga_p, ssm_p.reshape(bsz * seq, sw), sma_p, smb_p,
                    wa_bf, wb_bf, wo_bf, tm=MOBA_BLOCK)

    xs2 = x_sample.reshape(bd, dm)
    (q_s, k_s, v_s, sga_s, u_s, sgb_s, sma_s, smb_s) = _proj_in(
        xs2, g_row, w_in_bf, qg_t, kg_t, bd_ones, tm=bd, aw=aw, sw=sw, head_dim=head_dim,
        q_dtype=F32, act_dtype=F32)
    cache_kt = cache_k.transpose(0, 1, 3, 4, 2)
    cache_vt = cache_v.transpose(0, 1, 3, 4, 2)
    attn_s = _sample_attention(page_table, rel_bias, q_s, k_s, v_s, cache_kt, cache_vt, ppb=ppb)
    ssm_s, nre_s, nim_s = _s5_sample(u_s, sgb_s, state_re.reshape(bd, ns), state_im.reshape(bd, ns), a_re, a_im,
                                     coef_re, coef_im, bt_re, bt_im, ct_re, ct_im, dsk, w_glu, bgl, gch=gch,
                                     nstate=n_state)
    y_s = _proj_out(xs2, attn_s, sga_s, ssm_s, sma_s, smb_s, wa_bf, wb_bf, wo_bf, tm=bd)

    return (
        y_p.reshape(bsz, seq, dm), y_s.reshape(bd, 1, dm),
        k_p, v_p,
        fre_p.reshape(bsz, n_groups, n_state), fim_p.reshape(bsz, n_groups, n_state),
        k_s.reshape(bd, 1, n_heads, head_dim), v_s.reshape(bd, 1, n_heads, head_dim),
        nre_s.reshape(bd, n_groups, n_state), nim_s.reshape(bd, n_groups, n_state),
    )


def kernel(x_prompt, x_sample, cache_k, cache_v, state_ssm_re, state_ssm_im, page_table, rel_bias, norm_g, w_in,
           q_norm_g, k_norm_g, lam_re, lam_im, log_dt, b_re, b_im, c_re, c_im, d_skip, w_glu, b_glu, w_branch_a,
           w_branch_b, w_out):
    depth = norm_g.shape[0]
    assert depth == 1, "one layer per step"
    outs = _layer(x_prompt, x_sample, cache_k, cache_v, state_ssm_re[0], state_ssm_im[0], page_table,
                  rel_bias, norm_g[0], w_in[0], q_norm_g[0], k_norm_g[0], lam_re[0], lam_im[0], log_dt[0],
                  b_re[0], b_im[0], c_re[0], c_im[0], d_skip[0], w_glu[0], b_glu[0], w_branch_a[0],
                  w_branch_b[0], w_out[0])
    y_p, y_s, k_p, v_p, fre_p, fim_p, k_s, v_s, nre_s, nim_s = outs
    add = lambda a: a[None]
    return (y_p, y_s, add(k_p), add(v_p), add(fre_p), add(fim_p), add(k_s), add(v_s), add(nre_s), add(nim_s))
```

```python
import functools
import math

import jax
import jax.numpy as jnp
from jax import lax
from jax.experimental import pallas as pl
from jax.experimental.pallas import tpu as pltpu

F32 = jnp.float32
BF16 = jnp.bfloat16
HIGHEST = lax.Precision.HIGHEST

MOBA_BLOCK = 256
MOBA_TOPK = 3
REL_BUCKETS = 32
REL_MAX_DIST = 4096
RMS_EPS = 1e-6
LOG2E = math.log2(math.e)
ATTN_HEADS_PER_STEP = 4
LANES = 128
BF16_SUBLANES = 16
VMEM_LIMIT = 56 * 1024 * 1024

NEG_INF = float("-inf")


def _cparams(n_axes, vmem=VMEM_LIMIT):
    return pltpu.CompilerParams(dimension_semantics=("arbitrary",) * n_axes, vmem_limit_bytes=vmem)


def _const_spec(shape, n_grid):
    zeros = (0,) * len(shape)
    if n_grid == 1:
        imap = lambda i: zeros
    elif n_grid == 2:
        imap = lambda i, j: zeros
    else:
        imap = lambda i, j, k: zeros
    return pl.BlockSpec(shape, imap, pipeline_mode=pl.Buffered(1))


def _proj_in_kernel(x_ref, g_ref, w_ref, qg_ref, kg_ref, bd_ref, *out_refs, aw, sw, dm, head_dim, prompt):
    if prompt:
        (q_ref, kb_ref, ksum_ref, kt_ref, vt_ref, vta_ref, sga_ref, u_ref, sgb_ref, sma_ref, smb_ref) = out_refs
    else:
        (q_ref, k_ref, v_ref, sga_ref, u_ref, sgb_ref, sma_ref, smb_ref) = out_refs
    x = x_ref[...]
    ms = jnp.mean(x * x, axis=-1, keepdims=True)
    h = (x * lax.rsqrt(ms + RMS_EPS) * g_ref[...]).astype(BF16)

    def seg(a, b):
        return jnp.dot(h, w_ref[:, a:b], preferred_element_type=F32)

    def headnorm(z, gain):
        ss = jnp.dot((z * z).astype(BF16), bd_ref[...], preferred_element_type=F32)
        return z * lax.rsqrt(ss * (1.0 / head_dim) + RMS_EPS) * gain

    o = 0
    q = headnorm(seg(o, o + aw), qg_ref[...])
    q_ref[...] = q.astype(q_ref.dtype)
    o += aw
    k = headnorm(seg(o, o + aw), kg_ref[...])
    o += aw
    v = seg(o, o + aw)
    o += aw
    if prompt:
        kb_ref[...] = k.astype(BF16)
        ksum_ref[0] = jnp.sum(k, axis=0, keepdims=True)
        kt_ref[0] = k.T
        vt = v.T
        vt_ref[0] = vt
        ones = jnp.ones((BF16_SUBLANES, vt.shape[1]), BF16)
        rows = head_dim + BF16_SUBLANES
        for hd in range(aw // head_dim):
            vta_ref[0, 0, hd * rows:hd * rows + head_dim, :] = vt[hd * head_dim:(hd + 1) * head_dim, :].astype(BF16)
            vta_ref[0, 0, hd * rows + head_dim:(hd + 1) * rows, :] = ones
    else:
        k_ref[...] = k
        v_ref[...] = v
    sga_ref[...] = jax.nn.silu(seg(o, o + aw)).astype(sga_ref.dtype)
    o += aw
    u_ref[...] = seg(o, o + sw).astype(u_ref.dtype)
    o += sw
    sgb_ref[...] = jax.nn.silu(seg(o, o + sw)).astype(sgb_ref.dtype)
    o += sw
    sma_ref[...] = jax.nn.sigmoid(seg(o, o + dm)).astype(sma_ref.dtype)
    o += dm
    smb_ref[...] = jax.nn.sigmoid(seg(o, o + dm)).astype(smb_ref.dtype)


def _proj_in(x2, g, w_bf, qg_t, kg_t, bd, *, tm, aw, sw, head_dim, q_dtype, act_dtype, prompt_batch=None):
    t, dm = x2.shape
    n_tiles = t // tm
    row = lambda w: pl.BlockSpec((tm, w), lambda i: (i, 0))
    prompt = prompt_batch is not None
    if prompt:
        assert tm == MOBA_BLOCK and n_tiles % prompt_batch == 0
        per_seq = n_tiles // prompt_batch
        seq = per_seq * tm
        n_heads = aw // head_dim
        va_rows = n_heads * (head_dim + BF16_SUBLANES)
        t_spec = pl.BlockSpec((1, aw, tm), lambda i: (i // per_seq, 0, i % per_seq))
        out_shape = [
            jax.ShapeDtypeStruct((t, aw), q_dtype),
            jax.ShapeDtypeStruct((t, aw), BF16),
            jax.ShapeDtypeStruct((n_tiles, 1, aw), F32),
            jax.ShapeDtypeStruct((prompt_batch, aw, seq), F32),
            jax.ShapeDtypeStruct((prompt_batch, aw, seq), F32),
            jax.ShapeDtypeStruct((prompt_batch, per_seq, va_rows, tm), BF16),
        ]
        out_specs = [row(aw), row(aw), pl.BlockSpec((1, 1, aw), lambda i: (i, 0, 0)), t_spec, t_spec,
                     pl.BlockSpec((1, 1, va_rows, tm), lambda i: (i // per_seq, i % per_seq, 0, 0))]
    else:
        out_shape = [
            jax.ShapeDtypeStruct((t, aw), q_dtype),
            jax.ShapeDtypeStruct((t, aw), F32),
            jax.ShapeDtypeStruct((t, aw), F32),
        ]
        out_specs = [row(aw)] * 3
    out_shape += [
        jax.ShapeDtypeStruct((t, aw), act_dtype),
        jax.ShapeDtypeStruct((t, sw), act_dtype),
        jax.ShapeDtypeStruct((t, sw), act_dtype),
        jax.ShapeDtypeStruct((t, dm), act_dtype),
        jax.ShapeDtypeStruct((t, dm), act_dtype),
    ]
    out_specs += [row(aw), row(sw), row(sw), row(dm), row(dm)]
    kern = functools.partial(_proj_in_kernel, aw=aw, sw=sw, dm=dm, head_dim=head_dim, prompt=prompt)
    return pl.pallas_call(
        kern,
        grid=(n_tiles,),
        in_specs=[
            row(dm),
            _const_spec((1, dm), 1),
            _const_spec(w_bf.shape, 1),
            _const_spec((1, aw), 1),
            _const_spec((1, aw), 1),
            _const_spec((aw, aw), 1),
        ],
        out_specs=out_specs,
        out_shape=out_shape,
        compiler_params=_cparams(1),
        name="proj_in",
    )(x2, g, w_bf, qg_t, kg_t, bd)


def _rel_bucket(dist):
    n = jnp.maximum(dist, 0)
    max_exact = REL_BUCKETS // 2
    nf = jnp.maximum(n, 1).astype(F32)
    large = max_exact + (
        jnp.log(nf / max_exact) * ((REL_BUCKETS - max_exact) / math.log(REL_MAX_DIST / max_exact))
    ).astype(jnp.int32)
    return jnp.where(n < max_exact, n, jnp.minimum(large, REL_BUCKETS - 1))


def _bias_lookup(bucket, relb_ref, head, lo=0):
    val = jnp.zeros(bucket.shape, F32)
    for b in range(lo, REL_BUCKETS):
        val = jnp.where(bucket == b, relb_ref[b, head], val)
    return val


def _bias_tiles_kernel(relb_ref, o_ref):
    h = pl.program_id(0)
    d = pl.program_id(1)
    key = lax.broadcasted_iota(jnp.int32, (MOBA_BLOCK, MOBA_BLOCK), 0)
    qry = lax.broadcasted_iota(jnp.int32, (MOBA_BLOCK, MOBA_BLOCK), 1)
    dist = d * MOBA_BLOCK + qry - key
    bucket = _rel_bucket(dist)
    exact = REL_BUCKETS // 2
    near = d * MOBA_BLOCK - (MOBA_BLOCK - 1) < exact

    @pl.when(near)
    def _():
        bias = _bias_lookup(bucket, relb_ref, h) * LOG2E
        o_ref[0, 0] = jnp.where(dist >= 0, bias, NEG_INF)

    @pl.when(jnp.logical_not(near))
    def _():
        o_ref[0, 0] = _bias_lookup(bucket, relb_ref, h, lo=exact) * LOG2E


def _n_bias_tiles(n_blocks):
    saturated = -(-(REL_MAX_DIST + MOBA_BLOCK - 1) // MOBA_BLOCK) + 1
    return min(n_blocks, saturated), n_blocks >= saturated


def _bias_tiles(rel_bias, n_heads, n_dist):
    return pl.pallas_call(
        _bias_tiles_kernel,
        grid=(n_heads, n_dist),
        in_specs=[pl.BlockSpec(memory_space=pltpu.SMEM)],
        out_specs=pl.BlockSpec((1, 1, MOBA_BLOCK, MOBA_BLOCK), lambda h, d: (h, d, 0, 0)),
        out_shape=jax.ShapeDtypeStruct((n_heads, n_dist, MOBA_BLOCK, MOBA_BLOCK), F32),
        compiler_params=_cparams(2),
        name="bias_tiles",
    )(rel_bias)


def _select_topk(s):
    rowi = lax.broadcasted_iota(jnp.int32, s.shape, 0)
    sel = jnp.zeros(s.shape, jnp.bool_)
    for _ in range(MOBA_TOPK):
        mx = jnp.max(s, axis=0, keepdims=True)
        cand = jnp.where(s == mx, rowi, s.shape[0])
        idx = jnp.min(cand, axis=0, keepdims=True)
        hit = (rowi == idx) & (mx > NEG_INF)
        sel = sel | hit
        s = jnp.where(hit, NEG_INF, s)
    return sel


def _attn_kernel(q_ref, kb_ref, vta_ref, ksum_ref, bias_ref, o_ref,
                 *scratch, head_dim, n_dist, far_const, scale, nh):
    t = pl.program_id(2)
    blk = MOBA_BLOCK
    n_blocks = vta_ref.shape[1]
    pair = nh * head_dim
    v_rows = head_dim + BF16_SUBLANES
    qts, sels, ms, accs, s_bufs, c_bufs, o_bufs = (scratch[i * nh:(i + 1) * nh] for i in range(7))

    qt = q_ref[0].astype(F32).T
    feat = lax.broadcasted_iota(jnp.int32, (pair, blk), 0)
    means = ksum_ref[0] * (1.0 / blk)
    blk_i = lax.broadcasted_iota(jnp.int32, (n_blocks, blk), 0)
    for h in range(nh):
        in_head = (feat >= h * head_dim) & (feat < (h + 1) * head_dim)
        qt_h = jnp.where(in_head, qt, 0.0)
        qts[h][...] = (qt_h * (scale * LOG2E)).astype(BF16)
        sc = jnp.dot(means, qt_h, preferred_element_type=F32, precision=HIGHEST)
        sc = jnp.where(blk_i < t, sc, NEG_INF)
        sels[h][...] = (_select_topk(sc) | (blk_i == t)).astype(F32)
        ms[h][...] = jnp.full(ms[h].shape, NEG_INF, F32)
        accs[h][...] = jnp.zeros(accs[h].shape, F32)

    def block_of(n):
        return jnp.where(n == 0, t, jnp.where(n <= t, n - 1, 0))

    def stage_a(n, slot, far):
        j = block_of(n)
        kj = kb_ref[0, pl.ds(pl.multiple_of(j * blk, blk), blk), :]
        tile = jnp.minimum(t - j, n_dist - 1)
        for h in range(nh):
            s = jnp.dot(kj, qts[h][...], preferred_element_type=F32)
            if far:
                off = bias_ref[h, n_dist - 1, 0:1, :]
            else:
                s = s + bias_ref[h, tile]
                off = jnp.zeros((1, blk), F32)
            s_bufs[h][slot] = s.astype(BF16)
            c_bufs[h][slot] = jnp.max(s, axis=0, keepdims=True) + off
            o_bufs[h][slot] = off

    def stage_b(n, slot):
        j = block_of(n)
        for h in range(nh):
            picked = jnp.where(n <= t, sels[h][pl.ds(j, 1), :], 0.0)
            m_old = ms[h][...]
            off = o_bufs[h][slot]
            shift = (jnp.maximum(m_old, c_bufs[h][slot]) - off).astype(BF16)
            m_blk = shift.astype(F32) + off
            m_new = jnp.where(picked > 0.0, m_blk, m_old)
            p = jnp.exp2(s_bufs[h][slot] - shift)
            upd = jnp.dot(vta_ref[0, j, h * v_rows:(h + 1) * v_rows, :], p, preferred_element_type=F32)
            accs[h][...] = accs[h][...] * jnp.exp2(m_old - m_new) + upd * picked
            ms[h][...] = m_new

    stage_a(jnp.int32(0), 0, False)
    stage_a(jnp.int32(1), 1, False)

    n_iter = (t + 2) // 2

    def two_blocks(i, carry):
        j = 2 * i
        both_far = (t - (j + 2) >= n_dist - 1) if far_const else False
        for parity in range(2):
            cur, nxt = 2 * parity, 2 - 2 * parity
            mine = (i % 2 == parity)
            for far in ((False, True) if far_const else (False,)):
                @pl.when(mine & (i < n_iter - 1) & (both_far == far))
                def _():
                    stage_a(j + 2, nxt, far)
                    stage_a(j + 3, nxt + 1, far)
                    stage_b(j, cur)
                    stage_b(j + 1, cur + 1)

            @pl.when(mine & (i == n_iter - 1))
            def _():
                stage_b(j, cur)
                stage_b(j + 1, cur + 1)

        return carry

    lax.fori_loop(0, n_iter, two_blocks, 0)

    outs = [accs[h][0:head_dim, :] / accs[h][head_dim:head_dim + 1, :] for h in range(nh)]
    o_ref[0] = jnp.concatenate(outs, axis=0).T.astype(o_ref.dtype)


def _attention(q_bf, k_bf, vt_aug, ksum, bias, *, head_dim, nh):
    b, l, aw = q_bf.shape
    width = nh * head_dim
    assert width % LANES == 0 and aw % width == 0 and l % MOBA_BLOCK == 0
    n_groups = aw // width
    n_blocks = l // MOBA_BLOCK
    n_dist, far_const = _n_bias_tiles(n_blocks)
    assert bias.shape[1] == n_dist
    v_rows = head_dim + BF16_SUBLANES
    assert vt_aug.shape == (b, n_blocks, nh * n_groups * v_rows, MOBA_BLOCK)
    kern = functools.partial(_attn_kernel, head_dim=head_dim, n_dist=n_dist, far_const=far_const,
                             scale=head_dim ** -0.5, nh=nh)
    one = pl.Buffered(1)
    return pl.pallas_call(
        kern,
        grid=(b, n_groups, n_blocks),
        in_specs=[
            pl.BlockSpec((1, MOBA_BLOCK, width), lambda bi, hg, t: (bi, t, hg)),
            pl.BlockSpec((1, l, width), lambda bi, hg, t: (bi, 0, hg), pipeline_mode=one),
            pl.BlockSpec((1, n_blocks, nh * v_rows, MOBA_BLOCK), lambda bi, hg, t: (bi, 0, hg, 0), pipeline_mode=one),
            pl.BlockSpec((1, n_blocks, width), lambda bi, hg, t: (bi, 0, hg), pipeline_mode=one),
            pl.BlockSpec((nh, n_dist, MOBA_BLOCK, MOBA_BLOCK), lambda bi, hg, t: (hg, 0, 0, 0), pipeline_mode=one),
        ],
        out_specs=pl.BlockSpec((1, MOBA_BLOCK, width), lambda bi, hg, t: (bi, t, hg)),
        out_shape=jax.ShapeDtypeStruct((b, l, aw), BF16),
        scratch_shapes=(
            [pltpu.VMEM((width, MOBA_BLOCK), BF16)] * nh
            + [pltpu.VMEM((n_blocks, MOBA_BLOCK), F32)] * nh
            + [pltpu.VMEM((1, MOBA_BLOCK), F32)] * nh
            + [pltpu.VMEM((v_rows, MOBA_BLOCK), F32)] * nh
            + [pltpu.VMEM((4, MOBA_BLOCK, MOBA_BLOCK), BF16)] * nh
            + [pltpu.VMEM((4, 1, MOBA_BLOCK), F32)] * nh
            + [pltpu.VMEM((4, 1, MOBA_BLOCK), F32)] * nh
        ),
        compiler_params=_cparams(3),
        name="moba_attn",
    )(q_bf, k_bf, vt_aug, ksum, bias)


def _ssm_prep_kernel(lre_ref, lim_ref, ldt_ref, are_ref, aim_ref, cre_ref, cim_ref):
    lre = lre_ref[...]
    lim = lim_ref[...]
    dt = jnp.exp(ldt_ref[...])
    mag = jnp.exp(lre * dt)
    a_re = mag * jnp.cos(lim * dt)
    a_im = mag * jnp.sin(lim * dt)
    are_ref[...] = a_re
    aim_ref[...] = a_im
    nr = a_re - 1.0
    den = lre * lre + lim * lim
    cre_ref[...] = (nr * lre + a_im * lim) / den
    cim_ref[...] = (a_im * lre - nr * lim) / den


def _ssm_prep(lam_re, lam_im, log_dt):
    g, p = lam_re.shape
    shp = jax.ShapeDtypeStruct((g, p), F32)
    return pl.pallas_call(_ssm_prep_kernel, out_shape=[shp] * 4, name="ssm_prep")(
        lam_re, lam_im, log_dt.reshape(g, 1))


def _block_diag_mask(rows, cols, row_group, col_group):
    r = lax.broadcasted_iota(jnp.int32, (rows, cols), 0) // row_group
    c = lax.broadcasted_iota(jnp.int32, (rows, cols), 1) // col_group
    return r == c


def _build_ssm_weights(coef_re_ref, coef_im_ref, bt_re_ref, bt_im_ref, ct_re_ref, ct_im_ref,
                       wb_re_ref, wb_im_ref, wc_re_ref, wc_im_ref, *, gch, nstate):
    halves, kh, nh = wb_re_ref.shape
    mask_b = _block_diag_mask(kh, nh, gch, nstate)
    mask_c = _block_diag_mask(nh, kh, nstate, gch)
    for i in range(halves):
        cr = coef_re_ref[i]
        ci = coef_im_ref[i]
        br = bt_re_ref[i * kh:(i + 1) * kh, :]
        bi = bt_im_ref[i * kh:(i + 1) * kh, :]
        wb_re_ref[i] = jnp.where(mask_b, cr * br - ci * bi, 0.0).astype(wb_re_ref.dtype)
        wb_im_ref[i] = jnp.where(mask_b, cr * bi + ci * br, 0.0).astype(wb_im_ref.dtype)
        wc_re_ref[i] = jnp.where(mask_c, ct_re_ref[i * nh:(i + 1) * nh, :], 0.0).astype(wc_re_ref.dtype)
        wc_im_ref[i] = jnp.where(mask_c, -ct_im_ref[i * nh:(i + 1) * nh, :], 0.0).astype(wc_im_ref.dtype)


def _ssm_in(u, wb_re_ref, wb_im_ref, precision=None):
    halves, kh, _ = wb_re_ref.shape
    re, im = [], []
    for i in range(halves):
        ui = u[:, i * kh:(i + 1) * kh]
        re.append(jnp.dot(ui, wb_re_ref[i], preferred_element_type=F32, precision=precision))
        im.append(jnp.dot(ui, wb_im_ref[i], preferred_element_type=F32, precision=precision))
    return jnp.concatenate(re, axis=1), jnp.concatenate(im, axis=1)


def _ssm_out(x_re, x_im, wc_re_ref, wc_im_ref, precision=None):
    halves, nh, _ = wc_re_ref.shape
    ys = []
    for i in range(halves):
        ys.append(jnp.dot(x_re[:, i * nh:(i + 1) * nh], wc_re_ref[i], preferred_element_type=F32, precision=precision)
                  + jnp.dot(x_im[:, i * nh:(i + 1) * nh], wc_im_ref[i], preferred_element_type=F32,
                            precision=precision))
    return jnp.concatenate(ys, axis=1)


def _glu_tail(y, u32, dskip_ref, wglu_ref, bglu_ref, sgb, sw, precision=None):
    y = y + dskip_ref[...] * u32
    za = jax.nn.gelu(y).astype(wglu_ref.dtype)
    glu = jnp.dot(za, wglu_ref[...], preferred_element_type=F32, precision=precision) + bglu_ref[...]
    return glu[:, :sw] * jax.nn.sigmoid(glu[:, sw:]) * sgb


def _s5_prompt_kernel(u_ref, sgb_ref, are_ref, aim_ref, coef_re_ref, coef_im_ref, bt_re_ref, bt_im_ref,
                      ct_re_ref, ct_im_ref, dskip_ref, wglu_ref, bglu_ref,
                      out_ref, fre_ref, fim_ref,
                      wb_re_ref, wb_im_ref, wc_re_ref, wc_im_ref, bu_re_ref, bu_im_ref, xs_re_ref, xs_im_ref,
                      st_re_ref, st_im_ref, *, gch, nstate, sw):
    b = pl.program_id(0)
    t = pl.program_id(1)
    tl = u_ref.shape[1]

    @pl.when((b == 0) & (t == 0))
    def _():
        _build_ssm_weights(coef_re_ref, coef_im_ref, bt_re_ref, bt_im_ref, ct_re_ref, ct_im_ref,
                           wb_re_ref, wb_im_ref, wc_re_ref, wc_im_ref, gch=gch, nstate=nstate)

    @pl.when(t == 0)
    def _():
        st_re_ref[...] = jnp.zeros_like(st_re_ref)
        st_im_ref[...] = jnp.zeros_like(st_im_ref)

    u = u_ref[0]
    bu_re, bu_im = _ssm_in(u, wb_re_ref, wb_im_ref)
    bu_re_ref[...] = bu_re
    bu_im_ref[...] = bu_im
    a_re = are_ref[...]
    a_im = aim_ref[...]

    def step(i, carry):
        xr, xi = carry
        nr = a_re * xr - a_im * xi + bu_re_ref[pl.ds(i, 1), :]
        ni = a_re * xi + a_im * xr + bu_im_ref[pl.ds(i, 1), :]
        xs_re_ref[pl.ds(i, 1), :] = nr
        xs_im_ref[pl.ds(i, 1), :] = ni
        return nr, ni

    xr, xi = lax.fori_loop(0, tl, step, (st_re_ref[...], st_im_ref[...]), unroll=8)
    st_re_ref[...] = xr
    st_im_ref[...] = xi
    fre_ref[0] = xr
    fim_ref[0] = xi

    y = _ssm_out(xs_re_ref[...].astype(BF16), xs_im_ref[...].astype(BF16), wc_re_ref, wc_im_ref)
    out = _glu_tail(y, u.astype(F32), dskip_ref, wglu_ref, bglu_ref, sgb_ref[0].astype(F32), sw)
    out_ref[0] = out.astype(out_ref.dtype)


def _s5_prompt(u, sgb, a_re, a_im, coef_re, coef_im, bt_re, bt_im, ct_re, ct_im, d_skip, w_glu, b_glu, *, tl,
               gch, nstate):
    b, l, sw = u.shape
    ns = a_re.shape[1]
    halves = 2
    kh = sw // halves
    nh = ns // halves
    kern = functools.partial(_s5_prompt_kernel, gch=gch, nstate=nstate, sw=sw)
    tile = pl.BlockSpec((1, tl, sw), lambda bi, t: (bi, t, 0))
    c2 = lambda shape: _const_spec(shape, 2)
    return pl.pallas_call(
        kern,
        grid=(b, l // tl),
        in_specs=[tile, tile, c2((1, ns)), c2((1, ns)), c2((halves, 1, nh)), c2((halves, 1, nh)),
                  c2(bt_re.shape), c2(bt_im.shape), c2(ct_re.shape), c2(ct_im.shape),
                  c2((1, sw)), c2(w_glu.shape), c2((1, 2 * sw))],
        out_specs=[tile,
                   pl.BlockSpec((1, 1, ns), lambda bi, t: (bi, 0, 0)),
                   pl.BlockSpec((1, 1, ns), lambda bi, t: (bi, 0, 0))],
        out_shape=[jax.ShapeDtypeStruct((b, l, sw), BF16),
                   jax.ShapeDtypeStruct((b, 1, ns), F32),
                   jax.ShapeDtypeStruct((b, 1, ns), F32)],
        scratch_shapes=[
            pltpu.VMEM((halves, kh, nh), BF16), pltpu.VMEM((halves, kh, nh), BF16),
            pltpu.VMEM((halves, nh, kh), BF16), pltpu.VMEM((halves, nh, kh), BF16),
            pltpu.VMEM((tl, ns), F32), pltpu.VMEM((tl, ns), F32),
            pltpu.VMEM((tl, ns), F32), pltpu.VMEM((tl, ns), F32),
            pltpu.VMEM((1, ns), F32), pltpu.VMEM((1, ns), F32),
        ],
        compiler_params=_cparams(2),
        name="s5_prompt",
    )(u, sgb, a_re, a_im, coef_re, coef_im, bt_re, bt_im, ct_re, ct_im, d_skip, w_glu, b_glu)


def _s5_sample_kernel(u_ref, sgb_ref, sre_ref, sim_ref, are_ref, aim_ref, coef_re_ref, coef_im_ref,
                      bt_re_ref, bt_im_ref, ct_re_ref, ct_im_ref, dskip_ref, wglu_ref, bglu_ref,
                      out_ref, nre_ref, nim_ref, wb_re_ref, wb_im_ref, wc_re_ref, wc_im_ref, *, gch, nstate, sw):
    _build_ssm_weights(coef_re_ref, coef_im_ref, bt_re_ref, bt_im_ref, ct_re_ref, ct_im_ref,
                       wb_re_ref, wb_im_ref, wc_re_ref, wc_im_ref, gch=gch, nstate=nstate)
    u = u_ref[...]
    bu_re, bu_im = _ssm_in(u, wb_re_ref, wb_im_ref, precision=HIGHEST)
    a_re = are_ref[...]
    a_im = aim_ref[...]
    s_re = sre_ref[...]
    s_im = sim_ref[...]
    x_re = a_re * s_re - a_im * s_im + bu_re
    x_im = a_re * s_im + a_im * s_re + bu_im
    nre_ref[...] = x_re
    nim_ref[...] = x_im
    y = _ssm_out(x_re, x_im, wc_re_ref, wc_im_ref, precision=HIGHEST)
    out_ref[...] = _glu_tail(y, u, dskip_ref, wglu_ref, bglu_ref, sgb_ref[...], sw, precision=HIGHEST)


def _s5_sample(u, sgb, s_re, s_im, a_re, a_im, coef_re, coef_im, bt_re, bt_im, ct_re, ct_im, d_skip, w_glu,
               b_glu, *, gch, nstate):
    n, sw = u.shape
    ns = a_re.shape[1]
    halves = 2
    kh = sw // halves
    nh = ns // halves
    kern = functools.partial(_s5_sample_kernel, gch=gch, nstate=nstate, sw=sw)
    return pl.pallas_call(
        kern,
        out_shape=[jax.ShapeDtypeStruct((n, sw), F32),
                   jax.ShapeDtypeStruct((n, ns), F32),
                   jax.ShapeDtypeStruct((n, ns), F32)],
        scratch_shapes=[
            pltpu.VMEM((halves, kh, nh), F32), pltpu.VMEM((halves, kh, nh), F32),
            pltpu.VMEM((halves, nh, kh), F32), pltpu.VMEM((halves, nh, kh), F32),
        ],
        compiler_params=pltpu.CompilerParams(vmem_limit_bytes=VMEM_LIMIT),
        name="s5_sample",
    )(u, sgb, s_re, s_im, a_re, a_im, coef_re, coef_im, bt_re, bt_im, ct_re, ct_im, d_skip, w_glu, b_glu)


def _proj_out_kernel(x_ref, attn_ref, sga_ref, ssm_ref, sma_ref, smb_ref, wa_ref, wb_ref, wo_ref, o_ref):
    attn_out = (attn_ref[...].astype(F32) * sga_ref[...].astype(F32)).astype(BF16)
    ya = jnp.dot(attn_out, wa_ref[...], preferred_element_type=F32)
    yb = jnp.dot(ssm_ref[...].astype(BF16), wb_ref[...], preferred_element_type=F32)
    merged = sma_ref[...].astype(F32) * ya + smb_ref[...].astype(F32) * yb
    o_ref[...] = x_ref[...] + jnp.dot(merged.astype(BF16), wo_ref[...], preferred_element_type=F32)


def _proj_out(x2, attn, sga, ssm, sma, smb, wa, wb, wo, *, tm):
    t, dm = x2.shape
    aw = attn.shape[1]
    sw = ssm.shape[1]
    row = lambda w: pl.BlockSpec((tm, w), lambda i: (i, 0))
    return pl.pallas_call(
        _proj_out_kernel,
        grid=(t // tm,),
        in_specs=[row(dm), row(aw), row(aw), row(sw), row(dm), row(dm),
                  _const_spec(wa.shape, 1), _const_spec(wb.shape, 1), _const_spec(wo.shape, 1)],
        out_specs=row(dm),
        out_shape=jax.ShapeDtypeStruct((t, dm), F32),
        compiler_params=_cparams(1),
        name="proj_out",
    )(x2, attn, sga, ssm, sma, smb, wa, wb, wo)


def _col_from_row(row, n):
    eye = lax.broadcasted_iota(jnp.int32, (n, n), 0) == lax.broadcasted_iota(jnp.int32, (n, n), 1)
    return jnp.sum(jnp.where(eye, jnp.broadcast_to(row, (n, n)), 0.0), axis=1, keepdims=True)


def _row_from_col(col, n):
    eye = lax.broadcasted_iota(jnp.int32, (n, n), 0) == lax.broadcasted_iota(jnp.int32, (n, n), 1)
    return jnp.sum(jnp.where(eye, jnp.broadcast_to(col, (n, n)), 0.0), axis=0, keepdims=True)


def _sample_attn_kernel(pt_ref, relb_ref, q_ref, qp_ref, knp_ref, vnp_ref, ck_ref, cv_ref, o_ref,
                        kbuf, ksem, qb_ref, lg_ref, tb_ref, selv_ref, sels_ref, selsem, vbuf, vsem,
                        *, n_seq, n_heads, head_dim, ppb, page, n_pages, n_ring, group, scale):
    step = pl.program_id(0)
    total = n_seq * n_pages
    n_blocks = n_pages // ppb
    past_len = n_pages * page
    n_sel = MOBA_TOPK
    cur = lax.rem(step, 2)
    prev = 1 - cur

    def k_copy(g):
        phys = pt_ref[lax.div(g, n_pages), lax.rem(g, n_pages)]
        slot = lax.rem(g, n_ring)
        return pltpu.make_async_copy(ck_ref.at[0, phys], kbuf.at[slot], ksem.at[slot])

    def sel_copy(par):
        return pltpu.make_async_copy(selv_ref.at[par], sels_ref.at[par], selsem)

    def chosen(par, h, r, i):
        j = sels_ref[par, h, r]
        pgl = jnp.maximum(j, 0) * ppb + i
        return j, pgl, (h * n_sel + r) * ppb + i

    def v_copy(b, par, h, r, i):
        _, pgl, slot = chosen(par, h, r, i)
        return pltpu.make_async_copy(cv_ref.at[0, pt_ref[b, pgl], h], vbuf.at[par, slot], vsem.at[par, slot])

    head_i = lax.broadcasted_iota(jnp.int32, (n_heads, page), 0)
    lane = lax.broadcasted_iota(jnp.int32, (n_heads, LANES), 1)

    @pl.when(step == 0)
    def _():
        for g in range(n_ring - group):
            k_copy(jnp.int32(g)).start()
        for bk in range(REL_BUCKETS):
            tab = jnp.zeros((n_heads, page), F32)
            for h in range(n_heads):
                tab = jnp.where(head_i == h, relb_ref[bk, h], tab)
            tb_ref[bk] = tab

    @pl.when(step < n_seq)
    def _():
        q = q_ref[0]
        for h in range(n_heads):
            col = _col_from_row(q[:, h * head_dim:(h + 1) * head_dim], head_dim)
            qb_ref[h] = jnp.broadcast_to(col, (head_dim, page))

        def stream(i, carry):
            g0 = step * n_pages + i * group
            for u in range(group):
                nxt = g0 + u + (n_ring - group)

                @pl.when(nxt < total)
                def _():
                    k_copy(nxt).start()

            for u in range(group):
                k_copy(g0 + u).wait()
            slot0 = lax.rem(g0, n_ring)
            rows = [[] for _ in range(group)]
            for h in range(n_heads):
                qh = qb_ref[h]
                for u in range(group):
                    rows[u].append(jnp.sum(kbuf[slot0 + u, h] * qh, axis=0, keepdims=True))
            for u in range(group):
                lg_ref[cur, i * group + u] = jnp.concatenate(rows[u], axis=0)
            return carry

        lax.fori_loop(0, n_pages // group, stream, 0)

        sc = jnp.full((n_heads, LANES), NEG_INF, F32)
        for jb in range(n_blocks):
            tot = lg_ref[cur, jb * ppb]
            for i in range(1, ppb):
                tot = tot + lg_ref[cur, jb * ppb + i]
            sc = jnp.where(lane == jb, jnp.sum(tot, axis=1, keepdims=True) * (1.0 / MOBA_BLOCK), sc)
        sel = jnp.full((n_heads, LANES), -1, jnp.int32)
        for r in range(n_sel):
            mx = jnp.max(sc, axis=1, keepdims=True)
            idx = jnp.min(jnp.where(sc == mx, lane, LANES), axis=1, keepdims=True)
            ok = mx > NEG_INF
            sel = jnp.where(lane == r, jnp.where(ok, idx, -1), sel)
            sc = jnp.where((lane == idx) & ok, NEG_INF, sc)
        selv_ref[cur] = sel
        sel_copy(cur).start()

    @pl.when(step >= 1)
    def _():
        b = step - 1
        q = qp_ref[0]
        sel = selv_ref[prev]
        pos = lax.broadcasted_iota(jnp.int32, (n_heads, page), 1)
        qk_new = q * knp_ref[0]
        s_self = jnp.zeros((n_heads, 1), F32)
        for h in range(n_heads):
            dot_h = jnp.sum(qk_new[:, h * head_dim:(h + 1) * head_dim], axis=1, keepdims=True)
            s_self = jnp.where(head_i[:, 0:1] == h, dot_h, s_self)
        s_self = s_self * scale + tb_ref[0][:, 0:1]
        tiles = []
        for r in range(n_sel):
            blk_r = jnp.sum(jnp.where(lane == r, sel, 0), axis=1, keepdims=True)
            for i in range(ppb):
                raw = jnp.zeros((n_heads, page), F32)
                for h in range(n_heads):
                    raw = jnp.where(head_i == h, lg_ref[prev, chosen(prev, h, r, i)[1]][h:h + 1, :], raw)
                bucket = _rel_bucket(past_len - (jnp.maximum(blk_r, 0) * ppb + i) * page - pos)
                bias = jnp.zeros((n_heads, page), F32)
                for bk in range(REL_BUCKETS):
                    bias = jnp.where(bucket == bk, tb_ref[bk], bias)
                tiles.append(jnp.where(blk_r >= 0, raw * scale + bias, NEG_INF))
        m = s_self
        for s in tiles:
            m = jnp.maximum(m, jnp.max(s, axis=1, keepdims=True))
        p_self = jnp.exp(s_self - m)
        den = p_self
        probs = []
        for s in tiles:
            p = jnp.exp(s - m)
            den = den + jnp.sum(p, axis=1, keepdims=True)
            probs.append(p)
        rows = []
        for h in range(n_heads):
            acc = jnp.zeros((head_dim, page), F32)
            for n, p in enumerate(probs):
                j, _, slot = chosen(prev, h, n // ppb, n % ppb)

                @pl.when(j >= 0)
                def _():
                    v_copy(b, prev, h, n // ppb, n % ppb).wait()

                acc = acc + jnp.where(j >= 0, vbuf[prev, slot], 0.0) * p[h:h + 1, :]
            o_row = _row_from_col(jnp.sum(acc, axis=1, keepdims=True), head_dim)
            v_self = vnp_ref[0][:, h * head_dim:(h + 1) * head_dim]
            rows.append((o_row + p_self[h:h + 1, :] * v_self) / den[h:h + 1, :])
        o_ref[0] = jnp.concatenate(rows, axis=1)

    @pl.when(step < n_seq)
    def _():
        sel_copy(cur).wait()
        for h in range(n_heads):
            for r in range(n_sel):
                for i in range(ppb):
                    @pl.when(chosen(cur, h, r, i)[0] >= 0)
                    def _():
                        v_copy(step, cur, h, r, i).start()


def _sample_attention(page_table, rel_bias, q, k_new, v_new, cache_kt, cache_vt, *, ppb, n_ring=32):
    bd, w = q.shape
    _, n_pool, n_heads, head_dim, page = cache_kt.shape
    n_pages = page_table.shape[1]
    assert n_pages % ppb == 0 and n_pages // ppb <= LANES and n_heads * head_dim == w
    n_slots = n_heads * MOBA_TOPK * ppb
    group = math.gcd(n_pages, 4)
    n_ring = min(n_ring, bd * n_pages)
    assert n_ring % group == 0 and n_ring > group
    kern = functools.partial(_sample_attn_kernel, n_seq=bd, n_heads=n_heads, head_dim=head_dim, ppb=ppb, page=page,
                             n_pages=n_pages, n_ring=n_ring, group=group, scale=head_dim ** -0.5)
    row_cur = pl.BlockSpec((1, 1, w), lambda s, pt: (jnp.minimum(s, bd - 1), 0, 0))
    row_prev = pl.BlockSpec((1, 1, w), lambda s, pt: (jnp.maximum(s - 1, 0), 0, 0))
    grid_spec = pltpu.PrefetchScalarGridSpec(
        num_scalar_prefetch=1,
        grid=(bd + 1,),
        in_specs=[pl.BlockSpec(memory_space=pltpu.SMEM), row_cur, row_prev, row_prev, row_prev,
                  pl.BlockSpec(memory_space=pl.ANY), pl.BlockSpec(memory_space=pl.ANY)],
        out_specs=row_prev,
        scratch_shapes=[
            pltpu.VMEM((n_ring, n_heads, head_dim, page), F32),
            pltpu.SemaphoreType.DMA((n_ring,)),
            pltpu.VMEM((n_heads, head_dim, page), F32),
            pltpu.VMEM((2, n_pages, n_heads, page), F32),
            pltpu.VMEM((REL_BUCKETS, n_heads, page), F32),
            pltpu.VMEM((2, n_heads, LANES), jnp.int32),
            pltpu.SMEM((2, n_heads, LANES), jnp.int32),
            pltpu.SemaphoreType.DMA(()),
            pltpu.VMEM((2, n_slots, head_dim, page), F32),
            pltpu.SemaphoreType.DMA((2, n_slots)),
        ],
    )
    q3 = q.reshape(bd, 1, w)
    out = pl.pallas_call(
        kern,
        grid_spec=grid_spec,
        out_shape=jax.ShapeDtypeStruct((bd, 1, w), F32),
        compiler_params=_cparams(1),
        name="sample_attn",
    )(page_table, rel_bias, q3, q3, k_new.reshape(bd, 1, w), v_new.reshape(bd, 1, w), cache_kt, cache_vt)
    return out.reshape(bd, w)


def _layer(x_prompt, x_sample, cache_k, cache_v, state_re, state_im, page_table, rel_bias, norm_g, w_in,
           q_norm_g, k_norm_g, lam_re, lam_im, log_dt, b_re, b_im, c_re, c_im, d_skip, w_glu, b_glu,
           w_branch_a, w_branch_b, w_out):
    bsz, seq, dm = x_prompt.shape
    bd, s_new, _ = x_sample.shape
    _, n_pool, page, n_heads, head_dim = cache_k.shape
    n_groups, n_state, gch = b_re.shape
    aw = n_heads * head_dim
    sw = n_groups * gch
    ns = n_groups * n_state
    n_pages = page_table.shape[1]
    past_len = n_pages * page
    ppb = MOBA_BLOCK // page
    assert s_new == 1 and seq % MOBA_BLOCK == 0 and past_len % MOBA_BLOCK == 0
    assert w_in.shape == (dm, 4 * aw + 2 * sw + 2 * dm)

    w_in_bf = w_in.astype(BF16)
    g_row = norm_g.reshape(1, dm)
    qg_t = jnp.tile(q_norm_g.reshape(1, head_dim), (1, n_heads))
    kg_t = jnp.tile(k_norm_g.reshape(1, head_dim), (1, n_heads))
    head_of = jnp.arange(aw) // head_dim
    bd_ones = (head_of[:, None] == head_of[None, :]).astype(BF16)
    halves = 2
    bt_re = jnp.tile(b_re.transpose(0, 2, 1).reshape(sw, n_state), (1, n_groups // halves))
    bt_im = jnp.tile(b_im.transpose(0, 2, 1).reshape(sw, n_state), (1, n_groups // halves))
    ct_re = jnp.tile(c_re.transpose(0, 2, 1).reshape(ns, gch), (1, n_groups // halves))
    ct_im = jnp.tile(c_im.transpose(0, 2, 1).reshape(ns, gch), (1, n_groups // halves))
    dsk = d_skip.reshape(1, sw)
    bgl = b_glu.reshape(1, 2 * sw)

    a_re, a_im, coef_re, coef_im = _ssm_prep(lam_re, lam_im, log_dt)
    a_re = a_re.reshape(1, ns)
    a_im = a_im.reshape(1, ns)
    coef_re = coef_re.reshape(halves, 1, ns // halves)
    coef_im = coef_im.reshape(halves, 1, ns // halves)

    wa_bf = w_branch_a.astype(BF16)
    wb_bf = w_branch_b.astype(BF16)
    wo_bf = w_out.astype(BF16)
    wglu_bf = w_glu.astype(BF16)

    xp2 = x_prompt.reshape(bsz * seq, dm)
    (q_p, kb_p, ksum_p, kt_p, vt_p, vta_p, sga_p, u_p, sgb_p, sma_p, smb_p) = _proj_in(
        xp2, g_row, w_in_bf, qg_t, kg_t, bd_ones, tm=MOBA_BLOCK, aw=aw, sw=sw, head_dim=head_dim,
        q_dtype=BF16, act_dtype=BF16, prompt_batch=bsz)
    n_blocks = seq // MOBA_BLOCK
    bias = _bias_tiles(rel_bias, n_heads, _n_bias_tiles(n_blocks)[0])
    attn_p = _attention(q_p.reshape(bsz, seq, aw), kb_p.reshape(bsz, seq, aw), vta_p,
                        ksum_p.reshape(bsz, n_blocks, aw), bias, head_dim=head_dim, nh=ATTN_HEADS_PER_STEP)
    k_p = kt_p.reshape(bsz, n_heads, head_dim, seq).transpose(0, 3, 1, 2)
    v_p = vt_p.reshape(bsz, n_heads, head_dim, seq).transpose(0, 3, 1, 2)
    ssm_p, fre_p, fim_p = _s5_prompt(u_p.reshape(bsz, seq, sw), sgb_p.reshape(bsz, seq, sw), a_re, a_im, coef_re,
                                     coef_im, bt_re, bt_im, ct_re, ct_im, dsk, wglu_bf, bgl, tl=MOBA_BLOCK,
                                     gch=gch, nstate=n_state)
    y_p = _proj_out(xp2, attn_p.reshape(bsz * seq, aw), sga_p, ssm_p.reshape(bsz * seq, sw), sma_p, smb_p,
                    wa_bf, wb_bf, wo_bf, tm=MOBA_BLOCK)

    xs2 = x_sample.reshape(bd, dm)
    (q_s, k_s, v_s, sga_s, u_s, sgb_s, sma_s, smb_s) = _proj_in(
        xs2, g_row, w_in_bf, qg_t, kg_t, bd_ones, tm=bd, aw=aw, sw=sw, head_dim=head_dim,
        q_dtype=F32, act_dtype=F32)
    cache_kt = cache_k.transpose(0, 1, 3, 4, 2)
    cache_vt = cache_v.transpose(0, 1, 3, 4, 2)
    attn_s = _sample_attention(page_table, rel_bias, q_s, k_s, v_s, cache_kt, cache_vt, ppb=ppb)
    ssm_s, nre_s, nim_s = _s5_sample(u_s, sgb_s, state_re.reshape(bd, ns), state_im.reshape(bd, ns), a_re, a_im,
                                     coef_re, coef_im, bt_re, bt_im, ct_re, ct_im, dsk, w_glu, bgl, gch=gch,
                                     nstate=n_state)
    y_s = _proj_out(xs2, attn_s, sga_s, ssm_s, sma_s, smb_s, wa_bf, wb_bf, wo_bf, tm=bd)

    return (
        y_p.reshape(bsz, seq, dm), y_s.reshape(bd, 1, dm),
        k_p, v_p,
        fre_p.reshape(bsz, n_groups, n_state), fim_p.reshape(bsz, n_groups, n_state),
        k_s.reshape(bd, 1, n_heads, head_dim), v_s.reshape(bd, 1, n_heads, head_dim),
        nre_s.reshape(bd, n_groups, n_state), nim_s.reshape(bd, n_groups, n_state),
    )


def kernel(x_prompt, x_sample, cache_k, cache_v, state_ssm_re, state_ssm_im, page_table, rel_bias, norm_g, w_in,
           q_norm_g, k_norm_g, lam_re, lam_im, log_dt, b_re, b_im, c_re, c_im, d_skip, w_glu, b_glu, w_branch_a,
           w_branch_b, w_out):
    depth = norm_g.shape[0]
    assert depth == 1, "one layer per step"
    outs = _layer(x_prompt, x_sample, cache_k, cache_v, state_ssm_re[0], state_ssm_im[0], page_table,
                  rel_bias, norm_g[0], w_in[0], q_norm_g[0], k_norm_g[0], lam_re[0], lam_im[0], log_dt[0],
                  b_re[0], b_im[0], c_re[0], c_im[0], d_skip[0], w_glu[0], b_glu[0], w_branch_a[0],
                  w_branch_b[0], w_out[0])
    y_p, y_s, k_p, v_p, fre_p, fim_p, k_s, v_s, nre_s, nim_s = outs
    add = lambda a: a[None]
    return (y_p, y_s, add(k_p), add(v_p), add(fre_p), add(fim_p), add(k_s), add(v_s), add(nre_s), add(nim_s))
```

```python
import functools
import math

import jax
import jax.numpy as jnp
from jax import lax
from jax.experimental import pallas as pl
from jax.experimental.pallas import tpu as pltpu

F32 = jnp.float32
BF16 = jnp.bfloat16
HIGHEST = lax.Precision.HIGHEST

MOBA_BLOCK = 256
MOBA_TOPK = 3
REL_BUCKETS = 32
REL_MAX_DIST = 4096
RMS_EPS = 1e-6
LOG2E = math.log2(math.e)
ATTN_HEADS_PER_STEP = 4
LANES = 128
BF16_SUBLANES = 16
VMEM_LIMIT = 56 * 1024 * 1024

NEG_INF = float("-inf")


def _cparams(n_axes, vmem=VMEM_LIMIT):
    return pltpu.CompilerParams(dimension_semantics=("arbitrary",) * n_axes, vmem_limit_bytes=vmem)


def _const_spec(shape, n_grid):
    zeros = (0,) * len(shape)
    if n_grid == 1:
        imap = lambda i: zeros
    elif n_grid == 2:
        imap = lambda i, j: zeros
    else:
        imap = lambda i, j, k: zeros
    return pl.BlockSpec(shape, imap, pipeline_mode=pl.Buffered(1))


def _proj_in_kernel(x_ref, g_ref, w_ref, qg_ref, kg_ref, bd_ref, *out_refs, aw, sw, dm, head_dim, prompt):
    if prompt:
        (q_ref, kb_ref, ksum_ref, kt_ref, vt_ref, vta_ref, sga_ref, u_ref, sgb_ref, sma_ref, smb_ref) = out_refs
    else:
        (q_ref, k_ref, v_ref, sga_ref, u_ref, sgb_ref, sma_ref, smb_ref) = out_refs
    x = x_ref[...]
    ms = jnp.mean(x * x, axis=-1, keepdims=True)
    h = (x * lax.rsqrt(ms + RMS_EPS) * g_ref[...]).astype(BF16)

    def seg(a, b):
        return jnp.dot(h, w_ref[:, a:b], preferred_element_type=F32)

    def headnorm(z, gain):
        ss = jnp.dot((z * z).astype(BF16), bd_ref[...], preferred_element_type=F32)
        return z * lax.rsqrt(ss * (1.0 / head_dim) + RMS_EPS) * gain

    o = 0
    q = headnorm(seg(o, o + aw), qg_ref[...])
    q_ref[...] = q.astype(q_ref.dtype)
    o += aw
    k = headnorm(seg(o, o + aw), kg_ref[...])
    o += aw
    v = seg(o, o + aw)
    o += aw
    if prompt:
        kb_ref[...] = k.astype(BF16)
        ksum_ref[0] = jnp.sum(k, axis=0, keepdims=True)
        kt_ref[0] = k.T
        vt = v.T
        vt_ref[0] = vt
        ones = jnp.ones((BF16_SUBLANES, vt.shape[1]), BF16)
        rows = head_dim + BF16_SUBLANES
        for hd in range(aw // head_dim):
            vta_ref[0, 0, hd * rows:hd * rows + head_dim, :] = vt[hd * head_dim:(hd + 1) * head_dim, :].astype(BF16)
            vta_ref[0, 0, hd * rows + head_dim:(hd + 1) * rows, :] = ones
    else:
        k_ref[...] = k
        v_ref[...] = v
    sga_ref[...] = jax.nn.silu(seg(o, o + aw)).astype(sga_ref.dtype)
    o += aw
    u_ref[...] = seg(o, o + sw).astype(u_ref.dtype)
    o += sw
    sgb_ref[...] = jax.nn.silu(seg(o, o + sw)).astype(sgb_ref.dtype)
    o += sw
    sma_ref[...] = jax.nn.sigmoid(seg(o, o + dm)).astype(sma_ref.dtype)
    o += dm
    smb_ref[...] = jax.nn.sigmoid(seg(o, o + dm)).astype(smb_ref.dtype)


def _proj_in(x2, g, w_bf, qg_t, kg_t, bd, *, tm, aw, sw, head_dim, q_dtype, act_dtype, prompt_batch=None):
    t, dm = x2.shape
    n_tiles = t // tm
    row = lambda w: pl.BlockSpec((tm, w), lambda i: (i, 0))
    prompt = prompt_batch is not None
    if prompt:
        assert tm == MOBA_BLOCK and n_tiles % prompt_batch == 0
        per_seq = n_tiles // prompt_batch
        seq = per_seq * tm
        n_heads = aw // head_dim
        va_rows = n_heads * (head_dim + BF16_SUBLANES)
        t_spec = pl.BlockSpec((1, aw, tm), lambda i: (i // per_seq, 0, i % per_seq))
        out_shape = [
            jax.ShapeDtypeStruct((t, aw), q_dtype),
            jax.ShapeDtypeStruct((t, aw), BF16),
            jax.ShapeDtypeStruct((n_tiles, 1, aw), F32),
            jax.ShapeDtypeStruct((prompt_batch, aw, seq), F32),
            jax.ShapeDtypeStruct((prompt_batch, aw, seq), F32),
            jax.ShapeDtypeStruct((prompt_batch, per_seq, va_rows, tm), BF16),
        ]
        out_specs = [row(aw), row(aw), pl.BlockSpec((1, 1, aw), lambda i: (i, 0, 0)), t_spec, t_spec,
                     pl.BlockSpec((1, 1, va_rows, tm), lambda i: (i // per_seq, i % per_seq, 0, 0))]
    else:
        out_shape = [
            jax.ShapeDtypeStruct((t, aw), q_dtype),
            jax.ShapeDtypeStruct((t, aw), F32),
            jax.ShapeDtypeStruct((t, aw), F32),
        ]
        out_specs = [row(aw)] * 3
    out_shape += [
        jax.ShapeDtypeStruct((t, aw), act_dtype),
        jax.ShapeDtypeStruct((t, sw), act_dtype),
        jax.ShapeDtypeStruct((t, sw), act_dtype),
        jax.ShapeDtypeStruct((t, dm), act_dtype),
        jax.ShapeDtypeStruct((t, dm), act_dtype),
    ]
    out_specs += [row(aw), row(sw), row(sw), row(dm), row(dm)]
    kern = functools.partial(_proj_in_kernel, aw=aw, sw=sw, dm=dm, head_dim=head_dim, prompt=prompt)
    return pl.pallas_call(
        kern,
        grid=(n_tiles,),
        in_specs=[
            row(dm),
            _const_spec((1, dm), 1),
            _const_spec(w_bf.shape, 1),
            _const_spec((1, aw), 1),
            _const_spec((1, aw), 1),
            _const_spec((aw, aw), 1),
        ],
        out_specs=out_specs,
        out_shape=out_shape,
        compiler_params=_cparams(1),
        name="proj_in",
    )(x2, g, w_bf, qg_t, kg_t, bd)


def _rel_bucket(dist):
    n = jnp.maximum(dist, 0)
    max_exact = REL_BUCKETS // 2
    nf = jnp.maximum(n, 1).astype(F32)
    large = max_exact + (
        jnp.log(nf / max_exact) * ((REL_BUCKETS - max_exact) / math.log(REL_MAX_DIST / max_exact))
    ).astype(jnp.int32)
    return jnp.where(n < max_exact, n, jnp.minimum(large, REL_BUCKETS - 1))


def _bias_lookup(bucket, relb_ref, head, lo=0):
    val = jnp.zeros(bucket.shape, F32)
    for b in range(lo, REL_BUCKETS):
        val = jnp.where(bucket == b, relb_ref[b, head], val)
    return val


def _bias_tiles_kernel(relb_ref, o_ref):
    h = pl.program_id(0)
    d = pl.program_id(1)
    key = lax.broadcasted_iota(jnp.int32, (MOBA_BLOCK, MOBA_BLOCK), 0)
    qry = lax.broadcasted_iota(jnp.int32, (MOBA_BLOCK, MOBA_BLOCK), 1)
    dist = d * MOBA_BLOCK + qry - key
    bucket = _rel_bucket(dist)
    exact = REL_BUCKETS // 2
    near = d * MOBA_BLOCK - (MOBA_BLOCK - 1) < exact

    @pl.when(near)
    def _():
        bias = _bias_lookup(bucket, relb_ref, h) * LOG2E
        o_ref[0, 0] = jnp.where(dist >= 0, bias, NEG_INF)

    @pl.when(jnp.logical_not(near))
    def _():
        o_ref[0, 0] = _bias_lookup(bucket, relb_ref, h, lo=exact) * LOG2E


def _n_bias_tiles(n_blocks):
    saturated = -(-(REL_MAX_DIST + MOBA_BLOCK - 1) // MOBA_BLOCK) + 1
    return min(n_blocks, saturated), n_blocks >= saturated


def _bias_tiles(rel_bias, n_heads, n_dist):
    return pl.pallas_call(
        _bias_tiles_kernel,
        grid=(n_heads, n_dist),
        in_specs=[pl.BlockSpec(memory_space=pltpu.SMEM)],
        out_specs=pl.BlockSpec((1, 1, MOBA_BLOCK, MOBA_BLOCK), lambda h, d: (h, d, 0, 0)),
        out_shape=jax.ShapeDtypeStruct((n_heads, n_dist, MOBA_BLOCK, MOBA_BLOCK), F32),
        compiler_params=_cparams(2),
        name="bias_tiles",
    )(rel_bias)


def _select_topk(s):
    rowi = lax.broadcasted_iota(jnp.int32, s.shape, 0)
    sel = jnp.zeros(s.shape, jnp.bool_)
    for _ in range(MOBA_TOPK):
        mx = jnp.max(s, axis=0, keepdims=True)
        cand = jnp.where(s == mx, rowi, s.shape[0])
        idx = jnp.min(cand, axis=0, keepdims=True)
        hit = (rowi == idx) & (mx > NEG_INF)
        sel = sel | hit
        s = jnp.where(hit, NEG_INF, s)
    return sel


def _attn_kernel(q_ref, kb_ref, vta_ref, ksum_ref, bias_ref, o_ref,
                 *scratch, head_dim, n_dist, far_const, scale, nh):
    t = pl.program_id(2)
    blk = MOBA_BLOCK
    n_blocks = vta_ref.shape[1]
    pair = nh * head_dim
    v_rows = head_dim + BF16_SUBLANES
    qts, sels, ms, accs, s_bufs, c_bufs, o_bufs = (scratch[i * nh:(i + 1) * nh] for i in range(7))

    qt = q_ref[0].astype(F32).T
    feat = lax.broadcasted_iota(jnp.int32, (pair, blk), 0)
    means = ksum_ref[0] * (1.0 / blk)
    blk_i = lax.broadcasted_iota(jnp.int32, (n_blocks, blk), 0)
    for h in range(nh):
        in_head = (feat >= h * head_dim) & (feat < (h + 1) * head_dim)
        qt_h = jnp.where(in_head, qt, 0.0)
        qts[h][...] = (qt_h * (scale * LOG2E)).astype(BF16)
        sc = jnp.dot(means, qt_h, preferred_element_type=F32, precision=HIGHEST)
        sc = jnp.where(blk_i < t, sc, NEG_INF)
        sels[h][...] = (_select_topk(sc) | (blk_i == t)).astype(F32)
        ms[h][...] = jnp.full(ms[h].shape, NEG_INF, F32)
        accs[h][...] = jnp.zeros(accs[h].shape, F32)

    def block_of(n):
        return jnp.where(n == 0, t, jnp.where(n <= t, n - 1, 0))

    def stage_a(n, slot, far):
        j = block_of(n)
        kj = kb_ref[0, pl.ds(pl.multiple_of(j * blk, blk), blk), :]
        tile = jnp.minimum(t - j, n_dist - 1)
        for h in range(nh):
            s = jnp.dot(kj, qts[h][...], preferred_element_type=F32)
            if far:
                off = bias_ref[h, n_dist - 1, 0:1, :]
            else:
                s = s + bias_ref[h, tile]
                off = jnp.zeros((1, blk), F32)
            s_bufs[h][slot] = s.astype(BF16)
            c_bufs[h][slot] = jnp.max(s, axis=0, keepdims=True) + off
            o_bufs[h][slot] = off

    def stage_b(n, slot):
        j = block_of(n)
        for h in range(nh):
            picked = jnp.where(n <= t, sels[h][pl.ds(j, 1), :], 0.0)
            m_old = ms[h][...]
            off = o_bufs[h][slot]
            shift = (jnp.maximum(m_old, c_bufs[h][slot]) - off).astype(BF16)
            m_blk = shift.astype(F32) + off
            m_new = jnp.where(picked > 0.0, m_blk, m_old)
            p = jnp.exp2(s_bufs[h][slot] - shift)
            upd = jnp.dot(vta_ref[0, j, h * v_rows:(h + 1) * v_rows, :], p, preferred_element_type=F32)
            accs[h][...] = accs[h][...] * jnp.exp2(m_old - m_new) + upd * picked
            ms[h][...] = m_new

    stage_a(jnp.int32(0), 0, False)
    stage_a(jnp.int32(1), 1, False)

    n_iter = (t + 2) // 2

    def two_blocks(i, carry):
        j = 2 * i
        both_far = (t - (j + 2) >= n_dist - 1) if far_const else False
        for parity in range(2):
            cur, nxt = 2 * parity, 2 - 2 * parity
            mine = (i % 2 == parity)
            for far in ((False, True) if far_const else (False,)):
                @pl.when(mine & (i < n_iter - 1) & (both_far == far))
                def _():
                    stage_a(j + 2, nxt, far)
                    stage_a(j + 3, nxt + 1, far)
                    stage_b(j, cur)
                    stage_b(j + 1, cur + 1)

            @pl.when(mine & (i == n_iter - 1))
            def _():
                stage_b(j, cur)
                stage_b(j + 1, cur + 1)

        return carry

    lax.fori_loop(0, n_iter, two_blocks, 0)

    outs = [accs[h][0:head_dim, :] / accs[h][head_dim:head_dim + 1, :] for h in range(nh)]
    o_ref[0] = jnp.concatenate(outs, axis=0).T.astype(o_ref.dtype)


def _attention(q_bf, k_bf, vt_aug, ksum, bias, *, head_dim, nh):
    b, l, aw = q_bf.shape
    width = nh * head_dim
    assert width % LANES == 0 and aw % width == 0 and l % MOBA_BLOCK == 0
    n_groups = aw // width
    n_blocks = l // MOBA_BLOCK
    n_dist, far_const = _n_bias_tiles(n_blocks)
    assert bias.shape[1] == n_dist
    v_rows = head_dim + BF16_SUBLANES
    assert vt_aug.shape == (b, n_blocks, nh * n_groups * v_rows, MOBA_BLOCK)
    kern = functools.partial(_attn_kernel, head_dim=head_dim, n_dist=n_dist, far_const=far_const,
                             scale=head_dim ** -0.5, nh=nh)
    one = pl.Buffered(1)
    return pl.pallas_call(
        kern,
        grid=(b, n_groups, n_blocks),
        in_specs=[
            pl.BlockSpec((1, MOBA_BLOCK, width), lambda bi, hg, t: (bi, t, hg)),
            pl.BlockSpec((1, l, width), lambda bi, hg, t: (bi, 0, hg), pipeline_mode=one),
            pl.BlockSpec((1, n_blocks, nh * v_rows, MOBA_BLOCK), lambda bi, hg, t: (bi, 0, hg, 0), pipeline_mode=one),
            pl.BlockSpec((1, n_blocks, width), lambda bi, hg, t: (bi, 0, hg), pipeline_mode=one),
            pl.BlockSpec((nh, n_dist, MOBA_BLOCK, MOBA_BLOCK), lambda bi, hg, t: (hg, 0, 0, 0), pipeline_mode=one),
        ],
        out_specs=pl.BlockSpec((1, MOBA_BLOCK, width), lambda bi, hg, t: (bi, t, hg)),
        out_shape=jax.ShapeDtypeStruct((b, l, aw), BF16),
        scratch_shapes=(
            [pltpu.VMEM((width, MOBA_BLOCK), BF16)] * nh
            + [pltpu.VMEM((n_blocks, MOBA_BLOCK), F32)] * nh
            + [pltpu.VMEM((1, MOBA_BLOCK), F32)] * nh
            + [pltpu.VMEM((v_rows, MOBA_BLOCK), F32)] * nh
            + [pltpu.VMEM((4, MOBA_BLOCK, MOBA_BLOCK), BF16)] * nh
            + [pltpu.VMEM((4, 1, MOBA_BLOCK), F32)] * nh
            + [pltpu.VMEM((4, 1, MOBA_BLOCK), F32)] * nh
        ),
        compiler_params=_cparams(3),
        name="moba_attn",
    )(q_bf, k_bf, vt_aug, ksum, bias)


def _ssm_prep_kernel(lre_ref, lim_ref, ldt_ref, are_ref, aim_ref, cre_ref, cim_ref):
    lre = lre_ref[...]
    lim = lim_ref[...]
    dt = jnp.exp(ldt_ref[...])
    mag = jnp.exp(lre * dt)
    a_re = mag * jnp.cos(lim * dt)
    a_im = mag * jnp.sin(lim * dt)
    are_ref[...] = a_re
    aim_ref[...] = a_im
    nr = a_re - 1.0
    den = lre * lre + lim * lim
    cre_ref[...] = (nr * lre + a_im * lim) / den
    cim_ref[...] = (a_im * lre - nr * lim) / den


def _ssm_prep(lam_re, lam_im, log_dt):
    g, p = lam_re.shape
    shp = jax.ShapeDtypeStruct((g, p), F32)
    return pl.pallas_call(_ssm_prep_kernel, out_shape=[shp] * 4, name="ssm_prep")(
        lam_re, lam_im, log_dt.reshape(g, 1))


def _block_diag_mask(rows, cols, row_group, col_group):
    r = lax.broadcasted_iota(jnp.int32, (rows, cols), 0) // row_group
    c = lax.broadcasted_iota(jnp.int32, (rows, cols), 1) // col_group
    return r == c


def _build_ssm_weights(coef_re_ref, coef_im_ref, bt_re_ref, bt_im_ref, ct_re_ref, ct_im_ref,
                       wb_re_ref, wb_im_ref, wc_re_ref, wc_im_ref, *, gch, nstate):
    halves, kh, nh = wb_re_ref.shape
    mask_b = _block_diag_mask(kh, nh, gch, nstate)
    mask_c = _block_diag_mask(nh, kh, nstate, gch)
    for i in range(halves):
        cr = coef_re_ref[i]
        ci = coef_im_ref[i]
        br = bt_re_ref[i * kh:(i + 1) * kh, :]
        bi = bt_im_ref[i * kh:(i + 1) * kh, :]
        wb_re_ref[i] = jnp.where(mask_b, cr * br - ci * bi, 0.0).astype(wb_re_ref.dtype)
        wb_im_ref[i] = jnp.where(mask_b, cr * bi + ci * br, 0.0).astype(wb_im_ref.dtype)
        wc_re_ref[i] = jnp.where(mask_c, ct_re_ref[i * nh:(i + 1) * nh, :], 0.0).astype(wc_re_ref.dtype)
        wc_im_ref[i] = jnp.where(mask_c, -ct_im_ref[i * nh:(i + 1) * nh, :], 0.0).astype(wc_im_ref.dtype)


def _ssm_in(u, wb_re_ref, wb_im_ref, precision=None):
    halves, kh, _ = wb_re_ref.shape
    re, im = [], []
    for i in range(halves):
        ui = u[:, i * kh:(i + 1) * kh]
        re.append(jnp.dot(ui, wb_re_ref[i], preferred_element_type=F32, precision=precision))
        im.append(jnp.dot(ui, wb_im_ref[i], preferred_element_type=F32, precision=precision))
    return jnp.concatenate(re, axis=1), jnp.concatenate(im, axis=1)


def _ssm_out(x_re, x_im, wc_re_ref, wc_im_ref, precision=None):
    halves, nh, _ = wc_re_ref.shape
    ys = []
    for i in range(halves):
        ys.append(jnp.dot(x_re[:, i * nh:(i + 1) * nh], wc_re_ref[i], preferred_element_type=F32, precision=precision)
                  + jnp.dot(x_im[:, i * nh:(i + 1) * nh], wc_im_ref[i], preferred_element_type=F32,
                            precision=precision))
    return jnp.concatenate(ys, axis=1)


def _glu_tail(y, u32, dskip_ref, wglu_ref, bglu_ref, sgb, sw, precision=None):
    y = y + dskip_ref[...] * u32
    za = jax.nn.gelu(y).astype(wglu_ref.dtype)
    glu = jnp.dot(za, wglu_ref[...], preferred_element_type=F32, precision=precision) + bglu_ref[...]
    return glu[:, :sw] * jax.nn.sigmoid(glu[:, sw:]) * sgb


def _s5_prompt_kernel(u_ref, sgb_ref, are_ref, aim_ref, coef_re_ref, coef_im_ref, bt_re_ref, bt_im_ref,
                      ct_re_ref, ct_im_ref, dskip_ref, wglu_ref, bglu_ref,
                      out_ref, fre_ref, fim_ref,
                      wb_re_ref, wb_im_ref, wc_re_ref, wc_im_ref, bu_re_ref, bu_im_ref, xs_re_ref, xs_im_ref,
                      st_re_ref, st_im_ref, *, gch, nstate, sw):
    b = pl.program_id(0)
    t = pl.program_id(1)
    tl = u_ref.shape[1]

    @pl.when((b == 0) & (t == 0))
    def _():
        _build_ssm_weights(coef_re_ref, coef_im_ref, bt_re_ref, bt_im_ref, ct_re_ref, ct_im_ref,
                           wb_re_ref, wb_im_ref, wc_re_ref, wc_im_ref, gch=gch, nstate=nstate)

    @pl.when(t == 0)
    def _():
        st_re_ref[...] = jnp.zeros_like(st_re_ref)
        st_im_ref[...] = jnp.zeros_like(st_im_ref)

    u = u_ref[0]
    bu_re, bu_im = _ssm_in(u, wb_re_ref, wb_im_ref)
    bu_re_ref[...] = bu_re
    bu_im_ref[...] = bu_im
    a_re = are_ref[...]
    a_im = aim_ref[...]

    def step(i, carry):
        xr, xi = carry
        nr = a_re * xr - a_im * xi + bu_re_ref[pl.ds(i, 1), :]
        ni = a_re * xi + a_im * xr + bu_im_ref[pl.ds(i, 1), :]
        xs_re_ref[pl.ds(i, 1), :] = nr
        xs_im_ref[pl.ds(i, 1), :] = ni
        return nr, ni

    xr, xi = lax.fori_loop(0, tl, step, (st_re_ref[...], st_im_ref[...]), unroll=8)
    st_re_ref[...] = xr
    st_im_ref[...] = xi
    fre_ref[0] = xr
    fim_ref[0] = xi

    y = _ssm_out(xs_re_ref[...].astype(BF16), xs_im_ref[...].astype(BF16), wc_re_ref, wc_im_ref)
    out = _glu_tail(y, u.astype(F32), dskip_ref, wglu_ref, bglu_ref, sgb_ref[0].astype(F32), sw)
    out_ref[0] = out.astype(out_ref.dtype)


def _s5_prompt(u, sgb, a_re, a_im, coef_re, coef_im, bt_re, bt_im, ct_re, ct_im, d_skip, w_glu, b_glu, *, tl,
               gch, nstate):
    b, l, sw = u.shape
    ns = a_re.shape[1]
    halves = 2
    kh = sw // halves
    nh = ns // halves
    kern = functools.partial(_s5_prompt_kernel, gch=gch, nstate=nstate, sw=sw)
    tile = pl.BlockSpec((1, tl, sw), lambda bi, t: (bi, t, 0))
    c2 = lambda shape: _const_spec(shape, 2)
    return pl.pallas_call(
        kern,
        grid=(b, l // tl),
        in_specs=[tile, tile, c2((1, ns)), c2((1, ns)), c2((halves, 1, nh)), c2((halves, 1, nh)),
                  c2(bt_re.shape), c2(bt_im.shape), c2(ct_re.shape), c2(ct_im.shape),
                  c2((1, sw)), c2(w_glu.shape), c2((1, 2 * sw))],
        out_specs=[tile,
                   pl.BlockSpec((1, 1, ns), lambda bi, t: (bi, 0, 0)),
                   pl.BlockSpec((1, 1, ns), lambda bi, t: (bi, 0, 0))],
        out_shape=[jax.ShapeDtypeStruct((b, l, sw), BF16),
                   jax.ShapeDtypeStruct((b, 1, ns), F32),
                   jax.ShapeDtypeStruct((b, 1, ns), F32)],
        scratch_shapes=[
            pltpu.VMEM((halves, kh, nh), BF16), pltpu.VMEM((halves, kh, nh), BF16),
            pltpu.VMEM((halves, nh, kh), BF16), pltpu.VMEM((halves, nh, kh), BF16),
            pltpu.VMEM((tl, ns), F32), pltpu.VMEM((tl, ns), F32),
            pltpu.VMEM((tl, ns), F32), pltpu.VMEM((tl, ns), F32),
            pltpu.VMEM((1, ns), F32), pltpu.VMEM((1, ns), F32),
        ],
        compiler_params=_cparams(2),
        name="s5_prompt",
    )(u, sgb, a_re, a_im, coef_re, coef_im, bt_re, bt_im, ct_re, ct_im, d_skip, w_glu, b_glu)


def _s5_sample_kernel(u_ref, sgb_ref, sre_ref, sim_ref, are_ref, aim_ref, coef_re_ref, coef_im_ref,
                      bt_re_ref, bt_im_ref, ct_re_ref, ct_im_ref, dskip_ref, wglu_ref, bglu_ref,
                      out_ref, nre_ref, nim_ref, wb_re_ref, wb_im_ref, wc_re_ref, wc_im_ref, *, gch, nstate, sw):
    _build_ssm_weights(coef_re_ref, coef_im_ref, bt_re_ref, bt_im_ref, ct_re_ref, ct_im_ref,
                       wb_re_ref, wb_im_ref, wc_re_ref, wc_im_ref, gch=gch, nstate=nstate)
    u = u_ref[...]
    bu_re, bu_im = _ssm_in(u, wb_re_ref, wb_im_ref, precision=HIGHEST)
    a_re = are_ref[...]
    a_im = aim_ref[...]
    s_re = sre_ref[...]
    s_im = sim_ref[...]
    x_re = a_re * s_re - a_im * s_im + bu_re
    x_im = a_re * s_im + a_im * s_re + bu_im
    nre_ref[...] = x_re
    nim_ref[...] = x_im
    y = _ssm_out(x_re, x_im, wc_re_ref, wc_im_ref, precision=HIGHEST)
    out_ref[...] = _glu_tail(y, u, dskip_ref, wglu_ref, bglu_ref, sgb_ref[...], sw, precision=HIGHEST)


def _s5_sample(u, sgb, s_re, s_im, a_re, a_im, coef_re, coef_im, bt_re, bt_im, ct_re, ct_im, d_skip, w_glu,
               b_glu, *, gch, nstate):
    n, sw = u.shape
    ns = a_re.shape[1]
    halves = 2
    kh = sw // halves
    nh = ns // halves
    kern = functools.partial(_s5_sample_kernel, gch=gch, nstate=nstate, sw=sw)
    return pl.pallas_call(
        kern,
        out_shape=[jax.ShapeDtypeStruct((n, sw), F32),
                   jax.ShapeDtypeStruct((n, ns), F32),
                   jax.ShapeDtypeStruct((n, ns), F32)],
        scratch_shapes=[
            pltpu.VMEM((halves, kh, nh), F32), pltpu.VMEM((halves, kh, nh), F32),
            pltpu.VMEM((halves, nh, kh), F32), pltpu.VMEM((halves, nh, kh), F32),
        ],
        compiler_params=pltpu.CompilerParams(vmem_limit_bytes=VMEM_LIMIT),
        name="s5_sample",
    )(u, sgb, s_re, s_im, a_re, a_im, coef_re, coef_im, bt_re, bt_im, ct_re, ct_im, d_skip, w_glu, b_glu)


def _proj_out_kernel(x_ref, attn_ref, sga_ref, ssm_ref, sma_ref, smb_ref, wa_ref, wb_ref, wo_ref, o_ref):
    attn_out = (attn_ref[...].astype(F32) * sga_ref[...].astype(F32)).astype(BF16)
    ya = jnp.dot(attn_out, wa_ref[...], preferred_element_type=F32)
    yb = jnp.dot(ssm_ref[...].astype(BF16), wb_ref[...], preferred_element_type=F32)
    merged = sma_ref[...].astype(F32) * ya + smb_ref[...].astype(F32) * yb
    o_ref[...] = x_ref[...] + jnp.dot(merged.astype(BF16), wo_ref[...], preferred_element_type=F32)


def _proj_out(x2, attn, sga, ssm, sma, smb, wa, wb, wo, *, tm):
    t, dm = x2.shape
    aw = attn.shape[1]
    sw = ssm.shape[1]
    row = lambda w: pl.BlockSpec((tm, w), lambda i: (i, 0))
    return pl.pallas_call(
        _proj_out_kernel,
        grid=(t // tm,),
        in_specs=[row(dm), row(aw), row(aw), row(sw), row(dm), row(dm),
                  _const_spec(wa.shape, 1), _const_spec(wb.shape, 1), _const_spec(wo.shape, 1)],
        out_specs=row(dm),
        out_shape=jax.ShapeDtypeStruct((t, dm), F32),
        compiler_params=_cparams(1),
        name="proj_out",
    )(x2, attn, sga, ssm, sma, smb, wa, wb, wo)


def _col_from_row(row, n):
    eye = lax.broadcasted_iota(jnp.int32, (n, n), 0) == lax.broadcasted_iota(jnp.int32, (n, n), 1)
    return jnp.sum(jnp.where(eye, jnp.broadcast_to(row, (n, n)), 0.0), axis=1, keepdims=True)


def _row_from_col(col, n):
    eye = lax.broadcasted_iota(jnp.int32, (n, n), 0) == lax.broadcasted_iota(jnp.int32, (n, n), 1)
    return jnp.sum(jnp.where(eye, jnp.broadcast_to(col, (n, n)), 0.0), axis=0, keepdims=True)


def _sample_attn_kernel(pt_ref, relb_ref, q_ref, qp_ref, knp_ref, vnp_ref, ck_ref, cv_ref, o_ref,
                        kbuf, ksem, qb_ref, lg_ref, tb_ref, selv_ref, sels_ref, selsem, vbuf, vsem,
                        *, n_seq, n_heads, head_dim, ppb, page, n_pages, n_ring, scale):
    step = pl.program_id(0)
    total = n_seq * n_pages
    n_blocks = n_pages // ppb
    past_len = n_pages * page
    n_sel = MOBA_TOPK
    cur = lax.rem(step, 2)
    prev = 1 - cur

    def k_copy(g):
        phys = pt_ref[lax.div(g, n_pages), lax.rem(g, n_pages)]
        slot = lax.rem(g, n_ring)
        return pltpu.make_async_copy(ck_ref.at[0, phys], kbuf.at[slot], ksem.at[slot])

    def sel_copy(par):
        return pltpu.make_async_copy(selv_ref.at[par], sels_ref.at[par], selsem)

    def chosen(par, h, r, i):
        j = sels_ref[par, h, r]
        pgl = jnp.maximum(j, 0) * ppb + i
        return j, pgl, (h * n_sel + r) * ppb + i

    def v_copy(b, par, h, r, i):
        _, pgl, slot = chosen(par, h, r, i)
        return pltpu.make_async_copy(cv_ref.at[0, pt_ref[b, pgl], h], vbuf.at[par, slot], vsem.at[par, slot])

    head_i = lax.broadcasted_iota(jnp.int32, (n_heads, page), 0)
    lane = lax.broadcasted_iota(jnp.int32, (n_heads, LANES), 1)

    @pl.when(step == 0)
    def _():
        for g in range(n_ring - 1):
            k_copy(jnp.int32(g)).start()
        for bk in range(REL_BUCKETS):
            tab = jnp.zeros((n_heads, page), F32)
            for h in range(n_heads):
                tab = jnp.where(head_i == h, relb_ref[bk, h], tab)
            tb_ref[bk] = tab

    @pl.when(step < n_seq)
    def _():
        q = q_ref[0]
        for h in range(n_heads):
            col = _col_from_row(q[:, h * head_dim:(h + 1) * head_dim], head_dim)
            qb_ref[h] = jnp.broadcast_to(col, (head_dim, page))

        def stream(pg, carry):
            g = step * n_pages + pg
            nxt = g + (n_ring - 1)

            @pl.when(nxt < total)
            def _():
                k_copy(nxt).start()

            k_copy(g).wait()
            kt = kbuf[lax.rem(g, n_ring)]
            lg_ref[cur, pg] = jnp.sum(kt * qb_ref[...], axis=1)
            return carry

        lax.fori_loop(0, n_pages, stream, 0)

        sc = jnp.full((n_heads, LANES), NEG_INF, F32)
        for jb in range(n_blocks):
            tot = lg_ref[cur, jb * ppb]
            for i in range(1, ppb):
                tot = tot + lg_ref[cur, jb * ppb + i]
            sc = jnp.where(lane == jb, jnp.sum(tot, axis=1, keepdims=True) * (1.0 / MOBA_BLOCK), sc)
        sel = jnp.full((n_heads, LANES), -1, jnp.int32)
        for r in range(n_sel):
            mx = jnp.max(sc, axis=1, keepdims=True)
            idx = jnp.min(jnp.where(sc == mx, lane, LANES), axis=1, keepdims=True)
            ok = mx > NEG_INF
            sel = jnp.where(lane == r, jnp.where(ok, idx, -1), sel)
            sc = jnp.where((lane == idx) & ok, NEG_INF, sc)
        selv_ref[cur] = sel
        sel_copy(cur).start()

    @pl.when(step >= 1)
    def _():
        b = step - 1
        q = qp_ref[0]
        sel = selv_ref[prev]
        pos = lax.broadcasted_iota(jnp.int32, (n_heads, page), 1)
        qk_new = q * knp_ref[0]
        s_self = jnp.zeros((n_heads, 1), F32)
        for h in range(n_heads):
            dot_h = jnp.sum(qk_new[:, h * head_dim:(h + 1) * head_dim], axis=1, keepdims=True)
            s_self = jnp.where(head_i[:, 0:1] == h, dot_h, s_self)
        s_self = s_self * scale + tb_ref[0][:, 0:1]
        tiles = []
        for r in range(n_sel):
            blk_r = jnp.sum(jnp.where(lane == r, sel, 0), axis=1, keepdims=True)
            for i in range(ppb):
                raw = jnp.zeros((n_heads, page), F32)
                for h in range(n_heads):
                    raw = jnp.where(head_i == h, lg_ref[prev, chosen(prev, h, r, i)[1]][h:h + 1, :], raw)
                bucket = _rel_bucket(past_len - (jnp.maximum(blk_r, 0) * ppb + i) * page - pos)
                bias = jnp.zeros((n_heads, page), F32)
                for bk in range(REL_BUCKETS):
                    bias = jnp.where(bucket == bk, tb_ref[bk], bias)
                tiles.append(jnp.where(blk_r >= 0, raw * scale + bias, NEG_INF))
        m = s_self
        for s in tiles:
            m = jnp.maximum(m, jnp.max(s, axis=1, keepdims=True))
        p_self = jnp.exp(s_self - m)
        den = p_self
        probs = []
        for s in tiles:
            p = jnp.exp(s - m)
            den = den + jnp.sum(p, axis=1, keepdims=True)
            probs.append(p)
        rows = []
        for h in range(n_heads):
            acc = jnp.zeros((head_dim, page), F32)
            for n, p in enumerate(probs):
                j, _, slot = chosen(prev, h, n // ppb, n % ppb)

                @pl.when(j >= 0)
                def _():
                    v_copy(b, prev, h, n // ppb, n % ppb).wait()

                acc = acc + jnp.where(j >= 0, vbuf[prev, slot], 0.0) * p[h:h + 1, :]
            o_row = _row_from_col(jnp.sum(acc, axis=1, keepdims=True), head_dim)
            v_self = vnp_ref[0][:, h * head_dim:(h + 1) * head_dim]
            rows.append((o_row + p_self[h:h + 1, :] * v_self) / den[h:h + 1, :])
        o_ref[0] = jnp.concatenate(rows, axis=1)

    @pl.when(step < n_seq)
    def _():
        sel_copy(cur).wait()
        for h in range(n_heads):
            for r in range(n_sel):
                for i in range(ppb):
                    @pl.when(chosen(cur, h, r, i)[0] >= 0)
                    def _():
                        v_copy(step, cur, h, r, i).start()


def _sample_attention(page_table, rel_bias, q, k_new, v_new, cache_kt, cache_vt, *, ppb, n_ring=32):
    bd, w = q.shape
    _, n_pool, n_heads, head_dim, page = cache_kt.shape
    n_pages = page_table.shape[1]
    assert n_pages % ppb == 0 and n_pages // ppb <= LANES and n_heads * head_dim == w
    n_slots = n_heads * MOBA_TOPK * ppb
    n_ring = min(n_ring, bd * n_pages)
    kern = functools.partial(_sample_attn_kernel, n_seq=bd, n_heads=n_heads, head_dim=head_dim, ppb=ppb, page=page,
                             n_pages=n_pages, n_ring=n_ring, scale=head_dim ** -0.5)
    row_cur = pl.BlockSpec((1, 1, w), lambda s, pt: (jnp.minimum(s, bd - 1), 0, 0))
    row_prev = pl.BlockSpec((1, 1, w), lambda s, pt: (jnp.maximum(s - 1, 0), 0, 0))
    grid_spec = pltpu.PrefetchScalarGridSpec(
        num_scalar_prefetch=1,
        grid=(bd + 1,),
        in_specs=[pl.BlockSpec(memory_space=pltpu.SMEM), row_cur, row_prev, row_prev, row_prev,
                  pl.BlockSpec(memory_space=pl.ANY), pl.BlockSpec(memory_space=pl.ANY)],
        out_specs=row_prev,
        scratch_shapes=[
            pltpu.VMEM((n_ring, n_heads, head_dim, page), F32),
            pltpu.SemaphoreType.DMA((n_ring,)),
            pltpu.VMEM((n_heads, head_dim, page), F32),
            pltpu.VMEM((2, n_pages, n_heads, page), F32),
            pltpu.VMEM((REL_BUCKETS, n_heads, page), F32),
            pltpu.VMEM((2, n_heads, LANES), jnp.int32),
            pltpu.SMEM((2, n_heads, LANES), jnp.int32),
            pltpu.SemaphoreType.DMA(()),
            pltpu.VMEM((2, n_slots, head_dim, page), F32),
            pltpu.SemaphoreType.DMA((2, n_slots)),
        ],
    )
    q3 = q.reshape(bd, 1, w)
    out = pl.pallas_call(
        kern,
        grid_spec=grid_spec,
        out_shape=jax.ShapeDtypeStruct((bd, 1, w), F32),
        compiler_params=_cparams(1),
        name="sample_attn",
    )(page_table, rel_bias, q3, q3, k_new.reshape(bd, 1, w), v_new.reshape(bd, 1, w), cache_kt, cache_vt)
    return out.reshape(bd, w)


def _layer(x_prompt, x_sample, cache_k, cache_v, state_re, state_im, page_table, rel_bias, norm_g, w_in,
           q_norm_g, k_norm_g, lam_re, lam_im, log_dt, b_re, b_im, c_re, c_im, d_skip, w_glu, b_glu,
           w_branch_a, w_branch_b, w_out):
    bsz, seq, dm = x_prompt.shape
    bd, s_new, _ = x_sample.shape
    _, n_pool, page, n_heads, head_dim = cache_k.shape
    n_groups, n_state, gch = b_re.shape
    aw = n_heads * head_dim
    sw = n_groups * gch
    ns = n_groups * n_state
    n_pages = page_table.shape[1]
    past_len = n_pages * page
    ppb = MOBA_BLOCK // page
    assert s_new == 1 and seq % MOBA_BLOCK == 0 and past_len % MOBA_BLOCK == 0
    assert w_in.shape == (dm, 4 * aw + 2 * sw + 2 * dm)

    w_in_bf = w_in.astype(BF16)
    g_row = norm_g.reshape(1, dm)
    qg_t = jnp.tile(q_norm_g.reshape(1, head_dim), (1, n_heads))
    kg_t = jnp.tile(k_norm_g.reshape(1, head_dim), (1, n_heads))
    head_of = jnp.arange(aw) // head_dim
    bd_ones = (head_of[:, None] == head_of[None, :]).astype(BF16)
    halves = 2
    bt_re = jnp.tile(b_re.transpose(0, 2, 1).reshape(sw, n_state), (1, n_groups // halves))
    bt_im = jnp.tile(b_im.transpose(0, 2, 1).reshape(sw, n_state), (1, n_groups // halves))
    ct_re = jnp.tile(c_re.transpose(0, 2, 1).reshape(ns, gch), (1, n_groups // halves))
    ct_im = jnp.tile(c_im.transpose(0, 2, 1).reshape(ns, gch), (1, n_groups // halves))
    dsk = d_skip.reshape(1, sw)
    bgl = b_glu.reshape(1, 2 * sw)

    a_re, a_im, coef_re, coef_im = _ssm_prep(lam_re, lam_im, log_dt)
    a_re = a_re.reshape(1, ns)
    a_im = a_im.reshape(1, ns)
    coef_re = coef_re.reshape(halves, 1, ns // halves)
    coef_im = coef_im.reshape(halves, 1, ns // halves)

    wa_bf = w_branch_a.astype(BF16)
    wb_bf = w_branch_b.astype(BF16)
    wo_bf = w_out.astype(BF16)
    wglu_bf = w_glu.astype(BF16)

    xp2 = x_prompt.reshape(bsz * seq, dm)
    (q_p, kb_p, ksum_p, kt_p, vt_p, vta_p, sga_p, u_p, sgb_p, sma_p, smb_p) = _proj_in(
        xp2, g_row, w_in_bf, qg_t, kg_t, bd_ones, tm=MOBA_BLOCK, aw=aw, sw=sw, head_dim=head_dim,
        q_dtype=BF16, act_dtype=BF16, prompt_batch=bsz)
    n_blocks = seq // MOBA_BLOCK
    bias = _bias_tiles(rel_bias, n_heads, _n_bias_tiles(n_blocks)[0])
    attn_p = _attention(q_p.reshape(bsz, seq, aw), kb_p.reshape(bsz, seq, aw), vta_p,
                        ksum_p.reshape(bsz, n_blocks, aw), bias, head_dim=head_dim, nh=ATTN_HEADS_PER_STEP)
    k_p = kt_p.reshape(bsz, n_heads, head_dim, seq).transpose(0, 3, 1, 2)
    v_p = vt_p.reshape(bsz, n_heads, head_dim, seq).transpose(0, 3, 1, 2)
    ssm_p, fre_p, fim_p = _s5_prompt(u_p.reshape(bsz, seq, sw), sgb_p.reshape(bsz, seq, sw), a_re, a_im, coef_re,
                                     coef_im, bt_re, bt_im, ct_re, ct_im, dsk, wglu_bf, bgl, tl=MOBA_BLOCK,
                                     gch=gch, nstate=n_state)
    y_p = _proj_out(xp2, attn_p.reshape(bsz * seq, aw), sga_p, ssm_p.reshape(bsz * seq, sw), sma_p, smb_p,
                    wa_bf, wb_bf, wo_bf, tm=MOBA_BLOCK)

    xs2 = x_sample.reshape(bd, dm)
    (q_s, k_s, v_s, sga_s, u_s, sgb_s, sma_s, smb_s) = _proj_in(
        xs2, g_row, w_in_bf, qg_t, kg_t, bd_ones, tm=bd, aw=aw, sw=sw, head_dim=head_dim,
        q_dtype=F32, act_dtype=F32)
    cache_kt = cache_k.transpose(0, 1, 3, 4, 2)
    cache_vt = cache_v.transpose(0, 1, 3, 4, 2)
    attn_s = _sample_attention(page_table, rel_bias, q_s, k_s, v_s, cache_kt, cache_vt, ppb=ppb)
    ssm_s, nre_s, nim_s = _s5_sample(u_s, sgb_s, state_re.reshape(bd, ns), state_im.reshape(bd, ns), a_re, a_im,
                                     coef_re, coef_im, bt_re, bt_im, ct_re, ct_im, dsk, w_glu, bgl, gch=gch,
                                     nstate=n_state)
    y_s = _proj_out(xs2, attn_s, sga_s, ssm_s, sma_s, smb_s, wa_bf, wb_bf, wo_bf, tm=bd)

    return (
        y_p.reshape(bsz, seq, dm), y_s.reshape(bd, 1, dm),
        k_p, v_p,
        fre_p.reshape(bsz, n_groups, n_state), fim_p.reshape(bsz, n_groups, n_state),
        k_s.reshape(bd, 1, n_heads, head_dim), v_s.reshape(bd, 1, n_heads, head_dim),
        nre_s.reshape(bd, n_groups, n_state), nim_s.reshape(bd, n_groups, n_state),
    )


def kernel(x_prompt, x_sample, cache_k, cache_v, state_ssm_re, state_ssm_im, page_table, rel_bias, norm_g, w_in,
           q_norm_g, k_norm_g, lam_re, lam_im, log_dt, b_re, b_im, c_re, c_im, d_skip, w_glu, b_glu, w_branch_a,
           w_branch_b, w_out):
    depth = norm_g.shape[0]
    assert depth == 1, "one layer per step"
    outs = _layer(x_prompt, x_sample, cache_k, cache_v, state_ssm_re[0], state_ssm_im[0], page_table,
                  rel_bias, norm_g[0], w_in[0], q_norm_g[0], k_norm_g[0], lam_re[0], lam_im[0], log_dt[0],
                  b_re[0], b_im[0], c_re[0], c_im[0], d_skip[0], w_glu[0], b_glu[0], w_branch_a[0],
                  w_branch_b[0], w_out[0])
    y_p, y_s, k_p, v_p, fre_p, fim_p, k_s, v_s, nre_s, nim_s = outs
    add = lambda a: a[None]
    return (y_p, y_s, add(k_p), add(v_p), add(fre_p), add(fim_p), add(k_s), add(v_s), add(nre_s), add(nim_s))
```

```python
import functools
import math

import jax
import jax.numpy as jnp
from jax import lax
from jax.experimental import pallas as pl
from jax.experimental.pallas import tpu as pltpu

F32 = jnp.float32
BF16 = jnp.bfloat16
HIGHEST = lax.Precision.HIGHEST

MOBA_BLOCK = 256
MOBA_TOPK = 3
REL_BUCKETS = 32
REL_MAX_DIST = 4096
RMS_EPS = 1e-6
LOG2E = math.log2(math.e)
PROJ_OUT_ROWS = 512
ATTN_HEADS_PER_STEP = 4
LANES = 128
BF16_SUBLANES = 16
VMEM_LIMIT = 56 * 1024 * 1024

NEG_INF = float("-inf")


def _cparams(n_axes, vmem=VMEM_LIMIT):
    return pltpu.CompilerParams(dimension_semantics=("arbitrary",) * n_axes, vmem_limit_bytes=vmem)


def _const_spec(shape, n_grid):
    zeros = (0,) * len(shape)
    if n_grid == 1:
        imap = lambda i: zeros
    elif n_grid == 2:
        imap = lambda i, j: zeros
    else:
        imap = lambda i, j, k: zeros
    return pl.BlockSpec(shape, imap, pipeline_mode=pl.Buffered(1))


def _proj_in_kernel(x_ref, g_ref, w_ref, qg_ref, kg_ref, bd_ref, *out_refs, aw, sw, dm, head_dim, prompt):
    if prompt:
        (q_ref, kb_ref, ksum_ref, kt_ref, vt_ref, vta_ref, sga_ref, u_ref, sgb_ref, sma_ref, smb_ref) = out_refs
    else:
        (q_ref, k_ref, v_ref, sga_ref, u_ref, sgb_ref, sma_ref, smb_ref) = out_refs
    x = x_ref[...]
    ms = jnp.mean(x * x, axis=-1, keepdims=True)
    h = (x * lax.rsqrt(ms + RMS_EPS) * g_ref[...]).astype(BF16)

    def seg(a, b):
        return jnp.dot(h, w_ref[:, a:b], preferred_element_type=F32)

    def headnorm(z, gain):
        ss = jnp.dot((z * z).astype(BF16), bd_ref[...], preferred_element_type=F32)
        return z * lax.rsqrt(ss * (1.0 / head_dim) + RMS_EPS) * gain

    o = 0
    q = headnorm(seg(o, o + aw), qg_ref[...])
    q_ref[...] = q.astype(q_ref.dtype)
    o += aw
    k = headnorm(seg(o, o + aw), kg_ref[...])
    o += aw
    v = seg(o, o + aw)
    o += aw
    if prompt:
        kb_ref[...] = k.astype(BF16)
        ksum_ref[0] = jnp.sum(k, axis=0, keepdims=True)
        kt_ref[0] = k.T
        vt = v.T
        vt_ref[0] = vt
        ones = jnp.ones((BF16_SUBLANES, vt.shape[1]), BF16)
        rows = head_dim + BF16_SUBLANES
        for hd in range(aw // head_dim):
            vta_ref[0, 0, hd * rows:hd * rows + head_dim, :] = vt[hd * head_dim:(hd + 1) * head_dim, :].astype(BF16)
            vta_ref[0, 0, hd * rows + head_dim:(hd + 1) * rows, :] = ones
    else:
        k_ref[...] = k
        v_ref[...] = v
    sga_ref[...] = jax.nn.silu(seg(o, o + aw)).astype(sga_ref.dtype)
    o += aw
    u_ref[...] = seg(o, o + sw).astype(u_ref.dtype)
    o += sw
    sgb_ref[...] = jax.nn.silu(seg(o, o + sw)).astype(sgb_ref.dtype)
    o += sw
    sma_ref[...] = jax.nn.sigmoid(seg(o, o + dm)).astype(sma_ref.dtype)
    o += dm
    smb_ref[...] = jax.nn.sigmoid(seg(o, o + dm)).astype(smb_ref.dtype)


def _proj_in(x2, g, w_bf, qg_t, kg_t, bd, *, tm, aw, sw, head_dim, q_dtype, act_dtype, prompt_batch=None):
    t, dm = x2.shape
    n_tiles = t // tm
    row = lambda w: pl.BlockSpec((tm, w), lambda i: (i, 0))
    prompt = prompt_batch is not None
    if prompt:
        assert tm == MOBA_BLOCK and n_tiles % prompt_batch == 0
        per_seq = n_tiles // prompt_batch
        seq = per_seq * tm
        n_heads = aw // head_dim
        va_rows = n_heads * (head_dim + BF16_SUBLANES)
        t_spec = pl.BlockSpec((1, aw, tm), lambda i: (i // per_seq, 0, i % per_seq))
        out_shape = [
            jax.ShapeDtypeStruct((t, aw), q_dtype),
            jax.ShapeDtypeStruct((t, aw), BF16),
            jax.ShapeDtypeStruct((n_tiles, 1, aw), F32),
            jax.ShapeDtypeStruct((prompt_batch, aw, seq), F32),
            jax.ShapeDtypeStruct((prompt_batch, aw, seq), F32),
            jax.ShapeDtypeStruct((prompt_batch, per_seq, va_rows, tm), BF16),
        ]
        out_specs = [row(aw), row(aw), pl.BlockSpec((1, 1, aw), lambda i: (i, 0, 0)), t_spec, t_spec,
                     pl.BlockSpec((1, 1, va_rows, tm), lambda i: (i // per_seq, i % per_seq, 0, 0))]
    else:
        out_shape = [
            jax.ShapeDtypeStruct((t, aw), q_dtype),
            jax.ShapeDtypeStruct((t, aw), F32),
            jax.ShapeDtypeStruct((t, aw), F32),
        ]
        out_specs = [row(aw)] * 3
    out_shape += [
        jax.ShapeDtypeStruct((t, aw), act_dtype),
        jax.ShapeDtypeStruct((t, sw), act_dtype),
        jax.ShapeDtypeStruct((t, sw), act_dtype),
        jax.ShapeDtypeStruct((t, dm), act_dtype),
        jax.ShapeDtypeStruct((t, dm), act_dtype),
    ]
    out_specs += [row(aw), row(sw), row(sw), row(dm), row(dm)]
    kern = functools.partial(_proj_in_kernel, aw=aw, sw=sw, dm=dm, head_dim=head_dim, prompt=prompt)
    return pl.pallas_call(
        kern,
        grid=(n_tiles,),
        in_specs=[
            row(dm),
            _const_spec((1, dm), 1),
            _const_spec(w_bf.shape, 1),
            _const_spec((1, aw), 1),
            _const_spec((1, aw), 1),
            _const_spec((aw, aw), 1),
        ],
        out_specs=out_specs,
        out_shape=out_shape,
        compiler_params=_cparams(1),
        name="proj_in",
    )(x2, g, w_bf, qg_t, kg_t, bd)


def _rel_bucket(dist):
    n = jnp.maximum(dist, 0)
    max_exact = REL_BUCKETS // 2
    nf = jnp.maximum(n, 1).astype(F32)
    large = max_exact + (
        jnp.log(nf / max_exact) * ((REL_BUCKETS - max_exact) / math.log(REL_MAX_DIST / max_exact))
    ).astype(jnp.int32)
    return jnp.where(n < max_exact, n, jnp.minimum(large, REL_BUCKETS - 1))


def _bias_lookup(bucket, relb_ref, head, lo=0):
    val = jnp.zeros(bucket.shape, F32)
    for b in range(lo, REL_BUCKETS):
        val = jnp.where(bucket == b, relb_ref[b, head], val)
    return val


def _bias_tiles_kernel(relb_ref, o_ref):
    h = pl.program_id(0)
    d = pl.program_id(1)
    key = lax.broadcasted_iota(jnp.int32, (MOBA_BLOCK, MOBA_BLOCK), 0)
    qry = lax.broadcasted_iota(jnp.int32, (MOBA_BLOCK, MOBA_BLOCK), 1)
    dist = d * MOBA_BLOCK + qry - key
    bucket = _rel_bucket(dist)
    exact = REL_BUCKETS // 2
    near = d * MOBA_BLOCK - (MOBA_BLOCK - 1) < exact

    @pl.when(near)
    def _():
        bias = _bias_lookup(bucket, relb_ref, h) * LOG2E
        o_ref[0, 0] = jnp.where(dist >= 0, bias, NEG_INF)

    @pl.when(jnp.logical_not(near))
    def _():
        o_ref[0, 0] = _bias_lookup(bucket, relb_ref, h, lo=exact) * LOG2E


def _n_bias_tiles(n_blocks):
    saturated = -(-(REL_MAX_DIST + MOBA_BLOCK - 1) // MOBA_BLOCK) + 1
    return min(n_blocks, saturated), n_blocks >= saturated


def _bias_tiles(rel_bias, n_heads, n_dist):
    return pl.pallas_call(
        _bias_tiles_kernel,
        grid=(n_heads, n_dist),
        in_specs=[pl.BlockSpec(memory_space=pltpu.SMEM)],
        out_specs=pl.BlockSpec((1, 1, MOBA_BLOCK, MOBA_BLOCK), lambda h, d: (h, d, 0, 0)),
        out_shape=jax.ShapeDtypeStruct((n_heads, n_dist, MOBA_BLOCK, MOBA_BLOCK), F32),
        compiler_params=_cparams(2),
        name="bias_tiles",
    )(rel_bias)


def _select_topk(s):
    rowi = lax.broadcasted_iota(jnp.int32, s.shape, 0)
    sel = jnp.zeros(s.shape, jnp.bool_)
    for _ in range(MOBA_TOPK):
        mx = jnp.max(s, axis=0, keepdims=True)
        cand = jnp.where(s == mx, rowi, s.shape[0])
        idx = jnp.min(cand, axis=0, keepdims=True)
        hit = (rowi == idx) & (mx > NEG_INF)
        sel = sel | hit
        s = jnp.where(hit, NEG_INF, s)
    return sel


def _attn_kernel(q_ref, kb_ref, vta_ref, ksum_ref, bias_ref, o_ref,
                 *scratch, head_dim, n_dist, far_const, scale, nh):
    t = pl.program_id(2)
    blk = MOBA_BLOCK
    n_blocks = vta_ref.shape[1]
    pair = nh * head_dim
    v_rows = head_dim + BF16_SUBLANES
    qts, sels, ms, accs, s_bufs, c_bufs, o_bufs = (scratch[i * nh:(i + 1) * nh] for i in range(7))

    qt = q_ref[0].astype(F32).T
    feat = lax.broadcasted_iota(jnp.int32, (pair, blk), 0)
    means = ksum_ref[0] * (1.0 / blk)
    blk_i = lax.broadcasted_iota(jnp.int32, (n_blocks, blk), 0)
    for h in range(nh):
        in_head = (feat >= h * head_dim) & (feat < (h + 1) * head_dim)
        qt_h = jnp.where(in_head, qt, 0.0)
        qts[h][...] = (qt_h * (scale * LOG2E)).astype(BF16)
        sc = jnp.dot(means, qt_h, preferred_element_type=F32, precision=HIGHEST)
        sc = jnp.where(blk_i < t, sc, NEG_INF)
        sels[h][...] = (_select_topk(sc) | (blk_i == t)).astype(F32)
        ms[h][...] = jnp.full(ms[h].shape, NEG_INF, F32)
        accs[h][...] = jnp.zeros(accs[h].shape, F32)

    def block_of(n):
        return jnp.where(n == 0, t, jnp.where(n <= t, n - 1, 0))

    def stage_a(n, slot, far):
        j = block_of(n)
        kj = kb_ref[0, pl.ds(pl.multiple_of(j * blk, blk), blk), :]
        tile = jnp.minimum(t - j, n_dist - 1)
        for h in range(nh):
            s = jnp.dot(kj, qts[h][...], preferred_element_type=F32)
            if far:
                off = bias_ref[h, n_dist - 1, 0:1, :]
            else:
                s = s + bias_ref[h, tile]
                off = jnp.zeros((1, blk), F32)
            s_bufs[h][slot] = s.astype(BF16)
            c_bufs[h][slot] = jnp.max(s, axis=0, keepdims=True) + off
            o_bufs[h][slot] = off

    def stage_b(n, slot):
        j = block_of(n)
        for h in range(nh):
            picked = jnp.where(n <= t, sels[h][pl.ds(j, 1), :], 0.0)
            m_old = ms[h][...]
            off = o_bufs[h][slot]
            shift = (jnp.maximum(m_old, c_bufs[h][slot]) - off).astype(BF16)
            m_blk = shift.astype(F32) + off
            m_new = jnp.where(picked > 0.0, m_blk, m_old)
            p = jnp.exp2(s_bufs[h][slot] - shift)
            upd = jnp.dot(vta_ref[0, j, h * v_rows:(h + 1) * v_rows, :], p, preferred_element_type=F32)
            accs[h][...] = accs[h][...] * jnp.exp2(m_old - m_new) + upd * picked
            ms[h][...] = m_new

    stage_a(jnp.int32(0), 0, False)
    stage_a(jnp.int32(1), 1, False)

    n_iter = (t + 2) // 2

    def two_blocks(i, carry):
        j = 2 * i
        both_far = (t - (j + 2) >= n_dist - 1) if far_const else False
        for parity in range(2):
            cur, nxt = 2 * parity, 2 - 2 * parity
            mine = (i % 2 == parity)
            for far in ((False, True) if far_const else (False,)):
                @pl.when(mine & (i < n_iter - 1) & (both_far == far))
                def _():
                    stage_a(j + 2, nxt, far)
                    stage_a(j + 3, nxt + 1, far)
                    stage_b(j, cur)
                    stage_b(j + 1, cur + 1)

            @pl.when(mine & (i == n_iter - 1))
            def _():
                stage_b(j, cur)
                stage_b(j + 1, cur + 1)

        return carry

    lax.fori_loop(0, n_iter, two_blocks, 0)

    outs = [accs[h][0:head_dim, :] / accs[h][head_dim:head_dim + 1, :] for h in range(nh)]
    o_ref[0] = jnp.concatenate(outs, axis=0).T.astype(o_ref.dtype)


def _attention(q_bf, k_bf, vt_aug, ksum, bias, *, head_dim, nh):
    b, l, aw = q_bf.shape
    width = nh * head_dim
    assert width % LANES == 0 and aw % width == 0 and l % MOBA_BLOCK == 0
    n_groups = aw // width
    n_blocks = l // MOBA_BLOCK
    n_dist, far_const = _n_bias_tiles(n_blocks)
    assert bias.shape[1] == n_dist
    v_rows = head_dim + BF16_SUBLANES
    assert vt_aug.shape == (b, n_blocks, nh * n_groups * v_rows, MOBA_BLOCK)
    kern = functools.partial(_attn_kernel, head_dim=head_dim, n_dist=n_dist, far_const=far_const,
                             scale=head_dim ** -0.5, nh=nh)
    one = pl.Buffered(1)
    return pl.pallas_call(
        kern,
        grid=(b, n_groups, n_blocks),
        in_specs=[
            pl.BlockSpec((1, MOBA_BLOCK, width), lambda bi, hg, t: (bi, t, hg)),
            pl.BlockSpec((1, l, width), lambda bi, hg, t: (bi, 0, hg), pipeline_mode=one),
            pl.BlockSpec((1, n_blocks, nh * v_rows, MOBA_BLOCK), lambda bi, hg, t: (bi, 0, hg, 0), pipeline_mode=one),
            pl.BlockSpec((1, n_blocks, width), lambda bi, hg, t: (bi, 0, hg), pipeline_mode=one),
            pl.BlockSpec((nh, n_dist, MOBA_BLOCK, MOBA_BLOCK), lambda bi, hg, t: (hg, 0, 0, 0), pipeline_mode=one),
        ],
        out_specs=pl.BlockSpec((1, MOBA_BLOCK, width), lambda bi, hg, t: (bi, t, hg)),
        out_shape=jax.ShapeDtypeStruct((b, l, aw), BF16),
        scratch_shapes=(
            [pltpu.VMEM((width, MOBA_BLOCK), BF16)] * nh
            + [pltpu.VMEM((n_blocks, MOBA_BLOCK), F32)] * nh
            + [pltpu.VMEM((1, MOBA_BLOCK), F32)] * nh
            + [pltpu.VMEM((v_rows, MOBA_BLOCK), F32)] * nh
            + [pltpu.VMEM((4, MOBA_BLOCK, MOBA_BLOCK), BF16)] * nh
            + [pltpu.VMEM((4, 1, MOBA_BLOCK), F32)] * nh
            + [pltpu.VMEM((4, 1, MOBA_BLOCK), F32)] * nh
        ),
        compiler_params=_cparams(3),
        name="moba_attn",
    )(q_bf, k_bf, vt_aug, ksum, bias)


def _ssm_prep_kernel(lre_ref, lim_ref, ldt_ref, are_ref, aim_ref, cre_ref, cim_ref):
    lre = lre_ref[...]
    lim = lim_ref[...]
    dt = jnp.exp(ldt_ref[...])
    mag = jnp.exp(lre * dt)
    a_re = mag * jnp.cos(lim * dt)
    a_im = mag * jnp.sin(lim * dt)
    are_ref[...] = a_re
    aim_ref[...] = a_im
    nr = a_re - 1.0
    den = lre * lre + lim * lim
    cre_ref[...] = (nr * lre + a_im * lim) / den
    cim_ref[...] = (a_im * lre - nr * lim) / den


def _ssm_prep(lam_re, lam_im, log_dt):
    g, p = lam_re.shape
    shp = jax.ShapeDtypeStruct((g, p), F32)
    return pl.pallas_call(_ssm_prep_kernel, out_shape=[shp] * 4, name="ssm_prep")(
        lam_re, lam_im, log_dt.reshape(g, 1))


def _block_diag_mask(rows, cols, row_group, col_group):
    r = lax.broadcasted_iota(jnp.int32, (rows, cols), 0) // row_group
    c = lax.broadcasted_iota(jnp.int32, (rows, cols), 1) // col_group
    return r == c


def _build_ssm_weights(coef_re_ref, coef_im_ref, bt_re_ref, bt_im_ref, ct_re_ref, ct_im_ref,
                       wb_re_ref, wb_im_ref, wc_re_ref, wc_im_ref, *, gch, nstate):
    halves, kh, nh = wb_re_ref.shape
    mask_b = _block_diag_mask(kh, nh, gch, nstate)
    mask_c = _block_diag_mask(nh, kh, nstate, gch)
    for i in range(halves):
        cr = coef_re_ref[i]
        ci = coef_im_ref[i]
        br = bt_re_ref[i * kh:(i + 1) * kh, :]
        bi = bt_im_ref[i * kh:(i + 1) * kh, :]
        wb_re_ref[i] = jnp.where(mask_b, cr * br - ci * bi, 0.0).astype(wb_re_ref.dtype)
        wb_im_ref[i] = jnp.where(mask_b, cr * bi + ci * br, 0.0).astype(wb_im_ref.dtype)
        wc_re_ref[i] = jnp.where(mask_c, ct_re_ref[i * nh:(i + 1) * nh, :], 0.0).astype(wc_re_ref.dtype)
        wc_im_ref[i] = jnp.where(mask_c, -ct_im_ref[i * nh:(i + 1) * nh, :], 0.0).astype(wc_im_ref.dtype)


def _ssm_in(u, wb_re_ref, wb_im_ref, precision=None):
    halves, kh, _ = wb_re_ref.shape
    re, im = [], []
    for i in range(halves):
        ui = u[:, i * kh:(i + 1) * kh]
        re.append(jnp.dot(ui, wb_re_ref[i], preferred_element_type=F32, precision=precision))
        im.append(jnp.dot(ui, wb_im_ref[i], preferred_element_type=F32, precision=precision))
    return jnp.concatenate(re, axis=1), jnp.concatenate(im, axis=1)


def _ssm_out(x_re, x_im, wc_re_ref, wc_im_ref, precision=None):
    halves, nh, _ = wc_re_ref.shape
    ys = []
    for i in range(halves):
        ys.append(jnp.dot(x_re[:, i * nh:(i + 1) * nh], wc_re_ref[i], preferred_element_type=F32, precision=precision)
                  + jnp.dot(x_im[:, i * nh:(i + 1) * nh], wc_im_ref[i], preferred_element_type=F32,
                            precision=precision))
    return jnp.concatenate(ys, axis=1)


def _glu_tail(y, u32, dskip_ref, wglu_ref, bglu_ref, sgb, sw, precision=None):
    y = y + dskip_ref[...] * u32
    za = jax.nn.gelu(y).astype(wglu_ref.dtype)
    glu = jnp.dot(za, wglu_ref[...], preferred_element_type=F32, precision=precision) + bglu_ref[...]
    return glu[:, :sw] * jax.nn.sigmoid(glu[:, sw:]) * sgb


def _s5_prompt_kernel(u_ref, sgb_ref, are_ref, aim_ref, coef_re_ref, coef_im_ref, bt_re_ref, bt_im_ref,
                      ct_re_ref, ct_im_ref, dskip_ref, wglu_ref, bglu_ref,
                      out_ref, fre_ref, fim_ref,
                      wb_re_ref, wb_im_ref, wc_re_ref, wc_im_ref, bu_re_ref, bu_im_ref, xs_re_ref, xs_im_ref,
                      st_re_ref, st_im_ref, *, gch, nstate, sw):
    b = pl.program_id(0)
    t = pl.program_id(1)
    tl = u_ref.shape[1]

    @pl.when((b == 0) & (t == 0))
    def _():
        _build_ssm_weights(coef_re_ref, coef_im_ref, bt_re_ref, bt_im_ref, ct_re_ref, ct_im_ref,
                           wb_re_ref, wb_im_ref, wc_re_ref, wc_im_ref, gch=gch, nstate=nstate)

    @pl.when(t == 0)
    def _():
        st_re_ref[...] = jnp.zeros_like(st_re_ref)
        st_im_ref[...] = jnp.zeros_like(st_im_ref)

    u = u_ref[0]
    bu_re, bu_im = _ssm_in(u, wb_re_ref, wb_im_ref)
    bu_re_ref[...] = bu_re
    bu_im_ref[...] = bu_im
    a_re = are_ref[...]
    a_im = aim_ref[...]

    def step(i, carry):
        xr, xi = carry
        nr = a_re * xr - a_im * xi + bu_re_ref[pl.ds(i, 1), :]
        ni = a_re * xi + a_im * xr + bu_im_ref[pl.ds(i, 1), :]
        xs_re_ref[pl.ds(i, 1), :] = nr
        xs_im_ref[pl.ds(i, 1), :] = ni
        return nr, ni

    xr, xi = lax.fori_loop(0, tl, step, (st_re_ref[...], st_im_ref[...]), unroll=8)
    st_re_ref[...] = xr
    st_im_ref[...] = xi
    fre_ref[0] = xr
    fim_ref[0] = xi

    y = _ssm_out(xs_re_ref[...].astype(BF16), xs_im_ref[...].astype(BF16), wc_re_ref, wc_im_ref)
    out = _glu_tail(y, u.astype(F32), dskip_ref, wglu_ref, bglu_ref, sgb_ref[0].astype(F32), sw)
    out_ref[0] = out.astype(out_ref.dtype)


def _s5_prompt(u, sgb, a_re, a_im, coef_re, coef_im, bt_re, bt_im, ct_re, ct_im, d_skip, w_glu, b_glu, *, tl,
               gch, nstate):
    b, l, sw = u.shape
    ns = a_re.shape[1]
    halves = 2
    kh = sw // halves
    nh = ns // halves
    kern = functools.partial(_s5_prompt_kernel, gch=gch, nstate=nstate, sw=sw)
    tile = pl.BlockSpec((1, tl, sw), lambda bi, t: (bi, t, 0))
    c2 = lambda shape: _const_spec(shape, 2)
    return pl.pallas_call(
        kern,
        grid=(b, l // tl),
        in_specs=[tile, tile, c2((1, ns)), c2((1, ns)), c2((halves, 1, nh)), c2((halves, 1, nh)),
                  c2(bt_re.shape), c2(bt_im.shape), c2(ct_re.shape), c2(ct_im.shape),
                  c2((1, sw)), c2(w_glu.shape), c2((1, 2 * sw))],
        out_specs=[tile,
                   pl.BlockSpec((1, 1, ns), lambda bi, t: (bi, 0, 0)),
                   pl.BlockSpec((1, 1, ns), lambda bi, t: (bi, 0, 0))],
        out_shape=[jax.ShapeDtypeStruct((b, l, sw), BF16),
                   jax.ShapeDtypeStruct((b, 1, ns), F32),
                   jax.ShapeDtypeStruct((b, 1, ns), F32)],
        scratch_shapes=[
            pltpu.VMEM((halves, kh, nh), BF16), pltpu.VMEM((halves, kh, nh), BF16),
            pltpu.VMEM((halves, nh, kh), BF16), pltpu.VMEM((halves, nh, kh), BF16),
            pltpu.VMEM((tl, ns), F32), pltpu.VMEM((tl, ns), F32),
            pltpu.VMEM((tl, ns), F32), pltpu.VMEM((tl, ns), F32),
            pltpu.VMEM((1, ns), F32), pltpu.VMEM((1, ns), F32),
        ],
        compiler_params=_cparams(2),
        name="s5_prompt",
    )(u, sgb, a_re, a_im, coef_re, coef_im, bt_re, bt_im, ct_re, ct_im, d_skip, w_glu, b_glu)


def _s5_sample_kernel(u_ref, sgb_ref, sre_ref, sim_ref, are_ref, aim_ref, coef_re_ref, coef_im_ref,
                      bt_re_ref, bt_im_ref, ct_re_ref, ct_im_ref, dskip_ref, wglu_ref, bglu_ref,
                      out_ref, nre_ref, nim_ref, wb_re_ref, wb_im_ref, wc_re_ref, wc_im_ref, *, gch, nstate, sw):
    _build_ssm_weights(coef_re_ref, coef_im_ref, bt_re_ref, bt_im_ref, ct_re_ref, ct_im_ref,
                       wb_re_ref, wb_im_ref, wc_re_ref, wc_im_ref, gch=gch, nstate=nstate)
    u = u_ref[...]
    bu_re, bu_im = _ssm_in(u, wb_re_ref, wb_im_ref, precision=HIGHEST)
    a_re = are_ref[...]
    a_im = aim_ref[...]
    s_re = sre_ref[...]
    s_im = sim_ref[...]
    x_re = a_re * s_re - a_im * s_im + bu_re
    x_im = a_re * s_im + a_im * s_re + bu_im
    nre_ref[...] = x_re
    nim_ref[...] = x_im
    y = _ssm_out(x_re, x_im, wc_re_ref, wc_im_ref, precision=HIGHEST)
    out_ref[...] = _glu_tail(y, u, dskip_ref, wglu_ref, bglu_ref, sgb_ref[...], sw, precision=HIGHEST)


def _s5_sample(u, sgb, s_re, s_im, a_re, a_im, coef_re, coef_im, bt_re, bt_im, ct_re, ct_im, d_skip, w_glu,
               b_glu, *, gch, nstate):
    n, sw = u.shape
    ns = a_re.shape[1]
    halves = 2
    kh = sw // halves
    nh = ns // halves
    kern = functools.partial(_s5_sample_kernel, gch=gch, nstate=nstate, sw=sw)
    return pl.pallas_call(
        kern,
        out_shape=[jax.ShapeDtypeStruct((n, sw), F32),
                   jax.ShapeDtypeStruct((n, ns), F32),
                   jax.ShapeDtypeStruct((n, ns), F32)],
        scratch_shapes=[
            pltpu.VMEM((halves, kh, nh), F32), pltpu.VMEM((halves, kh, nh), F32),
            pltpu.VMEM((halves, nh, kh), F32), pltpu.VMEM((halves, nh, kh), F32),
        ],
        compiler_params=pltpu.CompilerParams(vmem_limit_bytes=VMEM_LIMIT),
        name="s5_sample",
    )(u, sgb, s_re, s_im, a_re, a_im, coef_re, coef_im, bt_re, bt_im, ct_re, ct_im, d_skip, w_glu, b_glu)


def _proj_out_kernel(x_ref, attn_ref, sga_ref, ssm_ref, sma_ref, smb_ref, wa_ref, wb_ref, wo_ref, o_ref):
    attn_out = (attn_ref[...].astype(F32) * sga_ref[...].astype(F32)).astype(BF16)
    ya = jnp.dot(attn_out, wa_ref[...], preferred_element_type=F32)
    yb = jnp.dot(ssm_ref[...].astype(BF16), wb_ref[...], preferred_element_type=F32)
    merged = sma_ref[...].astype(F32) * ya + smb_ref[...].astype(F32) * yb
    o_ref[...] = x_ref[...] + jnp.dot(merged.astype(BF16), wo_ref[...], preferred_element_type=F32)


def _proj_out(x2, attn, sga, ssm, sma, smb, wa, wb, wo, *, tm):
    t, dm = x2.shape
    aw = attn.shape[1]
    sw = ssm.shape[1]
    row = lambda w: pl.BlockSpec((tm, w), lambda i: (i, 0))
    return pl.pallas_call(
        _proj_out_kernel,
        grid=(t // tm,),
        in_specs=[row(dm), row(aw), row(aw), row(sw), row(dm), row(dm),
                  _const_spec(wa.shape, 1), _const_spec(wb.shape, 1), _const_spec(wo.shape, 1)],
        out_specs=row(dm),
        out_shape=jax.ShapeDtypeStruct((t, dm), F32),
        compiler_params=_cparams(1),
        name="proj_out",
    )(x2, attn, sga, ssm, sma, smb, wa, wb, wo)


def _col_from_row(row, n):
    eye = lax.broadcasted_iota(jnp.int32, (n, n), 0) == lax.broadcasted_iota(jnp.int32, (n, n), 1)
    return jnp.sum(jnp.where(eye, jnp.broadcast_to(row, (n, n)), 0.0), axis=1, keepdims=True)


def _row_from_col(col, n):
    eye = lax.broadcasted_iota(jnp.int32, (n, n), 0) == lax.broadcasted_iota(jnp.int32, (n, n), 1)
    return jnp.sum(jnp.where(eye, jnp.broadcast_to(col, (n, n)), 0.0), axis=0, keepdims=True)


def _sample_attn_kernel(pt_ref, relb_ref, q_ref, qp_ref, knp_ref, vnp_ref, ck_ref, cv_ref, o_ref,
                        kbuf, ksem, qb_ref, lg_ref, tb_ref, selv_ref, sels_ref, selsem, vbuf, vsem,
                        *, n_seq, n_heads, head_dim, ppb, page, n_pages, n_ring, scale):
    step = pl.program_id(0)
    total = n_seq * n_pages
    n_blocks = n_pages // ppb
    past_len = n_pages * page
    n_sel = MOBA_TOPK
    cur = lax.rem(step, 2)
    prev = 1 - cur

    def k_copy(g):
        phys = pt_ref[lax.div(g, n_pages), lax.rem(g, n_pages)]
        slot = lax.rem(g, n_ring)
        return pltpu.make_async_copy(ck_ref.at[0, phys], kbuf.at[slot], ksem.at[slot])

    def sel_copy(par):
        return pltpu.make_async_copy(selv_ref.at[par], sels_ref.at[par], selsem)

    def chosen(par, h, r, i):
        j = sels_ref[par, h, r]
        pgl = jnp.maximum(j, 0) * ppb + i
        return j, pgl, (h * n_sel + r) * ppb + i

    def v_copy(b, par, h, r, i):
        _, pgl, slot = chosen(par, h, r, i)
        return pltpu.make_async_copy(cv_ref.at[0, pt_ref[b, pgl], h], vbuf.at[par, slot], vsem.at[par, slot])

    head_i = lax.broadcasted_iota(jnp.int32, (n_heads, page), 0)
    lane = lax.broadcasted_iota(jnp.int32, (n_heads, LANES), 1)

    @pl.when(step == 0)
    def _():
        for g in range(n_ring - 1):
            k_copy(jnp.int32(g)).start()
        for bk in range(REL_BUCKETS):
            tab = jnp.zeros((n_heads, page), F32)
            for h in range(n_heads):
                tab = jnp.where(head_i == h, relb_ref[bk, h], tab)
            tb_ref[bk] = tab

    @pl.when(step < n_seq)
    def _():
        q = q_ref[0]
        for h in range(n_heads):
            col = _col_from_row(q[:, h * head_dim:(h + 1) * head_dim], head_dim)
            qb_ref[h] = jnp.broadcast_to(col, (head_dim, page))

        def stream(pg, carry):
            g = step * n_pages + pg
            nxt = g + (n_ring - 1)

            @pl.when(nxt < total)
            def _():
                k_copy(nxt).start()

            k_copy(g).wait()
            kt = kbuf[lax.rem(g, n_ring)]
            lg_ref[cur, pg] = jnp.sum(kt * qb_ref[...], axis=1)
            return carry

        lax.fori_loop(0, n_pages, stream, 0)

        sc = jnp.full((n_heads, LANES), NEG_INF, F32)
        for jb in range(n_blocks):
            tot = lg_ref[cur, jb * ppb]
            for i in range(1, ppb):
                tot = tot + lg_ref[cur, jb * ppb + i]
            sc = jnp.where(lane == jb, jnp.sum(tot, axis=1, keepdims=True) * (1.0 / MOBA_BLOCK), sc)
        sel = jnp.full((n_heads, LANES), -1, jnp.int32)
        for r in range(n_sel):
            mx = jnp.max(sc, axis=1, keepdims=True)
            idx = jnp.min(jnp.where(sc == mx, lane, LANES), axis=1, keepdims=True)
            ok = mx > NEG_INF
            sel = jnp.where(lane == r, jnp.where(ok, idx, -1), sel)
            sc = jnp.where((lane == idx) & ok, NEG_INF, sc)
        selv_ref[cur] = sel
        sel_copy(cur).start()

    @pl.when(step >= 1)
    def _():
        b = step - 1
        q = qp_ref[0]
        sel = selv_ref[prev]
        pos = lax.broadcasted_iota(jnp.int32, (n_heads, page), 1)
        qk_new = q * knp_ref[0]
        s_self = jnp.zeros((n_heads, 1), F32)
        for h in range(n_heads):
            dot_h = jnp.sum(qk_new[:, h * head_dim:(h + 1) * head_dim], axis=1, keepdims=True)
            s_self = jnp.where(head_i[:, 0:1] == h, dot_h, s_self)
        s_self = s_self * scale + tb_ref[0][:, 0:1]
        tiles = []
        for r in range(n_sel):
            blk_r = jnp.sum(jnp.where(lane == r, sel, 0), axis=1, keepdims=True)
            for i in range(ppb):
                raw = jnp.zeros((n_heads, page), F32)
                for h in range(n_heads):
                    raw = jnp.where(head_i == h, lg_ref[prev, chosen(prev, h, r, i)[1]][h:h + 1, :], raw)
                bucket = _rel_bucket(past_len - (jnp.maximum(blk_r, 0) * ppb + i) * page - pos)
                bias = jnp.zeros((n_heads, page), F32)
                for bk in range(REL_BUCKETS):
                    bias = jnp.where(bucket == bk, tb_ref[bk], bias)
                tiles.append(jnp.where(blk_r >= 0, raw * scale + bias, NEG_INF))
        m = s_self
        for s in tiles:
            m = jnp.maximum(m, jnp.max(s, axis=1, keepdims=True))
        p_self = jnp.exp(s_self - m)
        den = p_self
        probs = []
        for s in tiles:
            p = jnp.exp(s - m)
            den = den + jnp.sum(p, axis=1, keepdims=True)
            probs.append(p)
        rows = []
        for h in range(n_heads):
            acc = jnp.zeros((head_dim, page), F32)
            for n, p in enumerate(probs):
                j, _, slot = chosen(prev, h, n // ppb, n % ppb)

                @pl.when(j >= 0)
                def _():
                    v_copy(b, prev, h, n // ppb, n % ppb).wait()

                acc = acc + jnp.where(j >= 0, vbuf[prev, slot], 0.0) * p[h:h + 1, :]
            o_row = _row_from_col(jnp.sum(acc, axis=1, keepdims=True), head_dim)
            v_self = vnp_ref[0][:, h * head_dim:(h + 1) * head_dim]
            rows.append((o_row + p_self[h:h + 1, :] * v_self) / den[h:h + 1, :])
        o_ref[0] = jnp.concatenate(rows, axis=1)

    @pl.when(step < n_seq)
    def _():
        sel_copy(cur).wait()
        for h in range(n_heads):
            for r in range(n_sel):
                for i in range(ppb):
                    @pl.when(chosen(cur, h, r, i)[0] >= 0)
                    def _():
                        v_copy(step, cur, h, r, i).start()


def _sample_attention(page_table, rel_bias, q, k_new, v_new, cache_kt, cache_vt, *, ppb, n_ring=32):
    bd, w = q.shape
    _, n_pool, n_heads, head_dim, page = cache_kt.shape
    n_pages = page_table.shape[1]
    assert n_pages % ppb == 0 and n_pages // ppb <= LANES and n_heads * head_dim == w
    n_slots = n_heads * MOBA_TOPK * ppb
    n_ring = min(n_ring, bd * n_pages)
    kern = functools.partial(_sample_attn_kernel, n_seq=bd, n_heads=n_heads, head_dim=head_dim, ppb=ppb, page=page,
                             n_pages=n_pages, n_ring=n_ring, scale=head_dim ** -0.5)
    row_cur = pl.BlockSpec((1, 1, w), lambda s, pt: (jnp.minimum(s, bd - 1), 0, 0))
    row_prev = pl.BlockSpec((1, 1, w), lambda s, pt: (jnp.maximum(s - 1, 0), 0, 0))
    grid_spec = pltpu.PrefetchScalarGridSpec(
        num_scalar_prefetch=1,
        grid=(bd + 1,),
        in_specs=[pl.BlockSpec(memory_space=pltpu.SMEM), row_cur, row_prev, row_prev, row_prev,
                  pl.BlockSpec(memory_space=pl.ANY), pl.BlockSpec(memory_space=pl.ANY)],
        out_specs=row_prev,
        scratch_shapes=[
            pltpu.VMEM((n_ring, n_heads, head_dim, page), F32),
            pltpu.SemaphoreType.DMA((n_ring,)),
            pltpu.VMEM((n_heads, head_dim, page), F32),
            pltpu.VMEM((2, n_pages, n_heads, page), F32),
            pltpu.VMEM((REL_BUCKETS, n_heads, page), F32),
            pltpu.VMEM((2, n_heads, LANES), jnp.int32),
            pltpu.SMEM((2, n_heads, LANES), jnp.int32),
            pltpu.SemaphoreType.DMA(()),
            pltpu.VMEM((2, n_slots, head_dim, page), F32),
            pltpu.SemaphoreType.DMA((2, n_slots)),
        ],
    )
    q3 = q.reshape(bd, 1, w)
    out = pl.pallas_call(
        kern,
        grid_spec=grid_spec,
        out_shape=jax.ShapeDtypeStruct((bd, 1, w), F32),
        compiler_params=_cparams(1),
        name="sample_attn",
    )(page_table, rel_bias, q3, q3, k_new.reshape(bd, 1, w), v_new.reshape(bd, 1, w), cache_kt, cache_vt)
    return out.reshape(bd, w)


def _layer(x_prompt, x_sample, cache_k, cache_v, state_re, state_im, page_table, rel_bias, norm_g, w_in,
           q_norm_g, k_norm_g, lam_re, lam_im, log_dt, b_re, b_im, c_re, c_im, d_skip, w_glu, b_glu,
           w_branch_a, w_branch_b, w_out):
    bsz, seq, dm = x_prompt.shape
    bd, s_new, _ = x_sample.shape
    _, n_pool, page, n_heads, head_dim = cache_k.shape
    n_groups, n_state, gch = b_re.shape
    aw = n_heads * head_dim
    sw = n_groups * gch
    ns = n_groups * n_state
    n_pages = page_table.shape[1]
    past_len = n_pages * page
    ppb = MOBA_BLOCK // page
    assert s_new == 1 and seq % MOBA_BLOCK == 0 and past_len % MOBA_BLOCK == 0
    assert w_in.shape == (dm, 4 * aw + 2 * sw + 2 * dm)

    w_in_bf = w_in.astype(BF16)
    g_row = norm_g.reshape(1, dm)
    qg_t = jnp.tile(q_norm_g.reshape(1, head_dim), (1, n_heads))
    kg_t = jnp.tile(k_norm_g.reshape(1, head_dim), (1, n_heads))
    head_of = jnp.arange(aw) // head_dim
    bd_ones = (head_of[:, None] == head_of[None, :]).astype(BF16)
    halves = 2
    bt_re = jnp.tile(b_re.transpose(0, 2, 1).reshape(sw, n_state), (1, n_groups // halves))
    bt_im = jnp.tile(b_im.transpose(0, 2, 1).reshape(sw, n_state), (1, n_groups // halves))
    ct_re = jnp.tile(c_re.transpose(0, 2, 1).reshape(ns, gch), (1, n_groups // halves))
    ct_im = jnp.tile(c_im.transpose(0, 2, 1).reshape(ns, gch), (1, n_groups // halves))
    dsk = d_skip.reshape(1, sw)
    bgl = b_glu.reshape(1, 2 * sw)

    a_re, a_im, coef_re, coef_im = _ssm_prep(lam_re, lam_im, log_dt)
    a_re = a_re.reshape(1, ns)
    a_im = a_im.reshape(1, ns)
    coef_re = coef_re.reshape(halves, 1, ns // halves)
    coef_im = coef_im.reshape(halves, 1, ns // halves)

    wa_bf = w_branch_a.astype(BF16)
    wb_bf = w_branch_b.astype(BF16)
    wo_bf = w_out.astype(BF16)
    wglu_bf = w_glu.astype(BF16)

    xp2 = x_prompt.reshape(bsz * seq, dm)
    (q_p, kb_p, ksum_p, kt_p, vt_p, vta_p, sga_p, u_p, sgb_p, sma_p, smb_p) = _proj_in(
        xp2, g_row, w_in_bf, qg_t, kg_t, bd_ones, tm=MOBA_BLOCK, aw=aw, sw=sw, head_dim=head_dim,
        q_dtype=BF16, act_dtype=BF16, prompt_batch=bsz)
    n_blocks = seq // MOBA_BLOCK
    bias = _bias_tiles(rel_bias, n_heads, _n_bias_tiles(n_blocks)[0])
    attn_p = _attention(q_p.reshape(bsz, seq, aw), kb_p.reshape(bsz, seq, aw), vta_p,
                        ksum_p.reshape(bsz, n_blocks, aw), bias, head_dim=head_dim, nh=ATTN_HEADS_PER_STEP)
    k_p = kt_p.reshape(bsz, n_heads, head_dim, seq).transpose(0, 3, 1, 2)
    v_p = vt_p.reshape(bsz, n_heads, head_dim, seq).transpose(0, 3, 1, 2)
    ssm_p, fre_p, fim_p = _s5_prompt(u_p.reshape(bsz, seq, sw), sgb_p.reshape(bsz, seq, sw), a_re, a_im, coef_re,
                                     coef_im, bt_re, bt_im, ct_re, ct_im, dsk, wglu_bf, bgl, tl=MOBA_BLOCK,
                                     gch=gch, nstate=n_state)
    y_p = _proj_out(xp2, attn_p.reshape(bsz * seq, aw), sga_p, ssm_p.reshape(bsz * seq, sw), sma_p, smb_p,
                    wa_bf, wb_bf, wo_bf, tm=math.gcd(bsz * seq, PROJ_OUT_ROWS))

    xs2 = x_sample.reshape(bd, dm)
    (q_s, k_s, v_s, sga_s, u_s, sgb_s, sma_s, smb_s) = _proj_in(
        xs2, g_row, w_in_bf, qg_t, kg_t, bd_ones, tm=bd, aw=aw, sw=sw, head_dim=head_dim,
        q_dtype=F32, act_dtype=F32)
    cache_kt = cache_k.transpose(0, 1, 3, 4, 2)
    cache_vt = cache_v.transpose(0, 1, 3, 4, 2)
    attn_s = _sample_attention(page_table, rel_bias, q_s, k_s, v_s, cache_kt, cache_vt, ppb=ppb)
    ssm_s, nre_s, nim_s = _s5_sample(u_s, sgb_s, state_re.reshape(bd, ns), state_im.reshape(bd, ns), a_re, a_im,
                                     coef_re, coef_im, bt_re, bt_im, ct_re, ct_im, dsk, w_glu, bgl, gch=gch,
                                     nstate=n_state)
    y_s = _proj_out(xs2, attn_s, sga_s, ssm_s, sma_s, smb_s, wa_bf, wb_bf, wo_bf, tm=bd)

    return (
        y_p.reshape(bsz, seq, dm), y_s.reshape(bd, 1, dm),
        k_p, v_p,
        fre_p.reshape(bsz, n_groups, n_state), fim_p.reshape(bsz, n_groups, n_state),
        k_s.reshape(bd, 1, n_heads, head_dim), v_s.reshape(bd, 1, n_heads, head_dim),
        nre_s.reshape(bd, n_groups, n_state), nim_s.reshape(bd, n_groups, n_state),
    )


def kernel(x_prompt, x_sample, cache_k, cache_v, state_ssm_re, state_ssm_im, page_table, rel_bias, norm_g, w_in,
           q_norm_g, k_norm_g, lam_re, lam_im, log_dt, b_re, b_im, c_re, c_im, d_skip, w_glu, b_glu, w_branch_a,
           w_branch_b, w_out):
    depth = norm_g.shape[0]
    assert depth == 1, "one layer per step"
    outs = _layer(x_prompt, x_sample, cache_k, cache_v, state_ssm_re[0], state_ssm_im[0], page_table,
                  rel_bias, norm_g[0], w_in[0], q_norm_g[0], k_norm_g[0], lam_re[0], lam_im[0], log_dt[0],
                  b_re[0], b_im[0], c_re[0], c_im[0], d_skip[0], w_glu[0], b_glu[0], w_branch_a[0],
                  w_branch_b[0], w_out[0])
    y_p, y_s, k_p, v_p, fre_p, fim_p, k_s, v_s, nre_s, nim_s = outs
    add = lambda a: a[None]
    return (y_p, y_s, add(k_p), add(v_p), add(fre_p), add(fim_p), add(k_s), add(v_s), add(nre_s), add(nim_s))
```
